```python
import math
import numpy as np
import jax
import jax.numpy as jnp
from jax import lax

D_MODEL = 2048
BATCH = 4
SEQ = 2048
DEPTH = 2

HEAD_DIM = 128
ROPE_THETA = 500000.0
ROPE_FRACTION = 4
Q_BLOCK = 128
EPS = 1e-6
NEG_INF = -1e30

DA_HEADS = 4
DA_QK_DIM = 64
DA_V_DIM = 2 * DA_QK_DIM

NSA_HEADS = 8
NSA_GROUPS = 2
NSA_HPG = NSA_HEADS // NSA_GROUPS
NSA_BLOCK = 64
NSA_TOP_N = 8
NSA_WINDOW = 512
NSA_FORCED_SCORE = 1e4

FOX_HEADS = 4
FOX_BIAS_INIT = 3.0

D_FF = 256 * ((8 * D_MODEL // 3 + 255) // 256)

DA_QK_COLS = DA_HEADS * 2 * DA_QK_DIM
DA_V_COLS = DA_HEADS * DA_V_DIM
NSA_Q_COLS = NSA_HEADS * HEAD_DIM
NSA_KV_COLS = 3 * 2 * NSA_GROUPS * HEAD_DIM
NSA_GATE_COLS = 3 * NSA_HEADS
FOX_COLS = FOX_HEADS * HEAD_DIM
MERGE_COLS = 3 * D_MODEL
IN_COLS = 2 * DA_QK_COLS + DA_V_COLS + NSA_Q_COLS + NSA_KV_COLS + NSA_GATE_COLS + 3 * FOX_COLS + FOX_HEADS + MERGE_COLS

kernel_name = 'hybrid_diff_nsa_fox_macaron'


def rms_norm(x, gain):
    xf = x.astype(jnp.float32)
    y = xf * lax.rsqrt(jnp.mean(xf * xf, axis=-1, keepdims=True) + EPS)
    return (y * gain.astype(jnp.float32)).astype(x.dtype)


def rope_cos_sin(positions, rot_dim):
    inv_freq = ROPE_THETA ** (-jnp.arange(0, rot_dim, 2, dtype=jnp.float32) / rot_dim)
    ang = positions.astype(jnp.float32)[..., None] * inv_freq
    return jnp.cos(ang)[:, :, None, :], jnp.sin(ang)[:, :, None, :]


def partial_rope(x, cos, sin):
    half = cos.shape[-1]
    xf = x.astype(jnp.float32)
    x1 = xf[..., :half]
    x2 = xf[..., half:2 * half]
    out = jnp.concatenate([x1 * cos - x2 * sin, x2 * cos + x1 * sin, xf[..., 2 * half:]], axis=-1)
    return out.astype(x.dtype)


def masked_softmax(s, mask):
    return jax.nn.softmax(jnp.where(mask, s.astype(jnp.float32), NEG_INF), axis=-1)


def causal_mask(lo, n, hi):
    return (lo + jnp.arange(n))[:, None] >= jnp.arange(hi)[None, :]


def swiglu(h, w_gate, w_up, w_down):
    return (jax.nn.silu(h @ w_gate) * (h @ w_up)) @ w_down


def diff_attention(q, k, v, lam, out_gain, lam_init):
    B, S, H, _, dq = q.shape
    scale = dq ** -0.5
    outs = []
    for c in range(S // Q_BLOCK):
        lo, hi = c * Q_BLOCK, (c + 1) * Q_BLOCK
        s = jnp.einsum('bqhmd,bkhmd->bhmqk', q[:, lo:hi], k[:, :hi]).astype(jnp.float32) * scale
        p = masked_softmax(s, causal_mask(lo, Q_BLOCK, hi))
        w = p[:, :, 0] - lam * p[:, :, 1]
        outs.append(jnp.einsum('bhqk,bkhd->bqhd', w.astype(v.dtype), v[:, :hi]))
    o = jnp.concatenate(outs, axis=1)
    o = rms_norm(o, out_gain) * (1.0 - lam_init)
    return o.reshape(B, S, H * o.shape[-1])


def forgetting_attention(q, k, v, log_f):
    B, S, H, d = q.shape
    scale = d ** -0.5
    cum = jnp.cumsum(log_f, axis=1).transpose(0, 2, 1)
    outs = []
    for c in range(S // Q_BLOCK):
        lo, hi = c * Q_BLOCK, (c + 1) * Q_BLOCK
        s = jnp.einsum('bqhd,bkhd->bhqk', q[:, lo:hi], k[:, :hi]).astype(jnp.float32) * scale
        s = s + cum[:, :, lo:hi, None] - cum[:, :, None, :hi]
        p = masked_softmax(s, causal_mask(lo, Q_BLOCK, hi))
        outs.append(jnp.einsum('bhqk,bkhd->bqhd', p.astype(v.dtype), v[:, :hi]))
    return jnp.concatenate(outs, axis=1).reshape(B, S, H * d)


def nsa_attention(q, k_c, v_c, k_s, v_s, k_w, v_w, gates, cmp_pos, cmp_w1, cmp_w2, k_cmp_gain):
    B, S, G, Hg, d = q.shape
    L = NSA_BLOCK
    NB = S // L
    W = NSA_WINDOW
    n_sel = min(NSA_TOP_N, NB)
    scale = d ** -0.5
    t = jnp.arange(S)

    def compress(tok, pos, w1, w2):
        blk = tok.reshape(B, NB, L, G, d) + pos[None, None, :, None, :]
        blk = blk.transpose(0, 1, 3, 2, 4).reshape(B, NB, G, L * d)
        return jax.nn.gelu(blk @ w1) @ w2

    kc = rms_norm(compress(k_c, cmp_pos[0], cmp_w1[0], cmp_w2[0]), k_cmp_gain)
    vc = compress(v_c, cmp_pos[1], cmp_w1[1], cmp_w2[1])
    blk_end = jnp.arange(NB) * L + L - 1
    cmp_mask = blk_end[None, :] <= t[:, None]
    s_c = jnp.einsum('bsghd,bngd->bsghn', q, kc).astype(jnp.float32) * scale
    p_c = masked_softmax(s_c, cmp_mask[:, None, None, :]) * cmp_mask[:, None, None, :]
    o_cmp = jnp.einsum('bsghn,bngd->bsghd', p_c.astype(vc.dtype), vc)

    importance = p_c.sum(axis=3)
    blk = jnp.arange(NB)[None, :]
    cur = (t // L)[:, None]
    valid = blk <= cur
    forced = (blk == 0) | (blk == cur) | (blk == cur - 1)
    score = jnp.where(forced[:, None, :], NSA_FORCED_SCORE,
                      jnp.where(valid[:, None, :], importance, -1.0))
    _, sel = lax.top_k(score, n_sel)

    ks_blk = k_s.reshape(B, NB, L, G, d).transpose(0, 3, 1, 2, 4)
    vs_blk = v_s.reshape(B, NB, L, G, d).transpose(0, 3, 1, 2, 4)
    gather = jax.vmap(jax.vmap(lambda table, i: table[i]))
    kw_pad = jnp.pad(k_w, ((0, 0), (W, 0), (0, 0), (0, 0)))
    vw_pad = jnp.pad(v_w, ((0, 0), (W, 0), (0, 0), (0, 0)))

    def chunk(c):
        lo = c * Q_BLOCK
        tq = lo + jnp.arange(Q_BLOCK)
        qc = lax.dynamic_slice_in_dim(q, lo, Q_BLOCK, axis=1)
        idx = lax.dynamic_slice_in_dim(sel, lo, Q_BLOCK, axis=1).transpose(0, 2, 1, 3)
        kg = gather(ks_blk, idx)
        vg = gather(vs_blk, idx)
        s_s = jnp.einsum('bqghd,bgqnld->bqghnl', qc, kg).astype(jnp.float32) * scale
        tok = idx[..., None] * L + jnp.arange(L)
        m_s = (tok <= tq[None, None, :, None, None]).transpose(0, 2, 1, 3, 4)
        p_s = masked_softmax(s_s.reshape(B, Q_BLOCK, G, Hg, n_sel * L),
                             m_s.reshape(B, Q_BLOCK, G, 1, n_sel * L))
        p_s = p_s.reshape(B, Q_BLOCK, G, Hg, n_sel, L)
        o_s = jnp.einsum('bqghnl,bgqnld->bqghd', p_s.astype(vg.dtype), vg)
        kwc = lax.dynamic_slice_in_dim(kw_pad, lo, W + Q_BLOCK, axis=1)
        vwc = lax.dynamic_slice_in_dim(vw_pad, lo, W + Q_BLOCK, axis=1)
        kpos = lo - W + jnp.arange(W + Q_BLOCK)
        m_w = ((kpos[None, :] <= tq[:, None]) & (kpos[None, :] > tq[:, None] - W)
               & (kpos[None, :] >= 0))
        s_w = jnp.einsum('bqghd,bkgd->bqghk', qc, kwc).astype(jnp.float32) * scale
        p_w = masked_softmax(s_w, m_w[:, None, None, :])
        o_w = jnp.einsum('bqghk,bkgd->bqghd', p_w.astype(vwc.dtype), vwc)
        return o_s, o_w

    o_s, o_w = lax.map(chunk, jnp.arange(S // Q_BLOCK))
    o_s = jnp.moveaxis(o_s, 0, 1).reshape(B, S, G, Hg, d)
    o_w = jnp.moveaxis(o_w, 0, 1).reshape(B, S, G, Hg, d)
    o = gates[..., 0:1] * o_cmp + gates[..., 1:2] * o_s + gates[..., 2:3] * o_w
    return o.reshape(B, S, G * Hg * d)


def token_mixer(h, cos_a, sin_a, cos_b, sin_b, lam_init, w_in, da_q_norm, da_k_norm, da_lambda,
                da_out_norm, nsa_q_norm, nsa_k_norm, nsa_cmp_pos, nsa_cmp_w1, nsa_cmp_w2,
                fox_q_norm, fox_k_norm, fox_f_bias, w_branch_a, w_branch_b, w_branch_c, w_out):
    B, S, _ = h.shape
    widths = [DA_QK_COLS, DA_QK_COLS, DA_V_COLS, NSA_Q_COLS, NSA_KV_COLS, NSA_GATE_COLS,
              FOX_COLS, FOX_COLS, FOX_COLS, FOX_HEADS, MERGE_COLS]
    (a_q, a_k, a_v, b_q, b_kv, b_g, c_q, c_k, c_v, c_f, g_m) = jnp.split(
        h @ w_in, np.cumsum(widths)[:-1].tolist(), axis=-1)

    qa = partial_rope(rms_norm(a_q.reshape(B, S, 2 * DA_HEADS, DA_QK_DIM), da_q_norm), cos_a, sin_a)
    ka = partial_rope(rms_norm(a_k.reshape(B, S, 2 * DA_HEADS, DA_QK_DIM), da_k_norm), cos_a, sin_a)
    qa = qa.reshape(B, S, DA_HEADS, 2, DA_QK_DIM)
    ka = ka.reshape(B, S, DA_HEADS, 2, DA_QK_DIM)
    va = a_v.reshape(B, S, DA_HEADS, DA_V_DIM)
    lp = da_lambda.astype(jnp.float32)
    lam = jnp.exp(jnp.sum(lp[0] * lp[1])) - jnp.exp(jnp.sum(lp[2] * lp[3])) + lam_init
    o_a = diff_attention(qa, ka, va, lam, da_out_norm, lam_init)

    qb = partial_rope(rms_norm(b_q.reshape(B, S, NSA_HEADS, HEAD_DIM), nsa_q_norm), cos_b, sin_b)
    qb = qb.reshape(B, S, NSA_GROUPS, NSA_HPG, HEAD_DIM)
    kv = b_kv.reshape(B, S, 3, 2, NSA_GROUPS, HEAD_DIM)
    k_cmp = partial_rope(kv[:, :, 0, 0], cos_b, sin_b)
    k_slc = partial_rope(rms_norm(kv[:, :, 1, 0], nsa_k_norm[1]), cos_b, sin_b)
    k_win = partial_rope(rms_norm(kv[:, :, 2, 0], nsa_k_norm[2]), cos_b, sin_b)
    gates = jax.nn.sigmoid(b_g.reshape(B, S, NSA_GROUPS, NSA_HPG, 3))
    o_b = nsa_attention(qb, k_cmp, kv[:, :, 0, 1], k_slc, kv[:, :, 1, 1], k_win, kv[:, :, 2, 1],
                        gates, nsa_cmp_pos, nsa_cmp_w1, nsa_cmp_w2, nsa_k_norm[0])

    qc = rms_norm(c_q.reshape(B, S, FOX_HEADS, HEAD_DIM), fox_q_norm)
    kc = rms_norm(c_k.reshape(B, S, FOX_HEADS, HEAD_DIM), fox_k_norm)
    vc = c_v.reshape(B, S, FOX_HEADS, HEAD_DIM)
    log_f = jax.nn.log_sigmoid(c_f.astype(jnp.float32) + fox_f_bias.astype(jnp.float32))
    o_c = forgetting_attention(qc, kc, vc, log_f)

    gm = jax.nn.sigmoid(g_m).reshape(B, S, 3, D_MODEL)
    y = (gm[:, :, 0] * (o_a @ w_branch_a) + gm[:, :, 1] * (o_b @ w_branch_b)
         + gm[:, :, 2] * (o_c @ w_branch_c))
    return y @ w_out


def setup_inputs(seed: int = 0) -> dict:
    key = jax.random.key(seed)
    keys = iter(jax.random.split(key, 40))

    def nrm(shape, scale):
        return jax.random.normal(next(keys), shape, jnp.float32) * scale

    def gain(shape):
        return 1.0 + nrm(shape, 0.02)

    D, F, L, d = D_MODEL, D_FF, NSA_BLOCK, HEAD_DIM
    x = jax.random.normal(next(keys), (BATCH, SEQ, D), jnp.float32)
    offsets = jax.random.randint(next(keys), (BATCH, 1), 0, 1024)
    positions = (jnp.arange(SEQ)[None, :] + offsets).astype(jnp.int32)
    return {
        'x': x,
        'positions': positions,
        'ffn1_norm': gain((DEPTH, D)),
        'ffn1_w_gate': nrm((DEPTH, D, F), D ** -0.5),
        'ffn1_w_up': nrm((DEPTH, D, F), D ** -0.5),
        'ffn1_w_down': nrm((DEPTH, F, D), F ** -0.5),
        'mix_norm': gain((DEPTH, D)),
        'w_in': nrm((DEPTH, D, IN_COLS), D ** -0.5),
        'da_q_norm': gain((DEPTH, DA_QK_DIM)),
        'da_k_norm': gain((DEPTH, DA_QK_DIM)),
        'da_lambda': nrm((DEPTH, 4, DA_QK_DIM), 0.1),
        'da_out_norm': gain((DEPTH, DA_V_DIM)),
        'nsa_q_norm': gain((DEPTH, d)),
        'nsa_k_norm': gain((DEPTH, 3, d)),
        'nsa_cmp_pos': nrm((DEPTH, 2, L, d), 0.02),
        'nsa_cmp_w1': nrm((DEPTH, 2, L * d, d), (L * d) ** -0.5),
        'nsa_cmp_w2': nrm((DEPTH, 2, d, d), d ** -0.5),
        'fox_q_norm': gain((DEPTH, d)),
        'fox_k_norm': gain((DEPTH, d)),
        'fox_f_bias': FOX_BIAS_INIT + nrm((DEPTH, FOX_HEADS), 0.5),
        'w_branch_a': nrm((DEPTH, DA_V_COLS, D), DA_V_COLS ** -0.5),
        'w_branch_b': nrm((DEPTH, NSA_Q_COLS, D), NSA_Q_COLS ** -0.5),
        'w_branch_c': nrm((DEPTH, FOX_COLS, D), FOX_COLS ** -0.5),
        'w_out': nrm((DEPTH, D, D), D ** -0.5),
        'ffn2_norm': gain((DEPTH, D)),
        'ffn2_w_gate': nrm((DEPTH, D, F), D ** -0.5),
        'ffn2_w_up': nrm((DEPTH, D, F), D ** -0.5),
        'ffn2_w_down': nrm((DEPTH, F, D), F ** -0.5),
    }


def reference(x, positions, ffn1_norm, ffn1_w_gate, ffn1_w_up, ffn1_w_down, mix_norm, w_in,
              da_q_norm, da_k_norm, da_lambda, da_out_norm, nsa_q_norm, nsa_k_norm, nsa_cmp_pos,
              nsa_cmp_w1, nsa_cmp_w2, fox_q_norm, fox_k_norm, fox_f_bias, w_branch_a, w_branch_b,
              w_branch_c, w_out, ffn2_norm, ffn2_w_gate, ffn2_w_up, ffn2_w_down):
    cos_a, sin_a = rope_cos_sin(positions, DA_QK_DIM // ROPE_FRACTION)
    cos_b, sin_b = rope_cos_sin(positions, HEAD_DIM // ROPE_FRACTION)
    for l in range(DEPTH):
        lam_init = 0.8 - 0.6 * math.exp(-0.3 * l)
        x = x + 0.5 * swiglu(rms_norm(x, ffn1_norm[l]), ffn1_w_gate[l], ffn1_w_up[l], ffn1_w_down[l])
        x = x + token_mixer(rms_norm(x, mix_norm[l]), cos_a, sin_a, cos_b, sin_b, lam_init, w_in[l],
                            da_q_norm[l], da_k_norm[l], da_lambda[l], da_out_norm[l],
                            nsa_q_norm[l], nsa_k_norm[l], nsa_cmp_pos[l], nsa_cmp_w1[l], nsa_cmp_w2[l],
                            fox_q_norm[l], fox_k_norm[l], fox_f_bias[l],
                            w_branch_a[l], w_branch_b[l], w_branch_c[l], w_out[l])
        x = x + 0.5 * swiglu(rms_norm(x, ffn2_norm[l]), ffn2_w_gate[l], ffn2_w_up[l], ffn2_w_down[l])
    return x
```

```python
import functools
import math

import jax
import jax.numpy as jnp
from jax import lax
from jax.experimental import pallas as pl
from jax.experimental.pallas import tpu as pltpu

F32 = jnp.float32
BF16 = jnp.bfloat16

HEAD_DIM = 128
LANES = 128
ROPE_THETA = 500000.0
ROPE_FRACTION = 4
EPS = 1e-6
NEG_INF = -1e30

DA_HEADS = 4
DA_QK_DIM = 64
NSA_HEADS = 8
NSA_GROUPS = 2
NSA_HPG = NSA_HEADS // NSA_GROUPS
NSA_BLOCK = 64
BLOCK_SHIFT = 6
NSA_TOP_N = 8
NSA_WINDOW = 512
NSA_FORCED_SCORE = 1e4
FOX_HEADS = 4

BLK_AQ, BLK_AK, BLK_AV, BLK_BQ, BLK_BKV, BLK_CQ, BLK_CK, BLK_CV = 0, 4, 8, 12, 20, 32, 36, 40
QKV_COLS = 44 * LANES
SMALL_COLS = 3 * LANES

G_DAQ, G_DAK, G_NQ, G_NK0, G_NK1, G_NK2, G_FQ, G_FK, G_FBIAS, G_DAOUT = range(10)

VMEM_LIMIT = 56 * 1024 * 1024


def _cparams(sem):
    return pltpu.CompilerParams(dimension_semantics=sem, vmem_limit_bytes=VMEM_LIMIT)


def _dot(a, b):
    return jnp.dot(a, b, preferred_element_type=F32)


def _dot_t(a, b):
    return lax.dot_general(a, b, (((1,), (1,)), ((), ())), preferred_element_type=F32)


def _rms_rows(xf, gain):
    return xf * lax.rsqrt(jnp.mean(xf * xf, axis=-1, keepdims=True) + EPS) * gain


def _ffn_kernel(x_ref, g_ref, wg_ref, wu_ref, wd_ref, o_ref, h_scr, acc_scr):
    j = pl.program_id(1)

    @pl.when(j == 0)
    def _():
        h_scr[...] = _rms_rows(x_ref[...], g_ref[...]).astype(BF16)
        acc_scr[...] = jnp.zeros_like(acc_scr)

    h = h_scr[...]
    g = _dot(h, wg_ref[...])
    u = _dot(h, wu_ref[...])
    a = (g * jax.nn.sigmoid(g)) * u
    acc_scr[...] += _dot(a.astype(BF16), wd_ref[...])

    @pl.when(j == pl.num_programs(1) - 1)
    def _():
        o_ref[...] = x_ref[...] + 0.5 * acc_scr[...]


def _ffn(x2d, gain, wg, wu, wd, *, tm=512, tf=512):
    T, D = x2d.shape
    F = wg.shape[1]
    return pl.pallas_call(
        _ffn_kernel,
        grid=(T // tm, F // tf),
        in_specs=[
            pl.BlockSpec((tm, D), lambda i, j: (i, 0)),
            pl.BlockSpec((1, D), lambda i, j: (0, 0)),
            pl.BlockSpec((D, tf), lambda i, j: (0, j)),
            pl.BlockSpec((D, tf), lambda i, j: (0, j)),
            pl.BlockSpec((tf, D), lambda i, j: (j, 0)),
        ],
        out_specs=pl.BlockSpec((tm, D), lambda i, j: (i, 0)),
        out_shape=jax.ShapeDtypeStruct((T, D), F32),
        scratch_shapes=[pltpu.VMEM((tm, D), BF16), pltpu.VMEM((tm, D), F32)],
        compiler_params=_cparams(("parallel", "arbitrary")),
        name="ffn",
    )(x2d, gain, wg, wu, wd)


def _proj_kernel(x_ref, g_ref, w_ref, o_ref, h_scr, *, act):
    @pl.when(pl.program_id(1) == 0)
    def _():
        h_scr[...] = _rms_rows(x_ref[...], g_ref[...]).astype(BF16)

    r = _dot(h_scr[...], w_ref[...])
    if act:
        r = jax.nn.sigmoid(r)
    o_ref[...] = r.astype(o_ref.dtype)


def _proj(x2d, gain, w, *, act, name, tm=512, tn=512):
    T, D = x2d.shape
    N = w.shape[1]
    return pl.pallas_call(
        functools.partial(_proj_kernel, act=act),
        grid=(T // tm, N // tn),
        in_specs=[
            pl.BlockSpec((tm, D), lambda i, j: (i, 0)),
            pl.BlockSpec((1, D), lambda i, j: (0, 0)),
            pl.BlockSpec((D, tn), lambda i, j: (0, j)),
        ],
        out_specs=pl.BlockSpec((tm, tn), lambda i, j: (i, j)),
        out_shape=jax.ShapeDtypeStruct((T, N), BF16),
        scratch_shapes=[pltpu.VMEM((tm, D), BF16)],
        compiler_params=_cparams(("parallel", "arbitrary")),
        name=name,
    )(x2d, gain, w)


def _split3(v):
    hi = v.astype(BF16)
    r = v - hi.astype(F32)
    mid = r.astype(BF16)
    lo = (r - mid.astype(F32)).astype(BF16)
    return hi, mid, lo


def _proj_small_kernel(x_ref, g_ref, whi_ref, wlo_ref, o_ref):
    h = _rms_rows(x_ref[...], g_ref[...])
    hi, mid, _ = _split3(h)
    whi = whi_ref[...]
    o_ref[...] = _dot(hi, whi) + (_dot(mid, whi) + _dot(hi, wlo_ref[...]))


def _proj_small(x2d, gain, whi, wlo, *, tm=512):
    T, D = x2d.shape
    N = whi.shape[1]
    return pl.pallas_call(
        _proj_small_kernel,
        grid=(T // tm,),
        in_specs=[
            pl.BlockSpec((tm, D), lambda i: (i, 0)),
            pl.BlockSpec((1, D), lambda i: (0, 0)),
            pl.BlockSpec((D, N), lambda i: (0, 0)),
            pl.BlockSpec((D, N), lambda i: (0, 0)),
        ],
        out_specs=pl.BlockSpec((tm, N), lambda i: (i, 0)),
        out_shape=jax.ShapeDtypeStruct((T, N), F32),
        compiler_params=_cparams(("parallel",)),
        name="proj_small",
    )(x2d, gain, whi, wlo)


def _prep_kernel(p_ref, sm_ref, rope_ref, gains_ref,
                 qa_ref, ka_ref, qb_ref, ks_ref, kw_ref, qc_ref, kf_ref, kcmp_ref, vcmp_ref, cum_ref,
                 carry_scr, *, tiles_per_seq):
    i = pl.program_id(0)
    tm = p_ref.shape[0]
    lane = lax.broadcasted_iota(jnp.int32, (tm, LANES), 1)
    low_half = lane < DA_QK_DIM

    def chunk(c):
        return p_ref[:, c * LANES:(c + 1) * LANES].astype(F32)

    def gain(r):
        return gains_ref[r:r + 1, :]

    def norm128(xc, r):
        return _rms_rows(xc, gain(r))

    def norm64(xc, r):
        sq = xc * xc
        tot = jnp.sum(sq, axis=-1, keepdims=True)
        lo = jnp.sum(jnp.where(low_half, sq, 0.0), axis=-1, keepdims=True)
        ms = jnp.where(low_half, lo, tot - lo) * (1.0 / DA_QK_DIM)
        return xc * lax.rsqrt(ms + EPS) * gain(r)

    def rope(y, t, half):
        c, s1, s2 = rope_ref[t], rope_ref[t + 1], rope_ref[t + 2]
        return y * c + pltpu.roll(y, LANES - half, 1) * s1 + pltpu.roll(y, half, 1) * s2

    half_a = DA_QK_DIM // ROPE_FRACTION // 2
    half_b = HEAD_DIM // ROPE_FRACTION // 2
    scale_a = DA_QK_DIM ** -0.5
    scale_b = HEAD_DIM ** -0.5

    for h in range(DA_HEADS):
        sl = slice(h * LANES, (h + 1) * LANES)
        qa_ref[:, sl] = (rope(norm64(chunk(BLK_AQ + h), G_DAQ), 0, half_a) * scale_a).astype(BF16)
        ka_ref[:, sl] = rope(norm64(chunk(BLK_AK + h), G_DAK), 0, half_a).astype(BF16)
    for h in range(NSA_HEADS):
        sl = slice(h * LANES, (h + 1) * LANES)
        qb_ref[:, sl] = (rope(norm128(chunk(BLK_BQ + h), G_NQ), 3, half_b) * scale_b).astype(BF16)
    for g in range(NSA_GROUPS):
        sl = slice(g * LANES, (g + 1) * LANES)
        kcmp_ref[g] = rope(chunk(BLK_BKV + 0 + g), 3, half_b).astype(BF16)
        vcmp_ref[g] = p_ref[:, (BLK_BKV + 2 + g) * LANES:(BLK_BKV + 3 + g) * LANES]
        ks_ref[:, sl] = rope(norm128(chunk(BLK_BKV + 4 + g), G_NK1), 3, half_b).astype(BF16)
        kw_ref[:, sl] = rope(norm128(chunk(BLK_BKV + 8 + g), G_NK2), 3, half_b).astype(BF16)
    for h in range(FOX_HEADS):
        sl = slice(h * LANES, (h + 1) * LANES)
        qc_ref[:, sl] = (norm128(chunk(BLK_CQ + h), G_FQ) * scale_b).astype(BF16)
        kf_ref[:, sl] = norm128(chunk(BLK_CK + h), G_FK).astype(BF16)

    @pl.when(i % tiles_per_seq == 0)
    def _():
        carry_scr[...] = jnp.zeros_like(carry_scr)

    z = sm_ref[...] + gain(G_FBIAS)
    logf = jnp.minimum(z, 0.0) - jnp.log(1.0 + jnp.exp(-jnp.abs(z)))
    r_id = lax.broadcasted_iota(jnp.int32, (tm, tm), 0)
    c_id = lax.broadcasted_iota(jnp.int32, (tm, tm), 1)
    tri = jnp.where(r_id >= c_id, 1.0, 0.0).astype(BF16)
    hi, mid, lo = _split3(logf)
    cum = _dot(tri, hi) + (_dot(tri, mid) + _dot(tri, lo)) + carry_scr[0:1, :]
    cum_ref[...] = cum
    carry_scr[0:1, :] = cum[tm - 1:tm, :]


def _prep(p_qkv, small, rope_tab, gains, *, seq, tm=256):
    T = p_qkv.shape[0]
    G = NSA_GROUPS

    def rows(w):
        return pl.BlockSpec((tm, w), lambda i: (i, 0))

    out_shape = (
        jax.ShapeDtypeStruct((T, 4 * LANES), BF16),
        jax.ShapeDtypeStruct((T, 4 * LANES), BF16),
        jax.ShapeDtypeStruct((T, 8 * LANES), BF16),
        jax.ShapeDtypeStruct((T, 2 * LANES), BF16),
        jax.ShapeDtypeStruct((T, 2 * LANES), BF16),
        jax.ShapeDtypeStruct((T, 4 * LANES), BF16),
        jax.ShapeDtypeStruct((T, 4 * LANES), BF16),
        jax.ShapeDtypeStruct((G, T, LANES), BF16),
        jax.ShapeDtypeStruct((G, T, LANES), BF16),
        jax.ShapeDtypeStruct((T, LANES), F32),
    )
    out_specs = (
        rows(4 * LANES), rows(4 * LANES), rows(8 * LANES), rows(2 * LANES), rows(2 * LANES),
        rows(4 * LANES), rows(4 * LANES),
        pl.BlockSpec((G, tm, LANES), lambda i: (0, i, 0)),
        pl.BlockSpec((G, tm, LANES), lambda i: (0, i, 0)),
        rows(LANES),
    )
    return pl.pallas_call(
        functools.partial(_prep_kernel, tiles_per_seq=seq // tm),
        grid=(T // tm,),
        in_specs=[
            rows(QKV_COLS),
            rows(LANES),
            pl.BlockSpec((6, tm, LANES), lambda i: (0, i, 0)),
            pl.BlockSpec(gains.shape, lambda i: (0, 0)),
        ],
        out_specs=out_specs,
        out_shape=out_shape,
        scratch_shapes=[pltpu.VMEM((8, LANES), F32)],
        compiler_params=_cparams(("arbitrary",)),
        name="prep",
    )(p_qkv, small, rope_tab, gains)


def _compress_kernel(x_ref, pos_ref, w1_ref, w2_ref, gain_ref, o_ref):
    kv = pl.program_id(0)
    blk = (x_ref[...].astype(F32) + pos_ref[...]).astype(BF16)
    hid = jax.nn.gelu(_dot(blk, w1_ref[...]))
    out = _dot(hid.astype(BF16), w2_ref[...])

    @pl.when(kv == 0)
    def _():
        o_ref[...] = _rms_rows(out, gain_ref[...]).astype(BF16)

    @pl.when(kv != 0)
    def _():
        o_ref[...] = out.astype(BF16)


def _compress(xkv, pos, w1, w2, gain):
    _, G, R, K = xkv.shape
    d = w1.shape[-1]
    return pl.pallas_call(
        _compress_kernel,
        grid=(2, G),
        in_specs=[
            pl.BlockSpec((None, None, R, K), lambda kv, g: (kv, g, 0, 0)),
            pl.BlockSpec((None, 1, K), lambda kv, g: (kv, 0, 0)),
            pl.BlockSpec((None, K, d), lambda kv, g: (kv, 0, 0)),
            pl.BlockSpec((None, d, d), lambda kv, g: (kv, 0, 0)),
            pl.BlockSpec((1, d), lambda kv, g: (0, 0)),
        ],
        out_specs=pl.BlockSpec((None, None, R, d), lambda kv, g: (kv, g, 0, 0)),
        out_shape=jax.ShapeDtypeStruct((2, G, R, d), BF16),
        compiler_params=_cparams(("arbitrary", "arbitrary")),
        name="compress",
    )(xkv, pos, w1, w2, gain)


def _online(s, mask, m, l, acc, v):
    s = jnp.where(mask, s, NEG_INF)
    m_new = jnp.maximum(m, jnp.max(s, axis=-1, keepdims=True))
    alpha = jnp.exp(m - m_new)
    p = jnp.where(mask, jnp.exp(s - m_new), 0.0)
    l_new = alpha * l + jnp.sum(p, axis=-1, keepdims=True)
    acc_new = alpha * acc + _dot(p.astype(BF16), v)
    return m_new, l_new, acc_new


def _softmax_init(tq, d):
    return (jnp.full((tq, 1), NEG_INF, F32), jnp.zeros((tq, 1), F32), jnp.zeros((tq, d), F32))


def _diff_kernel(q_ref, k_ref, v_ref, lam_ref, gains_ref, o_ref, *, tq, tk, lam_init):
    i = pl.program_id(2)
    lo = i * tq
    q = q_ref[...]
    lane = lax.broadcasted_iota(jnp.int32, (tq, LANES), 1)
    q0 = jnp.where(lane < DA_QK_DIM, q, jnp.zeros_like(q))
    q1 = jnp.where(lane < DA_QK_DIM, jnp.zeros_like(q), q)
    trow = lo + lax.broadcasted_iota(jnp.int32, (tq, tk), 0)
    kcol = lax.broadcasted_iota(jnp.int32, (tq, tk), 1)

    def body(kt, carry):
        s0c, s1c = carry
        off = pl.multiple_of(kt * tk, tk)
        k = k_ref[pl.ds(off, tk), :]
        v = v_ref[pl.ds(off, tk), :]
        mask = trow >= kcol + off
        s0c = _online(_dot_t(q0, k), mask, *s0c, v)
        s1c = _online(_dot_t(q1, k), mask, *s1c, v)
        return s0c, s1c

    nkt = (lo + tq + tk - 1) // tk
    (m0, l0, a0), (m1, l1, a1) = lax.fori_loop(
        0, nkt, body, (_softmax_init(tq, LANES), _softmax_init(tq, LANES)))

    lp = lam_ref[...]
    lam = (jnp.exp(jnp.sum(lp[0:1] * lp[1:2], axis=-1, keepdims=True))
           - jnp.exp(jnp.sum(lp[2:3] * lp[3:4], axis=-1, keepdims=True)) + lam_init)
    o = a0 / l0 - lam * (a1 / l1)
    o = _rms_rows(o, gains_ref[G_DAOUT:G_DAOUT + 1, :]) * (1.0 - lam_init)
    o_ref[...] = o.astype(BF16)


def _diff_attention(qa, ka, p_qkv, da_lambda, gains, *, batch, seq, lam_init, tq=128, tk=128):
    T = qa.shape[0]
    nq = seq // tq
    return pl.pallas_call(
        functools.partial(_diff_kernel, tq=tq, tk=tk, lam_init=lam_init),
        grid=(batch, DA_HEADS, nq),
        in_specs=[
            pl.BlockSpec((tq, LANES), lambda b, h, i: (b * nq + i, h)),
            pl.BlockSpec((seq, LANES), lambda b, h, i: (b, h)),
            pl.BlockSpec((seq, LANES), lambda b, h, i: (b, BLK_AV + h)),
            pl.BlockSpec(da_lambda.shape, lambda b, h, i: (0, 0)),
            pl.BlockSpec(gains.shape, lambda b, h, i: (0, 0)),
        ],
        out_specs=pl.BlockSpec((tq, LANES), lambda b, h, i: (b * nq + i, h)),
        out_shape=jax.ShapeDtypeStruct((T, DA_HEADS * LANES), BF16),
        compiler_params=_cparams(("parallel", "parallel", "arbitrary")),
        name="diff_attn",
    )(qa, ka, p_qkv, da_lambda, gains)


def _fox_kernel(q_ref, k_ref, v_ref, cum_ref, crow_ref, o_ref, *, tq, tk):
    h = pl.program_id(1)
    i = pl.program_id(2)
    lo = i * tq
    q = q_ref[...]
    lane = lax.broadcasted_iota(jnp.int32, (tq, LANES), 1)
    ccol = jnp.sum(jnp.where(lane == h, cum_ref[...], 0.0), axis=-1, keepdims=True)
    trow = lo + lax.broadcasted_iota(jnp.int32, (tq, tk), 0)
    kcol = lax.broadcasted_iota(jnp.int32, (tq, tk), 1)

    def body(kt, carry):
        off = pl.multiple_of(kt * tk, tk)
        k = k_ref[pl.ds(off, tk), :]
        v = v_ref[pl.ds(off, tk), :]
        s = _dot_t(q, k) + ccol - crow_ref[kt]
        return _online(s, trow >= kcol + off, *carry, v)

    nkt = (lo + tq + tk - 1) // tk
    m, l, acc = lax.fori_loop(0, nkt, body, _softmax_init(tq, LANES))
    o_ref[...] = (acc / l).astype(BF16)


def _fox_attention(qc, kf, p_qkv, cum, cum_row, *, batch, seq, tq=128, tk=128):
    T = qc.shape[0]
    nq = seq // tq
    return pl.pallas_call(
        functools.partial(_fox_kernel, tq=tq, tk=tk),
        grid=(batch, FOX_HEADS, nq),
        in_specs=[
            pl.BlockSpec((tq, LANES), lambda b, h, i: (b * nq + i, h)),
            pl.BlockSpec((seq, LANES), lambda b, h, i: (b, h)),
            pl.BlockSpec((seq, LANES), lambda b, h, i: (b, BLK_CV + h)),
            pl.BlockSpec((tq, LANES), lambda b, h, i: (b * nq + i, 0)),
            pl.BlockSpec((None, None, seq // tk, 1, tk), lambda b, h, i: (b, h, 0, 0, 0)),
        ],
        out_specs=pl.BlockSpec((tq, LANES), lambda b, h, i: (b * nq + i, h)),
        out_shape=jax.ShapeDtypeStruct((T, FOX_HEADS * LANES), BF16),
        compiler_params=_cparams(("parallel", "parallel", "arbitrary")),
        name="fox_attn",
    )(qc, kf, p_qkv, cum, cum_row)


def _nsa_kernel(q_ref, ks_ref, vs_ref, kw_ref, vw_ref, kc_ref, vc_ref, gt_ref, o_ref, *, tq, tk, n_sel):
    i = pl.program_id(2)
    lo = i * tq
    nb = kc_ref.shape[0]
    qs = [q_ref[:, h * LANES:(h + 1) * LANES] for h in range(NSA_HPG)]
    trow1 = lo + lax.broadcasted_iota(jnp.int32, (tq, 1), 0)

    pad = jnp.zeros((LANES - nb, LANES), BF16)
    kc = jnp.concatenate([kc_ref[...], pad], axis=0)
    vc = jnp.concatenate([vc_ref[...], pad], axis=0)
    nidx = lax.broadcasted_iota(jnp.int32, (tq, LANES), 1)
    cmask = nidx * NSA_BLOCK + (NSA_BLOCK - 1) <= trow1
    importance = jnp.zeros((tq, LANES), F32)
    o_cmp = []
    for h in range(NSA_HPG):
        s = jnp.where(cmask, _dot_t(qs[h], kc), NEG_INF)
        e = jnp.where(cmask, jnp.exp(s - jnp.max(s, axis=-1, keepdims=True)), 0.0)
        p = e / jnp.maximum(jnp.sum(e, axis=-1, keepdims=True), 1e-30)
        importance = importance + p
        o_cmp.append(_dot(p.astype(BF16), vc))

    cur = trow1 >> BLOCK_SHIFT
    forced = (nidx == 0) | (nidx == cur) | (nidx == cur - 1)
    score = jnp.where(forced, NSA_FORCED_SCORE, jnp.where(nidx <= cur, importance, -1.0))
    nidx_f = nidx.astype(F32)
    sel = jnp.zeros((tq, LANES), F32)
    for _ in range(n_sel):
        best = jnp.max(score, axis=-1, keepdims=True)
        first = jnp.min(jnp.where(score == best, nidx_f, float(LANES)), axis=-1, keepdims=True)
        hit = nidx_f == first
        sel = jnp.where(hit, 1.0, sel)
        score = jnp.where(hit, -3e38, score)
    sel_b = sel.astype(BF16)

    trow = lo + lax.broadcasted_iota(jnp.int32, (tq, tk), 0)
    kcol = lax.broadcasted_iota(jnp.int32, (tq, tk), 1)
    e_row = lax.broadcasted_iota(jnp.int32, (LANES, tk), 0)
    e_col = lax.broadcasted_iota(jnp.int32, (LANES, tk), 1) >> BLOCK_SHIFT
    init = tuple(_softmax_init(tq, LANES) for _ in range(NSA_HPG))

    def attend(k_ref, v_ref, mask_fn, kt0, kt1):
        def body(kt, carry):
            off = pl.multiple_of(kt * tk, tk)
            k = k_ref[pl.ds(off, tk), :]
            v = v_ref[pl.ds(off, tk), :]
            mask = mask_fn(kt, kcol + off)
            return tuple(_online(_dot_t(qs[h], k), mask, *carry[h], v) for h in range(NSA_HPG))

        out = lax.fori_loop(kt0, kt1, body, init)
        return [acc / l for (_, l, acc) in out]

    def slc_mask(kt, key):
        expand = jnp.where(e_row == e_col + kt * (tk // NSA_BLOCK), 1.0, 0.0).astype(BF16)
        return (_dot(sel_b, expand) > 0.5) & (key <= trow)

    def win_mask(kt, key):
        return (key <= trow) & (key > trow - NSA_WINDOW)

    kt_end = (lo + tq + tk - 1) // tk
    o_slc = attend(ks_ref, vs_ref, slc_mask, 0, kt_end)
    o_win = attend(kw_ref, vw_ref, win_mask, jnp.maximum(lo - NSA_WINDOW, 0) // tk, kt_end)

    gt = jax.nn.sigmoid(gt_ref[...])
    for h in range(NSA_HPG):
        o = (gt[:, 3 * h:3 * h + 1] * o_cmp[h] + gt[:, 3 * h + 1:3 * h + 2] * o_slc[h]
             + gt[:, 3 * h + 2:3 * h + 3] * o_win[h])
        o_ref[:, h * LANES:(h + 1) * LANES] = o.astype(BF16)


def _nsa_attention(qb, ks, kw, p_qkv, cmp_kv, small, *, batch, seq, tq=128, tk=128):
    T = qb.shape[0]
    nq = seq // tq
    nb = seq // NSA_BLOCK
    n_sel = min(NSA_TOP_N, nb)
    hw = NSA_HPG * LANES
    kv_blk = lambda c: pl.BlockSpec((seq, LANES), lambda b, g, i: (b, c + g))
    return pl.pallas_call(
        functools.partial(_nsa_kernel, tq=tq, tk=tk, n_sel=n_sel),
        grid=(batch, NSA_GROUPS, nq),
        in_specs=[
            pl.BlockSpec((tq, hw), lambda b, g, i: (b * nq + i, g)),
            pl.BlockSpec((seq, LANES), lambda b, g, i: (b, g)),
            kv_blk(BLK_BKV + 6),
            pl.BlockSpec((seq, LANES), lambda b, g, i: (b, g)),
            kv_blk(BLK_BKV + 10),
            pl.BlockSpec((None, None, nb, LANES), lambda b, g, i: (0, g, b, 0)),
            pl.BlockSpec((None, None, nb, LANES), lambda b, g, i: (1, g, b, 0)),
            pl.BlockSpec((tq, LANES), lambda b, g, i: (b * nq + i, 1 + g)),
        ],
        out_specs=pl.BlockSpec((tq, hw), lambda b, g, i: (b * nq + i, g)),
        out_shape=jax.ShapeDtypeStruct((T, NSA_HEADS * LANES), BF16),
        compiler_params=_cparams(("parallel", "parallel", "arbitrary")),
        name="nsa_attn",
    )(qb, ks, p_qkv, kw, p_qkv, cmp_kv, cmp_kv, small)


def _merge_kernel(x_ref, oa_ref, ob_ref, oc_ref, g0_ref, g1_ref, g2_ref,
                  wa_ref, wb_ref, wc_ref, wo_ref, o_ref):
    y = (g0_ref[...].astype(F32) * _dot(oa_ref[...], wa_ref[...])
         + g1_ref[...].astype(F32) * _dot(ob_ref[...], wb_ref[...])
         + g2_ref[...].astype(F32) * _dot(oc_ref[...], wc_ref[...]))
    o_ref[...] = x_ref[...] + _dot(y.astype(BF16), wo_ref[...])


def _merge(x2d, oa, ob, oc, p_gm, wa, wb, wc, wo, *, tm=256):
    T, D = x2d.shape

    def rows(w, c=0):
        return pl.BlockSpec((tm, w), lambda i: (i, c))

    def full(w):
        return pl.BlockSpec(w.shape, lambda i: (0, 0))

    return pl.pallas_call(
        _merge_kernel,
        grid=(T // tm,),
        in_specs=[rows(D), rows(oa.shape[1]), rows(ob.shape[1]), rows(oc.shape[1]),
                  rows(D, 0), rows(D, 1), rows(D, 2), full(wa), full(wb), full(wc), full(wo)],
        out_specs=rows(D),
        out_shape=jax.ShapeDtypeStruct((T, D), F32),
        compiler_params=_cparams(("parallel",)),
        name="merge",
    )(x2d, oa, ob, oc, p_gm, p_gm, p_gm, wa, wb, wc, wo)


def _rope_tables(positions, width, rot_dim):
    half = rot_dim // 2
    inv_freq = ROPE_THETA ** (-jnp.arange(0, rot_dim, 2, dtype=F32) / rot_dim)
    ang = positions.astype(F32).reshape(-1, 1) * inv_freq
    cos, sin = jnp.cos(ang), jnp.sin(ang)
    T = ang.shape[0]
    rest = width - rot_dim
    c = jnp.concatenate([cos, cos, jnp.ones((T, rest), F32)], axis=1)
    s1 = jnp.concatenate([-sin, jnp.zeros((T, half + rest), F32)], axis=1)
    s2 = jnp.concatenate([jnp.zeros((T, half), F32), sin, jnp.zeros((T, rest), F32)], axis=1)
    rep = LANES // width
    return [jnp.tile(t, (1, rep)) for t in (c, s1, s2)]


def _pad_lanes(v):
    return jnp.pad(v.astype(F32), (0, LANES - v.shape[0]))


def kernel(x, positions, ffn1_norm, ffn1_w_gate, ffn1_w_up, ffn1_w_down, mix_norm, w_in, da_q_norm, da_k_norm, da_lambda, da_out_norm, nsa_q_norm, nsa_k_norm, nsa_cmp_pos, nsa_cmp_w1, nsa_cmp_w2, fox_q_norm, fox_k_norm, fox_f_bias, w_branch_a, w_branch_b, w_branch_c, w_out, ffn2_norm, ffn2_w_gate, ffn2_w_up, ffn2_w_down):
    B, S, D = x.shape
    depth = w_in.shape[0]
    T = B * S
    G, L, d = NSA_GROUPS, NSA_BLOCK, HEAD_DIM
    nb = S // L
    tk = 128

    rope_tab = jnp.stack(_rope_tables(positions, DA_QK_DIM, DA_QK_DIM // ROPE_FRACTION)
                         + _rope_tables(positions, HEAD_DIM, HEAD_DIM // ROPE_FRACTION))

    widths = [512, 512, 512, 1024, 1536, 24, 512, 512, 512, 4, 3 * D]
    offs = [0]
    for w in widths:
        offs.append(offs[-1] + w)
    (o_aq, o_ak, o_av, o_bq, o_bkv, o_bg, o_cq, o_ck, o_cv, o_cf, o_gm, o_end) = offs

    xc = x.reshape(T, D)
    for l in range(depth):
        lam_init = 0.8 - 0.6 * math.exp(-0.3 * l)
        wl = w_in[l]
        w_qkv = jnp.concatenate([wl[:, o_aq:o_bg], wl[:, o_cq:o_cf]], axis=1).astype(BF16)
        w_gm = wl[:, o_gm:o_end].astype(BF16)
        zpad = lambda n: jnp.zeros((D, n), F32)
        w_small = jnp.concatenate([
            wl[:, o_cf:o_gm], zpad(LANES - FOX_HEADS),
            wl[:, o_bg:o_bg + 12], zpad(LANES - 12),
            wl[:, o_bg + 12:o_bg + 24], zpad(LANES - 12)], axis=1)
        w_small_hi = w_small.astype(BF16)
        w_small_lo = (w_small - w_small_hi.astype(F32)).astype(BF16)
        gains = jnp.stack([
            jnp.tile(da_q_norm[l], 2), jnp.tile(da_k_norm[l], 2), nsa_q_norm[l],
            nsa_k_norm[l, 0], nsa_k_norm[l, 1], nsa_k_norm[l, 2], fox_q_norm[l], fox_k_norm[l],
            _pad_lanes(fox_f_bias[l]), da_out_norm[l]] + [jnp.zeros((LANES,), F32)] * 6).astype(F32)

        xc = _ffn(xc, ffn1_norm[l].reshape(1, D), ffn1_w_gate[l].astype(BF16),
                  ffn1_w_up[l].astype(BF16), ffn1_w_down[l].astype(BF16))

        g_mix = mix_norm[l].reshape(1, D)
        p_qkv = _proj(xc, g_mix, w_qkv, act=False, name="proj_qkv")
        p_gm = _proj(xc, g_mix, w_gm, act=True, name="proj_gate")
        small = _proj_small(xc, g_mix, w_small_hi, w_small_lo)

        qa, ka, qb, ks, kw, qc, kf, kcmp, vcmp, cum = _prep(p_qkv, small, rope_tab, gains, seq=S)

        xkv = jnp.stack([kcmp, vcmp]).reshape(2, G, B * nb, L * d)
        cmp_kv = _compress(xkv, nsa_cmp_pos[l].reshape(2, 1, L * d),
                           nsa_cmp_w1[l].astype(BF16), nsa_cmp_w2[l].astype(BF16),
                           nsa_k_norm[l, 0].reshape(1, d))

        cum_row = cum[:, :FOX_HEADS].reshape(B, S // tk, tk, FOX_HEADS).transpose(0, 3, 1, 2)
        cum_row = cum_row.reshape(B, FOX_HEADS, S // tk, 1, tk)

        o_a = _diff_attention(qa, ka, p_qkv, da_lambda[l], gains, batch=B, seq=S, lam_init=lam_init)
        o_b = _nsa_attention(qb, ks, kw, p_qkv, cmp_kv, small, batch=B, seq=S)
        o_c = _fox_attention(qc, kf, p_qkv, cum, cum_row, batch=B, seq=S)

        xc = _merge(xc, o_a, o_b, o_c, p_gm, w_branch_a[l].astype(BF16), w_branch_b[l].astype(BF16),
                    w_branch_c[l].astype(BF16), w_out[l].astype(BF16))

        xc = _ffn(xc, ffn2_norm[l].reshape(1, D), ffn2_w_gate[l].astype(BF16),
                  ffn2_w_up[l].astype(BF16), ffn2_w_down[l].astype(BF16))
    return xc.reshape(B, S, D)
```

```python
import functools
import math

import jax
import jax.numpy as jnp
from jax import lax
from jax.experimental import pallas as pl
from jax.experimental.pallas import tpu as pltpu

F32 = jnp.float32
BF16 = jnp.bfloat16

HEAD_DIM = 128
LANES = 128
ROPE_THETA = 500000.0
ROPE_FRACTION = 4
EPS = 1e-6
NEG_INF = -1e30

DA_HEADS = 4
DA_QK_DIM = 64
NSA_HEADS = 8
NSA_GROUPS = 2
NSA_HPG = NSA_HEADS // NSA_GROUPS
NSA_BLOCK = 64
BLOCK_SHIFT = 6
NSA_TOP_N = 8
NSA_WINDOW = 512
NSA_FORCED_SCORE = 1e4
FOX_HEADS = 4

ATT_TILE = 256
FOX_AUG = 2 * LANES

BLK_AQ, BLK_AK, BLK_AV, BLK_BQ, BLK_BKV, BLK_CQ, BLK_CK, BLK_CV = 0, 4, 8, 12, 20, 32, 36, 40
QKV_COLS = 44 * LANES
SMALL_COLS = 3 * LANES

G_DAQ, G_DAK, G_NQ, G_NK0, G_NK1, G_NK2, G_FQ, G_FK, G_FBIAS, G_DAOUT = range(10)

VMEM_LIMIT = 56 * 1024 * 1024


def _cparams(sem):
    return pltpu.CompilerParams(dimension_semantics=sem, vmem_limit_bytes=VMEM_LIMIT)


def _dot(a, b):
    return jnp.dot(a, b, preferred_element_type=F32)


def _rms_rows(xf, gain):
    return xf * lax.rsqrt(jnp.mean(xf * xf, axis=-1, keepdims=True) + EPS) * gain


def _ffn_kernel(x_ref, g_ref, wg_ref, wu_ref, wd_ref, o_ref, h_scr, acc_scr):
    j = pl.program_id(1)

    @pl.when(j == 0)
    def _():
        h_scr[...] = _rms_rows(x_ref[...], g_ref[...]).astype(BF16)
        acc_scr[...] = jnp.zeros_like(acc_scr)

    h = h_scr[...]
    g = _dot(h, wg_ref[...])
    u = _dot(h, wu_ref[...])
    a = (g * jax.nn.sigmoid(g)) * u
    acc_scr[...] += _dot(a.astype(BF16), wd_ref[...])

    @pl.when(j == pl.num_programs(1) - 1)
    def _():
        o_ref[...] = x_ref[...] + 0.5 * acc_scr[...]


def _ffn(x2d, gain, wg, wu, wd, *, tm=512, tf=512):
    T, D = x2d.shape
    F = wg.shape[1]
    return pl.pallas_call(
        _ffn_kernel,
        grid=(T // tm, F // tf),
        in_specs=[
            pl.BlockSpec((tm, D), lambda i, j: (i, 0)),
            pl.BlockSpec((1, D), lambda i, j: (0, 0)),
            pl.BlockSpec((D, tf), lambda i, j: (0, j)),
            pl.BlockSpec((D, tf), lambda i, j: (0, j)),
            pl.BlockSpec((tf, D), lambda i, j: (j, 0)),
        ],
        out_specs=pl.BlockSpec((tm, D), lambda i, j: (i, 0)),
        out_shape=jax.ShapeDtypeStruct((T, D), F32),
        scratch_shapes=[pltpu.VMEM((tm, D), BF16), pltpu.VMEM((tm, D), F32)],
        compiler_params=_cparams(("parallel", "arbitrary")),
        name="ffn",
    )(x2d, gain, wg, wu, wd)


def _proj_kernel(x_ref, g_ref, w_ref, o_ref, h_scr, *, act):
    @pl.when(pl.program_id(1) == 0)
    def _():
        h_scr[...] = _rms_rows(x_ref[...], g_ref[...]).astype(BF16)

    r = _dot(h_scr[...], w_ref[...])
    if act:
        r = jax.nn.sigmoid(r)
    o_ref[...] = r.astype(o_ref.dtype)


def _proj(x2d, gain, w, *, act, name, tm=512, tn=512):
    T, D = x2d.shape
    N = w.shape[1]
    return pl.pallas_call(
        functools.partial(_proj_kernel, act=act),
        grid=(T // tm, N // tn),
        in_specs=[
            pl.BlockSpec((tm, D), lambda i, j: (i, 0)),
            pl.BlockSpec((1, D), lambda i, j: (0, 0)),
            pl.BlockSpec((D, tn), lambda i, j: (0, j)),
        ],
        out_specs=pl.BlockSpec((tm, tn), lambda i, j: (i, j)),
        out_shape=jax.ShapeDtypeStruct((T, N), BF16),
        scratch_shapes=[pltpu.VMEM((tm, D), BF16)],
        compiler_params=_cparams(("parallel", "arbitrary")),
        name=name,
    )(x2d, gain, w)


def _split3(v):
    hi = v.astype(BF16)
    r = v - hi.astype(F32)
    mid = r.astype(BF16)
    lo = (r - mid.astype(F32)).astype(BF16)
    return hi, mid, lo


def _proj_small_kernel(x_ref, g_ref, whi_ref, wlo_ref, o_ref, ot_ref):
    h = _rms_rows(x_ref[...], g_ref[...])
    hi, mid, _ = _split3(h)
    whi = whi_ref[...]
    r = _dot(hi, whi) + (_dot(mid, whi) + _dot(hi, wlo_ref[...]))
    o_ref[...] = r
    for c in range(r.shape[1] // LANES):
        ot_ref[c * LANES:(c + 1) * LANES, :] = r[:, c * LANES:(c + 1) * LANES].T


def _proj_small(x2d, gain, whi, wlo, *, tm=512):
    T, D = x2d.shape
    N = whi.shape[1]
    return pl.pallas_call(
        _proj_small_kernel,
        grid=(T // tm,),
        in_specs=[
            pl.BlockSpec((tm, D), lambda i: (i, 0)),
            pl.BlockSpec((1, D), lambda i: (0, 0)),
            pl.BlockSpec((D, N), lambda i: (0, 0)),
            pl.BlockSpec((D, N), lambda i: (0, 0)),
        ],
        out_specs=(pl.BlockSpec((tm, N), lambda i: (i, 0)), pl.BlockSpec((N, tm), lambda i: (0, i))),
        out_shape=(jax.ShapeDtypeStruct((T, N), F32), jax.ShapeDtypeStruct((N, T), F32)),
        compiler_params=_cparams(("parallel",)),
        name="proj_small",
    )(x2d, gain, whi, wlo)


def _prep_kernel(p_ref, sm_ref, rope_ref, gains_ref,
                 qat_ref, ka_ref, vat_ref, qbt_ref, ks_ref, kw_ref, vst_ref, vwt_ref,
                 qct_ref, kf_ref, vct_ref, kcmp_ref, vcmp_ref,
                 carry_scr, *, tiles_per_seq):
    i = pl.program_id(0)
    tm = p_ref.shape[0]
    lane = lax.broadcasted_iota(jnp.int32, (tm, LANES), 1)
    low_half = lane < DA_QK_DIM

    def raw(c):
        return p_ref[:, c * LANES:(c + 1) * LANES]

    def chunk(c):
        return raw(c).astype(F32)

    def gain(r):
        return gains_ref[r:r + 1, :]

    def norm128(xc, r):
        return _rms_rows(xc, gain(r))

    def norm64(xc, r):
        sq = xc * xc
        tot = jnp.sum(sq, axis=-1, keepdims=True)
        lo = jnp.sum(jnp.where(low_half, sq, 0.0), axis=-1, keepdims=True)
        ms = jnp.where(low_half, lo, tot - lo) * (1.0 / DA_QK_DIM)
        return xc * lax.rsqrt(ms + EPS) * gain(r)

    def rope(y, t, half):
        c, s1, s2 = rope_ref[t], rope_ref[t + 1], rope_ref[t + 2]
        return y * c + pltpu.roll(y, LANES - half, 1) * s1 + pltpu.roll(y, half, 1) * s2

    def tr(y):
        return y.T.astype(BF16)

    half_a = DA_QK_DIM // ROPE_FRACTION // 2
    half_b = HEAD_DIM // ROPE_FRACTION // 2
    scale_a = DA_QK_DIM ** -0.5
    scale_b = HEAD_DIM ** -0.5

    for h in range(DA_HEADS):
        sl = slice(h * LANES, (h + 1) * LANES)
        qat_ref[h, 0] = tr(rope(norm64(chunk(BLK_AQ + h), G_DAQ), 0, half_a) * scale_a)
        ka_ref[:, sl] = rope(norm64(chunk(BLK_AK + h), G_DAK), 0, half_a).astype(BF16)
        vat_ref[h, 0] = tr(chunk(BLK_AV + h))
    for h in range(NSA_HEADS):
        qbt_ref[h, 0] = tr(rope(norm128(chunk(BLK_BQ + h), G_NQ), 3, half_b) * scale_b)
    for g in range(NSA_GROUPS):
        sl = slice(g * LANES, (g + 1) * LANES)
        kcmp_ref[g] = rope(chunk(BLK_BKV + 0 + g), 3, half_b).astype(BF16)
        vcmp_ref[g] = raw(BLK_BKV + 2 + g)
        ks_ref[:, sl] = rope(norm128(chunk(BLK_BKV + 4 + g), G_NK1), 3, half_b).astype(BF16)
        vst_ref[g, 0] = tr(chunk(BLK_BKV + 6 + g))
        kw_ref[:, sl] = rope(norm128(chunk(BLK_BKV + 8 + g), G_NK2), 3, half_b).astype(BF16)
        vwt_ref[g, 0] = tr(chunk(BLK_BKV + 10 + g))

    @pl.when(i % tiles_per_seq == 0)
    def _():
        carry_scr[...] = jnp.zeros_like(carry_scr)

    z = sm_ref[...] + gain(G_FBIAS)
    logf = jnp.minimum(z, 0.0) - jnp.log(1.0 + jnp.exp(-jnp.abs(z)))
    r_id = lax.broadcasted_iota(jnp.int32, (tm, tm), 0)
    c_id = lax.broadcasted_iota(jnp.int32, (tm, tm), 1)
    tri = jnp.where(r_id >= c_id, 1.0, 0.0).astype(BF16)
    hi, mid, lo = _split3(logf)
    cum = _dot(tri, hi) + (_dot(tri, mid) + _dot(tri, lo)) + carry_scr[0:1, :]
    carry_scr[0:1, :] = cum[tm - 1:tm, :]

    for h in range(FOX_HEADS):
        c = jnp.broadcast_to(cum[:, h:h + 1], (tm, LANES))
        c_hi, c_mid, c_lo = (v.astype(F32) for v in _split3(c))
        ones = jnp.where(lane < 6, 1.0, 0.0)
        q_aug = jnp.where(lane == 0, c_hi, jnp.where(lane == 1, c_mid, jnp.where(lane == 2, c_lo, ones)))
        k_aug = jnp.where(lane == 3, -c_hi, jnp.where(lane == 4, -c_mid, jnp.where(lane == 5, -c_lo, ones)))
        qct_ref[h, 0, 0:LANES, :] = tr(norm128(chunk(BLK_CQ + h), G_FQ) * scale_b)
        qct_ref[h, 0, LANES:FOX_AUG, :] = tr(q_aug)
        kf_ref[:, h * FOX_AUG:h * FOX_AUG + LANES] = norm128(chunk(BLK_CK + h), G_FK).astype(BF16)
        kf_ref[:, h * FOX_AUG + LANES:(h + 1) * FOX_AUG] = k_aug.astype(BF16)
        vct_ref[h, 0] = tr(chunk(BLK_CV + h))


def _prep(p_qkv, small, rope_tab, gains, *, seq):
    tm = ATT_TILE
    T = p_qkv.shape[0]
    nt = T // tm
    G = NSA_GROUPS

    def rows(w):
        return pl.BlockSpec((tm, w), lambda i: (i, 0))

    def tiles_t(n, d=LANES):
        return (jax.ShapeDtypeStruct((n, nt, d, tm), BF16), pl.BlockSpec((n, 1, d, tm), lambda i: (0, i, 0, 0)))

    def nat(w):
        return (jax.ShapeDtypeStruct((T, w), BF16), rows(w))

    def grp():
        return (jax.ShapeDtypeStruct((G, T, LANES), BF16), pl.BlockSpec((G, tm, LANES), lambda i: (0, i, 0)))

    outs = [
        tiles_t(DA_HEADS), nat(DA_HEADS * LANES), tiles_t(DA_HEADS),
        tiles_t(NSA_HEADS), nat(G * LANES), nat(G * LANES), tiles_t(G), tiles_t(G),
        tiles_t(FOX_HEADS, FOX_AUG), nat(FOX_HEADS * FOX_AUG), tiles_t(FOX_HEADS),
        grp(), grp(),
    ]
    return pl.pallas_call(
        functools.partial(_prep_kernel, tiles_per_seq=seq // tm),
        grid=(nt,),
        in_specs=[
            rows(QKV_COLS),
            rows(LANES),
            pl.BlockSpec((6, tm, LANES), lambda i: (0, i, 0)),
            pl.BlockSpec(gains.shape, lambda i: (0, 0)),
        ],
        out_specs=tuple(o[1] for o in outs),
        out_shape=tuple(o[0] for o in outs),
        scratch_shapes=[pltpu.VMEM((8, LANES), F32)],
        compiler_params=_cparams(("arbitrary",)),
        name="prep",
    )(p_qkv, small, rope_tab, gains)


def _compress_kernel(x_ref, pos_ref, w1_ref, w2_ref, gain_ref, kc_ref, vct_ref, *, batch):
    def mlp(kv):
        blk = (x_ref[kv].astype(F32) + pos_ref[kv]).astype(BF16)
        hid = jax.nn.gelu(_dot(blk, w1_ref[kv]))
        return _dot(hid.astype(BF16), w2_ref[kv])

    kc = _rms_rows(mlp(0), gain_ref[...])
    vc = mlp(1)
    nb = kc.shape[0] // batch
    pad = jnp.zeros((LANES - nb, LANES), F32)
    for b in range(batch):
        kc_ref[b] = jnp.concatenate([kc[b * nb:(b + 1) * nb], pad], axis=0).astype(BF16)
        vct_ref[b] = jnp.concatenate([vc[b * nb:(b + 1) * nb], pad], axis=0).T.astype(BF16)


def _compress(xkv, pos, w1, w2, gain, *, batch):
    _, G, R, K = xkv.shape
    d = w1.shape[-1]
    out = jax.ShapeDtypeStruct((G, batch, LANES, LANES), BF16)
    ospec = pl.BlockSpec((None, batch, LANES, LANES), lambda g: (g, 0, 0, 0))
    return pl.pallas_call(
        functools.partial(_compress_kernel, batch=batch),
        grid=(G,),
        in_specs=[
            pl.BlockSpec((2, None, R, K), lambda g: (0, g, 0, 0)),
            pl.BlockSpec((2, 1, K), lambda g: (0, 0, 0)),
            pl.BlockSpec((2, K, d), lambda g: (0, 0, 0)),
            pl.BlockSpec((2, d, d), lambda g: (0, 0, 0)),
            pl.BlockSpec((1, d), lambda g: (0, 0)),
        ],
        out_specs=(ospec, ospec),
        out_shape=(out, out),
        compiler_params=_cparams(("arbitrary",)),
        name="compress",
    )(xkv, pos, w1, w2, gain)


def _online_t(s_ts, mask, carry, v_ts):
    stats = []
    for s_t, (m, l, _) in zip(s_ts, carry):
        if mask is not None:
            s_t = jnp.where(mask, s_t, NEG_INF)
        m_new = jnp.maximum(m, jnp.max(s_t, axis=0, keepdims=True))
        alpha = jnp.exp(m - m_new)
        p = jnp.exp(s_t - m_new)
        l_new = alpha * l + jnp.sum(p, axis=0, keepdims=True)
        stats.append((m_new, l_new, alpha, p.astype(BF16)))
    return tuple((m_new, l_new, alpha * acc_t + _dot(v_t, p))
                 for (m_new, l_new, alpha, p), (_, _, acc_t), v_t in zip(stats, carry, v_ts))


def _softmax_init(tq, d):
    return (jnp.full((1, tq), NEG_INF, F32), jnp.zeros((1, tq), F32), jnp.zeros((d, tq), F32))


def _causal_mask_t(t):
    return lax.broadcasted_iota(jnp.int32, (t, t), 0) <= lax.broadcasted_iota(jnp.int32, (t, t), 1)


def _diff_kernel(qt_ref, k_ref, vt_ref, lam_ref, gains_ref, o_ref, *, lam_init):
    i = pl.program_id(2)
    t = ATT_TILE
    nh = qt_ref.shape[0]
    drow = lax.broadcasted_iota(jnp.int32, (LANES, t), 0)
    qts = []
    for h in range(nh):
        q = qt_ref[h, 0]
        qts.append(jnp.where(drow < DA_QK_DIM, q, jnp.zeros_like(q)))
        qts.append(jnp.where(drow < DA_QK_DIM, jnp.zeros_like(q), q))

    def tile(kt, carry, mask):
        off = pl.multiple_of(kt * t, t)
        ks = [k_ref[pl.ds(off, t), h * LANES:(h + 1) * LANES] for h in range(nh)]
        s_ts = [_dot(ks[c // 2], qts[c]) for c in range(2 * nh)]
        return _online_t(s_ts, mask, carry, [vt_ref[c // 2, kt] for c in range(2 * nh)])

    init = tuple(_softmax_init(t, LANES) for _ in range(2 * nh))
    carry = lax.fori_loop(0, i, lambda kt, c: tile(kt, c, None), init)
    carry = tile(i, carry, _causal_mask_t(t))

    lp = lam_ref[...]
    lam = (jnp.exp(jnp.sum(lp[0:1] * lp[1:2], axis=-1, keepdims=True))
           - jnp.exp(jnp.sum(lp[2:3] * lp[3:4], axis=-1, keepdims=True)) + lam_init)
    for h in range(nh):
        (_, l0, a0), (_, l1, a1) = carry[2 * h], carry[2 * h + 1]
        o = (a0 / l0 - lam * (a1 / l1)).T
        o = _rms_rows(o, gains_ref[G_DAOUT:G_DAOUT + 1, :]) * (1.0 - lam_init)
        o_ref[:, h * LANES:(h + 1) * LANES] = o.astype(BF16)


def _diff_attention(qat, ka, vat, da_lambda, gains, *, batch, seq, lam_init, heads_per_step=2):
    t = ATT_TILE
    nq = seq // t
    T = batch * seq
    hp = heads_per_step
    return pl.pallas_call(
        functools.partial(_diff_kernel, lam_init=lam_init),
        grid=(batch, DA_HEADS // hp, nq),
        in_specs=[
            pl.BlockSpec((hp, 1, LANES, t), lambda b, p, i: (p, b * nq + i, 0, 0)),
            pl.BlockSpec((seq, hp * LANES), lambda b, p, i: (b, p)),
            pl.BlockSpec((hp, nq, LANES, t), lambda b, p, i: (p, b, 0, 0)),
            pl.BlockSpec(da_lambda.shape, lambda b, p, i: (0, 0)),
            pl.BlockSpec(gains.shape, lambda b, p, i: (0, 0)),
        ],
        out_specs=pl.BlockSpec((t, hp * LANES), lambda b, p, i: (b * nq + i, p)),
        out_shape=jax.ShapeDtypeStruct((T, DA_HEADS * LANES), BF16),
        compiler_params=_cparams(("parallel", "parallel", "arbitrary")),
        name="diff_attn",
    )(qat, ka, vat, da_lambda, gains)


def _fox_kernel(qt_ref, k_ref, vt_ref, o_ref):
    i = pl.program_id(1)
    t = ATT_TILE
    nh = qt_ref.shape[0]
    qts = [qt_ref[h, 0] for h in range(nh)]

    def tile(kt, carry, mask):
        off = pl.multiple_of(kt * t, t)
        s_ts = [_dot(k_ref[pl.ds(off, t), h * FOX_AUG:(h + 1) * FOX_AUG], qts[h]) for h in range(nh)]
        return _online_t(s_ts, mask, carry, [vt_ref[h, kt] for h in range(nh)])

    init = tuple(_softmax_init(t, LANES) for _ in range(nh))
    carry = lax.fori_loop(0, i, lambda kt, c: tile(kt, c, None), init)
    carry = tile(i, carry, _causal_mask_t(t))
    for h in range(nh):
        _, l, acc = carry[h]
        o_ref[:, h * LANES:(h + 1) * LANES] = (acc / l).T.astype(BF16)


def _fox_attention(qct, kf, vct, *, batch, seq):
    t = ATT_TILE
    nq = seq // t
    T = batch * seq
    nh = FOX_HEADS
    return pl.pallas_call(
        _fox_kernel,
        grid=(batch, nq),
        in_specs=[
            pl.BlockSpec((nh, 1, FOX_AUG, t), lambda b, i: (0, b * nq + i, 0, 0)),
            pl.BlockSpec((seq, nh * FOX_AUG), lambda b, i: (b, 0)),
            pl.BlockSpec((nh, nq, LANES, t), lambda b, i: (0, b, 0, 0)),
        ],
        out_specs=pl.BlockSpec((t, nh * LANES), lambda b, i: (b * nq + i, 0)),
        out_shape=jax.ShapeDtypeStruct((T, nh * LANES), BF16),
        compiler_params=_cparams(("parallel", "arbitrary")),
        name="fox_attn",
    )(qct, kf, vct)


def _nsa_kernel(qt_ref, ks_ref, vst_ref, kw_ref, vwt_ref, kc_ref, vct_ref, gt_ref, o_ref, *, n_sel):
    i = pl.program_id(2)
    t = ATT_TILE
    lo = i * t
    nh = qt_ref.shape[0]
    qts = [qt_ref[h, 0] for h in range(nh)]
    tq_row = lo + lax.broadcasted_iota(jnp.int32, (1, t), 1)

    kc = kc_ref[...]
    vct = vct_ref[...]
    nidx = lax.broadcasted_iota(jnp.int32, (LANES, t), 0)
    cmask = nidx * NSA_BLOCK + (NSA_BLOCK - 1) <= tq_row
    importance = jnp.zeros((LANES, t), F32)
    o_cmp = []
    for h in range(nh):
        s = jnp.where(cmask, _dot(kc, qts[h]), NEG_INF)
        e = jnp.where(cmask, jnp.exp(s - jnp.max(s, axis=0, keepdims=True)), 0.0)
        p = e / jnp.maximum(jnp.sum(e, axis=0, keepdims=True), 1e-30)
        importance = importance + p
        o_cmp.append(_dot(vct, p.astype(BF16)))

    cur = tq_row >> BLOCK_SHIFT
    forced = (nidx == 0) | (nidx == cur) | (nidx == cur - 1)
    score = jnp.where(forced, NSA_FORCED_SCORE, jnp.where(nidx <= cur, importance, -1.0))
    nidx_f = nidx.astype(F32)
    sel = jnp.zeros((LANES, t), F32)
    for _ in range(n_sel):
        best = jnp.max(score, axis=0, keepdims=True)
        first = jnp.min(jnp.where(score == best, nidx_f, float(LANES)), axis=0, keepdims=True)
        hit = nidx_f == first
        sel = jnp.where(hit, 1.0, sel)
        score = jnp.where(hit, -3e38, score)
    sel_b = sel.astype(BF16)

    e_key = lax.broadcasted_iota(jnp.int32, (t, LANES), 0) >> BLOCK_SHIFT
    e_blk = lax.broadcasted_iota(jnp.int32, (t, LANES), 1)
    krow = lax.broadcasted_iota(jnp.int32, (t, t), 0)
    qcol = lax.broadcasted_iota(jnp.int32, (t, t), 1)
    causal = krow <= qcol
    init = tuple(_softmax_init(t, LANES) for _ in range(nh))

    def tile(k_ref, vt_ref, kt, carry, mask):
        off = pl.multiple_of(kt * t, t)
        k = k_ref[pl.ds(off, t), :]
        return _online_t([_dot(k, qts[h]) for h in range(nh)], mask, carry, [vt_ref[kt]] * nh)

    def selected(kt):
        expand = jnp.where(e_blk == e_key + kt * (t // NSA_BLOCK), 1.0, 0.0).astype(BF16)
        return _dot(expand, sel_b) > 0.5

    carry = lax.fori_loop(0, i, lambda kt, c: tile(ks_ref, vst_ref, kt, c, selected(kt)), init)
    carry = tile(ks_ref, vst_ref, i, carry, selected(i) & causal)
    o_slc = [acc / l for (_, l, acc) in carry]

    def in_window(kt):
        return krow + kt * t > (qcol + lo) - NSA_WINDOW

    kt0 = jnp.maximum(i - NSA_WINDOW // t, 0)
    carry = lax.fori_loop(kt0, i, lambda kt, c: tile(kw_ref, vwt_ref, kt, c, in_window(kt)), init)
    carry = tile(kw_ref, vwt_ref, i, carry, causal)
    o_win = [acc / l for (_, l, acc) in carry]

    gt = jax.nn.sigmoid(gt_ref[...])
    for h in range(nh):
        o = (gt[3 * h:3 * h + 1, :] * o_cmp[h] + gt[3 * h + 1:3 * h + 2, :] * o_slc[h]
             + gt[3 * h + 2:3 * h + 3, :] * o_win[h])
        o_ref[:, h * LANES:(h + 1) * LANES] = o.T.astype(BF16)


def _nsa_attention(qbt, ks, kw, vst, vwt, kc, vct, small_t, *, batch, seq):
    t = ATT_TILE
    nq = seq // t
    T = batch * seq
    n_sel = min(NSA_TOP_N, seq // NSA_BLOCK)
    hw = NSA_HPG * LANES
    k_spec = pl.BlockSpec((seq, LANES), lambda b, g, i: (b, g))
    vt_spec = pl.BlockSpec((None, nq, LANES, t), lambda b, g, i: (g, b, 0, 0))
    c_spec = pl.BlockSpec((None, None, LANES, LANES), lambda b, g, i: (g, b, 0, 0))
    return pl.pallas_call(
        functools.partial(_nsa_kernel, n_sel=n_sel),
        grid=(batch, NSA_GROUPS, nq),
        in_specs=[
            pl.BlockSpec((NSA_HPG, 1, LANES, t), lambda b, g, i: (g, b * nq + i, 0, 0)),
            k_spec, vt_spec, k_spec, vt_spec, c_spec, c_spec,
            pl.BlockSpec((LANES, t), lambda b, g, i: (1 + g, b * nq + i)),
        ],
        out_specs=pl.BlockSpec((t, hw), lambda b, g, i: (b * nq + i, g)),
        out_shape=jax.ShapeDtypeStruct((T, NSA_HEADS * LANES), BF16),
        compiler_params=_cparams(("parallel", "parallel", "arbitrary")),
        name="nsa_attn",
    )(qbt, ks, vst, kw, vwt, kc, vct, small_t)


def _merge_kernel(x_ref, oa_ref, ob_ref, oc_ref, g0_ref, g1_ref, g2_ref,
                  wa_ref, wb_ref, wc_ref, wo_ref, o_ref):
    y = (g0_ref[...].astype(F32) * _dot(oa_ref[...], wa_ref[...])
         + g1_ref[...].astype(F32) * _dot(ob_ref[...], wb_ref[...])
         + g2_ref[...].astype(F32) * _dot(oc_ref[...], wc_ref[...]))
    o_ref[...] = x_ref[...] + _dot(y.astype(BF16), wo_ref[...])


def _merge(x2d, oa, ob, oc, p_gm, wa, wb, wc, wo, *, tm=256):
    T, D = x2d.shape

    def rows(w, c=0):
        return pl.BlockSpec((tm, w), lambda i: (i, c))

    def full(w):
        return pl.BlockSpec(w.shape, lambda i: (0, 0))

    return pl.pallas_call(
        _merge_kernel,
        grid=(T // tm,),
        in_specs=[rows(D), rows(oa.shape[1]), rows(ob.shape[1]), rows(oc.shape[1]),
                  rows(D, 0), rows(D, 1), rows(D, 2), full(wa), full(wb), full(wc), full(wo)],
        out_specs=rows(D),
        out_shape=jax.ShapeDtypeStruct((T, D), F32),
        compiler_params=_cparams(("parallel",)),
        name="merge",
    )(x2d, oa, ob, oc, p_gm, p_gm, p_gm, wa, wb, wc, wo)


def _rope_tables(positions, width, rot_dim):
    half = rot_dim // 2
    inv_freq = ROPE_THETA ** (-jnp.arange(0, rot_dim, 2, dtype=F32) / rot_dim)
    ang = positions.astype(F32).reshape(-1, 1) * inv_freq
    cos, sin = jnp.cos(ang), jnp.sin(ang)
    T = ang.shape[0]
    rest = width - rot_dim
    c = jnp.concatenate([cos, cos, jnp.ones((T, rest), F32)], axis=1)
    s1 = jnp.concatenate([-sin, jnp.zeros((T, half + rest), F32)], axis=1)
    s2 = jnp.concatenate([jnp.zeros((T, half), F32), sin, jnp.zeros((T, rest), F32)], axis=1)
    rep = LANES // width
    return [jnp.tile(t, (1, rep)) for t in (c, s1, s2)]


def _pad_lanes(v):
    return jnp.pad(v.astype(F32), (0, LANES - v.shape[0]))


def kernel(x, positions, ffn1_norm, ffn1_w_gate, ffn1_w_up, ffn1_w_down, mix_norm, w_in, da_q_norm, da_k_norm, da_lambda, da_out_norm, nsa_q_norm, nsa_k_norm, nsa_cmp_pos, nsa_cmp_w1, nsa_cmp_w2, fox_q_norm, fox_k_norm, fox_f_bias, w_branch_a, w_branch_b, w_branch_c, w_out, ffn2_norm, ffn2_w_gate, ffn2_w_up, ffn2_w_down):
    B, S, D = x.shape
    depth = w_in.shape[0]
    T = B * S
    G, L, d = NSA_GROUPS, NSA_BLOCK, HEAD_DIM
    nb = S // L
    assert S % ATT_TILE == 0 and nb <= LANES

    rope_tab = jnp.stack(_rope_tables(positions, DA_QK_DIM, DA_QK_DIM // ROPE_FRACTION)
                         + _rope_tables(positions, HEAD_DIM, HEAD_DIM // ROPE_FRACTION))

    widths = [512, 512, 512, 1024, 1536, 24, 512, 512, 512, 4, 3 * D]
    offs = [0]
    for w in widths:
        offs.append(offs[-1] + w)
    (o_aq, o_ak, o_av, o_bq, o_bkv, o_bg, o_cq, o_ck, o_cv, o_cf, o_gm, o_end) = offs

    xc = x.reshape(T, D)
    for l in range(depth):
        lam_init = 0.8 - 0.6 * math.exp(-0.3 * l)
        wl = w_in[l]
        w_qkv = jnp.concatenate([wl[:, o_aq:o_bg], wl[:, o_cq:o_cf]], axis=1).astype(BF16)
        w_gm = wl[:, o_gm:o_end].astype(BF16)
        zpad = lambda n: jnp.zeros((D, n), F32)
        w_small = jnp.concatenate([
            wl[:, o_cf:o_gm], zpad(LANES - FOX_HEADS),
            wl[:, o_bg:o_bg + 12], zpad(LANES - 12),
            wl[:, o_bg + 12:o_bg + 24], zpad(LANES - 12)], axis=1)
        w_small_hi = w_small.astype(BF16)
        w_small_lo = (w_small - w_small_hi.astype(F32)).astype(BF16)
        gains = jnp.stack([
            jnp.tile(da_q_norm[l], 2), jnp.tile(da_k_norm[l], 2), nsa_q_norm[l],
            nsa_k_norm[l, 0], nsa_k_norm[l, 1], nsa_k_norm[l, 2], fox_q_norm[l], fox_k_norm[l],
            _pad_lanes(fox_f_bias[l]), da_out_norm[l]] + [jnp.zeros((LANES,), F32)] * 6).astype(F32)

        xc = _ffn(xc, ffn1_norm[l].reshape(1, D), ffn1_w_gate[l].astype(BF16),
                  ffn1_w_up[l].astype(BF16), ffn1_w_down[l].astype(BF16))

        g_mix = mix_norm[l].reshape(1, D)
        p_qkv = _proj(xc, g_mix, w_qkv, act=False, name="proj_qkv")
        p_gm = _proj(xc, g_mix, w_gm, act=True, name="proj_gate")
        small, small_t = _proj_small(xc, g_mix, w_small_hi, w_small_lo)

        (qat, ka, vat, qbt, ks, kw, vst, vwt, qct, kf, vct, kcmp, vcmp) = _prep(
            p_qkv, small, rope_tab, gains, seq=S)

        xkv = jnp.stack([kcmp, vcmp]).reshape(2, G, B * nb, L * d)
        kc, vcmp_t = _compress(xkv, nsa_cmp_pos[l].reshape(2, 1, L * d),
                               nsa_cmp_w1[l].astype(BF16), nsa_cmp_w2[l].astype(BF16),
                               nsa_k_norm[l, 0].reshape(1, d), batch=B)

        o_a = _diff_attention(qat, ka, vat, da_lambda[l], gains, batch=B, seq=S, lam_init=lam_init)
        o_b = _nsa_attention(qbt, ks, kw, vst, vwt, kc, vcmp_t, small_t, batch=B, seq=S)
        o_c = _fox_attention(qct, kf, vct, batch=B, seq=S)

        xc = _merge(xc, o_a, o_b, o_c, p_gm, w_branch_a[l].astype(BF16), w_branch_b[l].astype(BF16),
                    w_branch_c[l].astype(BF16), w_out[l].astype(BF16))

        xc = _ffn(xc, ffn2_norm[l].reshape(1, D), ffn2_w_gate[l].astype(BF16),
                  ffn2_w_up[l].astype(BF16), ffn2_w_down[l].astype(BF16))
    return xc.reshape(B, S, D)
```

```python
import functools
import math

import jax
import jax.numpy as jnp
from jax import lax
from jax.experimental import pallas as pl
from jax.experimental.pallas import tpu as pltpu

F32 = jnp.float32
BF16 = jnp.bfloat16

HEAD_DIM = 128
LANES = 128
ROPE_THETA = 500000.0
ROPE_FRACTION = 4
EPS = 1e-6
NEG_INF = -1e30
LOG2E = math.log2(math.e)

DA_HEADS = 4
DA_QK_DIM = 64
NSA_HEADS = 8
NSA_GROUPS = 2
NSA_HPG = NSA_HEADS // NSA_GROUPS
NSA_BLOCK = 64
BLOCK_SHIFT = 6
NSA_TOP_N = 8
NSA_WINDOW = 512
NSA_FORCED_SCORE = 1e4
FOX_HEADS = 4

ATT_TILE = 256
FOX_AUG = 2 * LANES

BLK_AQ, BLK_AK, BLK_AV, BLK_BQ, BLK_BKV, BLK_CQ, BLK_CK, BLK_CV = 0, 4, 8, 12, 20, 32, 36, 40
QKV_COLS = 44 * LANES
SMALL_COLS = 3 * LANES

G_DAQ, G_DAK, G_NQ, G_NK0, G_NK1, G_NK2, G_FQ, G_FK, G_FBIAS, G_DAOUT = range(10)

VMEM_LIMIT = 56 * 1024 * 1024


def _cparams(sem):
    return pltpu.CompilerParams(dimension_semantics=sem, vmem_limit_bytes=VMEM_LIMIT)


def _dot(a, b):
    return jnp.dot(a, b, preferred_element_type=F32)


def _dot_t(a, b):
    return lax.dot_general(a, b, (((1,), (1,)), ((), ())), preferred_element_type=F32)


def _rms_rows(xf, gain):
    return xf * lax.rsqrt(jnp.mean(xf * xf, axis=-1, keepdims=True) + EPS) * gain


def _ffn_kernel(x_ref, g_ref, wg_ref, wu_ref, wd_ref, o_ref, h_scr):
    @pl.when(pl.program_id(1) == 0)
    def _():
        xf = x_ref[...]
        h_scr[...] = _rms_rows(xf, g_ref[...]).astype(BF16)
        o_ref[...] = xf

    h = h_scr[...]
    g = _dot(h, wg_ref[...])
    u = _dot(h, wu_ref[...])
    a = (g * jax.nn.sigmoid(g)) * (0.5 * u)
    o_ref[...] += _dot(a.astype(BF16), wd_ref[...])


def _ffn(x2d, gain, wg, wu, wd, layer, *, tm=1024, tf=512):
    T, D = x2d.shape
    F = wg.shape[2]
    tm = min(tm, T)
    return pl.pallas_call(
        _ffn_kernel,
        grid=(T // tm, F // tf),
        in_specs=[
            pl.BlockSpec((tm, D), lambda i, j: (i, 0)),
            pl.BlockSpec((None, 1, D), lambda i, j: (layer, 0, 0)),
            pl.BlockSpec((None, D, tf), lambda i, j: (layer, 0, j)),
            pl.BlockSpec((None, D, tf), lambda i, j: (layer, 0, j)),
            pl.BlockSpec((None, tf, D), lambda i, j: (layer, j, 0)),
        ],
        out_specs=pl.BlockSpec((tm, D), lambda i, j: (i, 0)),
        out_shape=jax.ShapeDtypeStruct((T, D), F32),
        scratch_shapes=[pltpu.VMEM((tm, D), BF16)],
        compiler_params=_cparams(("parallel", "arbitrary")),
        name="ffn",
    )(x2d, gain, wg, wu, wd)


def _cast_kernel(x_ref, o_ref):
    o_ref[...] = x_ref[...].astype(BF16)


def _to_bf16(w, *, rows):
    depth, R, C = w.shape
    spec = pl.BlockSpec((None, rows, C), lambda l, i: (l, i, 0))
    return pl.pallas_call(
        _cast_kernel,
        grid=(depth, R // rows),
        in_specs=[spec],
        out_specs=spec,
        out_shape=jax.ShapeDtypeStruct(w.shape, BF16),
        compiler_params=_cparams(("parallel", "parallel")),
        name="cast_bf16",
    )(w)


def _proj_kernel(x_ref, g_ref, w_ref, o_ref, h_scr, *, act):
    @pl.when(pl.program_id(1) == 0)
    def _():
        h_scr[...] = _rms_rows(x_ref[...], g_ref[...]).astype(BF16)

    r = _dot(h_scr[...], w_ref[...])
    if act:
        r = jax.nn.sigmoid(r)
    o_ref[...] = r.astype(o_ref.dtype)


def _proj(x2d, gain, w, layer, *, act, name, tm=1024, tn=512):
    T, D = x2d.shape
    N = w.shape[2]
    tm = min(tm, T)
    return pl.pallas_call(
        functools.partial(_proj_kernel, act=act),
        grid=(T // tm, N // tn),
        in_specs=[
            pl.BlockSpec((tm, D), lambda i, j: (i, 0)),
            pl.BlockSpec((None, 1, D), lambda i, j: (layer, 0, 0)),
            pl.BlockSpec((None, D, tn), lambda i, j: (layer, 0, j)),
        ],
        out_specs=pl.BlockSpec((tm, tn), lambda i, j: (i, j)),
        out_shape=jax.ShapeDtypeStruct((T, N), BF16),
        scratch_shapes=[pltpu.VMEM((tm, D), BF16)],
        compiler_params=_cparams(("parallel", "arbitrary")),
        name=name,
    )(x2d, gain, w)


def _split3(v):
    hi = v.astype(BF16)
    r = v - hi.astype(F32)
    mid = r.astype(BF16)
    lo = (r - mid.astype(F32)).astype(BF16)
    return hi, mid, lo


def _proj_small_kernel(x_ref, g_ref, whi_ref, wlo_ref, o_ref, ot_ref):
    h = _rms_rows(x_ref[...], g_ref[...])
    hi, mid, _ = _split3(h)
    whi = whi_ref[...]
    r = _dot(hi, whi) + (_dot(mid, whi) + _dot(hi, wlo_ref[...]))
    o_ref[...] = r
    for c in range(r.shape[1] // LANES):
        ot_ref[c * LANES:(c + 1) * LANES, :] = r[:, c * LANES:(c + 1) * LANES].T


def _proj_small(x2d, gain, whi, wlo, layer, *, tm=512):
    T, D = x2d.shape
    N = whi.shape[2]
    return pl.pallas_call(
        _proj_small_kernel,
        grid=(T // tm,),
        in_specs=[
            pl.BlockSpec((tm, D), lambda i: (i, 0)),
            pl.BlockSpec((None, 1, D), lambda i: (layer, 0, 0)),
            pl.BlockSpec((None, D, N), lambda i: (layer, 0, 0)),
            pl.BlockSpec((None, D, N), lambda i: (layer, 0, 0)),
        ],
        out_specs=(pl.BlockSpec((tm, N), lambda i: (i, 0)), pl.BlockSpec((N, tm), lambda i: (0, i))),
        out_shape=(jax.ShapeDtypeStruct((T, N), F32), jax.ShapeDtypeStruct((N, T), F32)),
        compiler_params=_cparams(("parallel",)),
        name="proj_small",
    )(x2d, gain, whi, wlo)


def _prep_kernel(p_ref, sm_ref, rope_ref, gains_ref,
                 qat_ref, ka_ref, vat_ref, qbt_ref, ks_ref, kw_ref, vst_ref, vwt_ref,
                 qct_ref, kf_ref, vct_ref, kvc_ref,
                 carry_scr, *, tiles_per_seq):
    i = pl.program_id(0)
    tm = p_ref.shape[0]
    lane = lax.broadcasted_iota(jnp.int32, (tm, LANES), 1)
    low_half = lane < DA_QK_DIM

    def raw(c):
        return p_ref[:, c * LANES:(c + 1) * LANES]

    def chunk(c):
        return raw(c).astype(F32)

    def gain(r):
        return gains_ref[r:r + 1, :]

    def norm128(xc, r):
        return _rms_rows(xc, gain(r))

    def norm64(xc, r):
        sq = xc * xc
        tot = jnp.sum(sq, axis=-1, keepdims=True)
        lo = jnp.sum(jnp.where(low_half, sq, 0.0), axis=-1, keepdims=True)
        ms = jnp.where(low_half, lo, tot - lo) * (1.0 / DA_QK_DIM)
        return xc * lax.rsqrt(ms + EPS) * gain(r)

    def rope(y, t, half):
        c, s1, s2 = rope_ref[t], rope_ref[t + 1], rope_ref[t + 2]
        return y * c + pltpu.roll(y, LANES - half, 1) * s1 + pltpu.roll(y, half, 1) * s2

    eye = jnp.where(lax.broadcasted_iota(jnp.int32, (LANES, LANES), 0)
                    == lax.broadcasted_iota(jnp.int32, (LANES, LANES), 1), 1.0, 0.0).astype(BF16)

    def tr(y):
        return _dot_t(eye, y.astype(BF16)).astype(BF16)

    half_a = DA_QK_DIM // ROPE_FRACTION // 2
    half_b = HEAD_DIM // ROPE_FRACTION // 2
    scale_a = DA_QK_DIM ** -0.5 * LOG2E
    scale_b = HEAD_DIM ** -0.5 * LOG2E

    for h in range(DA_HEADS):
        sl = slice(h * LANES, (h + 1) * LANES)
        qat_ref[h, 0] = tr(rope(norm64(chunk(BLK_AQ + h), G_DAQ), 0, half_a) * scale_a)
        ka_ref[:, sl] = rope(norm64(chunk(BLK_AK + h), G_DAK), 0, half_a).astype(BF16)
        vat_ref[h, 0] = tr(raw(BLK_AV + h))
    for h in range(NSA_HEADS):
        qbt_ref[h, 0] = tr(rope(norm128(chunk(BLK_BQ + h), G_NQ), 3, half_b) * scale_b)
    for g in range(NSA_GROUPS):
        sl = slice(g * LANES, (g + 1) * LANES)
        kvc_ref[0, g] = rope(chunk(BLK_BKV + 0 + g), 3, half_b).astype(BF16)
        kvc_ref[1, g] = raw(BLK_BKV + 2 + g)
        ks_ref[:, sl] = rope(norm128(chunk(BLK_BKV + 4 + g), G_NK1), 3, half_b).astype(BF16)
        vst_ref[g, 0] = tr(raw(BLK_BKV + 6 + g))
        kw_ref[:, sl] = rope(norm128(chunk(BLK_BKV + 8 + g), G_NK2), 3, half_b).astype(BF16)
        vwt_ref[g, 0] = tr(raw(BLK_BKV + 10 + g))

    @pl.when(i % tiles_per_seq == 0)
    def _():
        carry_scr[...] = jnp.zeros_like(carry_scr)

    z = sm_ref[...] + gain(G_FBIAS)
    logf = jnp.minimum(z, 0.0) - jnp.log(1.0 + jnp.exp(-jnp.abs(z)))
    r_id = lax.broadcasted_iota(jnp.int32, (tm, tm), 0)
    c_id = lax.broadcasted_iota(jnp.int32, (tm, tm), 1)
    tri = jnp.where(r_id >= c_id, 1.0, 0.0).astype(BF16)
    hi, mid, lo = _split3(logf)
    cum = _dot(tri, hi) + (_dot(tri, mid) + _dot(tri, lo)) + carry_scr[0:1, :]
    carry_scr[0:1, :] = cum[tm - 1:tm, :]

    for h in range(FOX_HEADS):
        c = jnp.broadcast_to(cum[:, h:h + 1], (tm, LANES)) * LOG2E
        c_hi, c_mid, c_lo = (v.astype(F32) for v in _split3(c))
        ones = jnp.where(lane < 6, 1.0, 0.0)
        q_aug = jnp.where(lane == 0, c_hi, jnp.where(lane == 1, c_mid, jnp.where(lane == 2, c_lo, ones)))
        k_aug = jnp.where(lane == 3, -c_hi, jnp.where(lane == 4, -c_mid, jnp.where(lane == 5, -c_lo, ones)))
        qct_ref[h, 0, 0:LANES, :] = tr(norm128(chunk(BLK_CQ + h), G_FQ) * scale_b)
        qct_ref[h, 0, LANES:FOX_AUG, :] = tr(q_aug)
        kf_ref[:, h * FOX_AUG:h * FOX_AUG + LANES] = norm128(chunk(BLK_CK + h), G_FK).astype(BF16)
        kf_ref[:, h * FOX_AUG + LANES:(h + 1) * FOX_AUG] = k_aug.astype(BF16)
        vct_ref[h, 0] = tr(raw(BLK_CV + h))


def _prep(p_qkv, small, rope_tab, gains, *, seq):
    tm = ATT_TILE
    T = p_qkv.shape[0]
    nt = T // tm
    G = NSA_GROUPS

    def rows(w):
        return pl.BlockSpec((tm, w), lambda i: (i, 0))

    def tiles_t(n, d=LANES):
        return (jax.ShapeDtypeStruct((n, nt, d, tm), BF16), pl.BlockSpec((n, 1, d, tm), lambda i: (0, i, 0, 0)))

    def nat(w):
        return (jax.ShapeDtypeStruct((T, w), BF16), rows(w))

    def grp():
        return (jax.ShapeDtypeStruct((2, G, T, LANES), BF16),
                pl.BlockSpec((2, G, tm, LANES), lambda i: (0, 0, i, 0)))

    outs = [
        tiles_t(DA_HEADS), nat(DA_HEADS * LANES), tiles_t(DA_HEADS),
        tiles_t(NSA_HEADS), nat(G * LANES), nat(G * LANES), tiles_t(G), tiles_t(G),
        tiles_t(FOX_HEADS, FOX_AUG), nat(FOX_HEADS * FOX_AUG), tiles_t(FOX_HEADS),
        grp(),
    ]
    return pl.pallas_call(
        functools.partial(_prep_kernel, tiles_per_seq=seq // tm),
        grid=(nt,),
        in_specs=[
            rows(QKV_COLS),
            rows(LANES),
            pl.BlockSpec((6, tm, LANES), lambda i: (0, i, 0)),
            pl.BlockSpec(gains.shape, lambda i: (0, 0)),
        ],
        out_specs=tuple(o[1] for o in outs),
        out_shape=tuple(o[0] for o in outs),
        scratch_shapes=[pltpu.VMEM((8, LANES), F32)],
        compiler_params=_cparams(("arbitrary",)),
        name="prep",
    )(p_qkv, small, rope_tab, gains)


def _compress_kernel(x_ref, pos_ref, w1_ref, w2_ref, gain_ref, kc_ref, vct_ref, *, batch):
    def mlp(kv):
        blk = (x_ref[kv].astype(F32) + pos_ref[kv]).astype(BF16)
        hid = jax.nn.gelu(_dot(blk, w1_ref[kv].astype(BF16)))
        return _dot(hid.astype(BF16), w2_ref[kv].astype(BF16))

    kc = _rms_rows(mlp(0), gain_ref[...])
    vc = mlp(1)
    nb = kc.shape[0] // batch
    pad = jnp.zeros((LANES - nb, LANES), F32)
    for b in range(batch):
        kc_ref[b] = jnp.concatenate([kc[b * nb:(b + 1) * nb], pad], axis=0).astype(BF16)
        vct_ref[b] = jnp.concatenate([vc[b * nb:(b + 1) * nb], pad], axis=0).T.astype(BF16)


def _compress(xkv, pos, w1, w2, gain, layer, *, batch):
    _, G, R, K = xkv.shape
    d = w1.shape[-1]
    out = jax.ShapeDtypeStruct((G, batch, LANES, LANES), BF16)
    ospec = pl.BlockSpec((None, batch, LANES, LANES), lambda g: (g, 0, 0, 0))
    return pl.pallas_call(
        functools.partial(_compress_kernel, batch=batch),
        grid=(G,),
        in_specs=[
            pl.BlockSpec((2, None, R, K), lambda g: (0, g, 0, 0)),
            pl.BlockSpec((None, 2, 1, K), lambda g: (layer, 0, 0, 0)),
            pl.BlockSpec((None, 2, K, d), lambda g: (layer, 0, 0, 0)),
            pl.BlockSpec((None, 2, d, d), lambda g: (layer, 0, 0, 0)),
            pl.BlockSpec((1, d), lambda g: (0, 0)),
        ],
        out_specs=(ospec, ospec),
        out_shape=(out, out),
        compiler_params=_cparams(("arbitrary",)),
        name="compress",
    )(xkv, pos, w1, w2, gain)


def _online_t(s_ts, mask, carry, v_ts):
    stats = []
    for s_t, (m, l, _) in zip(s_ts, carry):
        if mask is not None:
            s_t = jnp.where(mask, s_t, NEG_INF)
        m_new = jnp.maximum(m, jnp.max(s_t, axis=0, keepdims=True))
        alpha = jnp.exp2(m - m_new)
        p = jnp.exp2(s_t - m_new)
        l_new = alpha * l + jnp.sum(p, axis=0, keepdims=True)
        stats.append((m_new, l_new, alpha, p.astype(BF16)))
    return tuple((m_new, l_new, alpha * acc_t + _dot(v_t, p))
                 for (m_new, l_new, alpha, p), (_, _, acc_t), v_t in zip(stats, carry, v_ts))


def _softmax_init(tq, d):
    return (jnp.full((1, tq), NEG_INF, F32), jnp.zeros((1, tq), F32), jnp.zeros((d, tq), F32))


def _causal_mask_t(t):
    return lax.broadcasted_iota(jnp.int32, (t, t), 0) <= lax.broadcasted_iota(jnp.int32, (t, t), 1)


def _diff_kernel(qt_ref, k_ref, vt_ref, lam_ref, gains_ref, o_ref, *, lam_init):
    i = pl.program_id(2)
    t = ATT_TILE
    nh = qt_ref.shape[0]
    drow = lax.broadcasted_iota(jnp.int32, (LANES, t), 0)
    qts = []
    for h in range(nh):
        q = qt_ref[h, 0]
        qts.append(jnp.where(drow < DA_QK_DIM, q, jnp.zeros_like(q)))
        qts.append(jnp.where(drow < DA_QK_DIM, jnp.zeros_like(q), q))

    def tile(kt, carry, mask):
        off = pl.multiple_of(kt * t, t)
        ks = [k_ref[pl.ds(off, t), h * LANES:(h + 1) * LANES] for h in range(nh)]
        s_ts = [_dot(ks[c // 2], qts[c]) for c in range(2 * nh)]
        return _online_t(s_ts, mask, carry, [vt_ref[c // 2, kt] for c in range(2 * nh)])

    init = tuple(_softmax_init(t, LANES) for _ in range(2 * nh))
    carry = lax.fori_loop(0, i, lambda kt, c: tile(kt, c, None), init)
    carry = tile(i, carry, _causal_mask_t(t))

    lp = lam_ref[...]
    lam = (jnp.exp(jnp.sum(lp[0:1] * lp[1:2], axis=-1, keepdims=True))
           - jnp.exp(jnp.sum(lp[2:3] * lp[3:4], axis=-1, keepdims=True)) + lam_init)
    for h in range(nh):
        (_, l0, a0), (_, l1, a1) = carry[2 * h], carry[2 * h + 1]
        o = (a0 / l0 - lam * (a1 / l1)).T
        o = _rms_rows(o, gains_ref[G_DAOUT:G_DAOUT + 1, :]) * (1.0 - lam_init)
        o_ref[:, h * LANES:(h + 1) * LANES] = o.astype(BF16)


def _diff_attention(qat, ka, vat, da_lambda, gains, *, batch, seq, lam_init, heads_per_step=2):
    t = ATT_TILE
    nq = seq // t
    T = batch * seq
    hp = heads_per_step
    return pl.pallas_call(
        functools.partial(_diff_kernel, lam_init=lam_init),
        grid=(batch, DA_HEADS // hp, nq),
        in_specs=[
            pl.BlockSpec((hp, 1, LANES, t), lambda b, p, i: (p, b * nq + i, 0, 0)),
            pl.BlockSpec((seq, hp * LANES), lambda b, p, i: (b, p)),
            pl.BlockSpec((hp, nq, LANES, t), lambda b, p, i: (p, b, 0, 0)),
            pl.BlockSpec(da_lambda.shape, lambda b, p, i: (0, 0)),
            pl.BlockSpec(gains.shape, lambda b, p, i: (0, 0)),
        ],
        out_specs=pl.BlockSpec((t, hp * LANES), lambda b, p, i: (b * nq + i, p)),
        out_shape=jax.ShapeDtypeStruct((T, DA_HEADS * LANES), BF16),
        compiler_params=_cparams(("parallel", "parallel", "arbitrary")),
        name="diff_attn",
    )(qat, ka, vat, da_lambda, gains)


def _fox_kernel(qt_ref, k_ref, vt_ref, o_ref):
    i = pl.program_id(1)
    t = ATT_TILE
    nh = qt_ref.shape[0]
    qts = [qt_ref[h, 0] for h in range(nh)]

    def tile(kt, carry, mask):
        off = pl.multiple_of(kt * t, t)
        s_ts = [_dot(k_ref[pl.ds(off, t), h * FOX_AUG:(h + 1) * FOX_AUG], qts[h]) for h in range(nh)]
        return _online_t(s_ts, mask, carry, [vt_ref[h, kt] for h in range(nh)])

    init = tuple(_softmax_init(t, LANES) for _ in range(nh))
    carry = lax.fori_loop(0, i, lambda kt, c: tile(kt, c, None), init)
    carry = tile(i, carry, _causal_mask_t(t))
    for h in range(nh):
        _, l, acc = carry[h]
        o_ref[:, h * LANES:(h + 1) * LANES] = (acc / l).T.astype(BF16)


def _fox_attention(qct, kf, vct, *, batch, seq):
    t = ATT_TILE
    nq = seq // t
    T = batch * seq
    nh = FOX_HEADS
    return pl.pallas_call(
        _fox_kernel,
        grid=(batch, nq),
        in_specs=[
            pl.BlockSpec((nh, 1, FOX_AUG, t), lambda b, i: (0, b * nq + i, 0, 0)),
            pl.BlockSpec((seq, nh * FOX_AUG), lambda b, i: (b, 0)),
            pl.BlockSpec((nh, nq, LANES, t), lambda b, i: (0, b, 0, 0)),
        ],
        out_specs=pl.BlockSpec((t, nh * LANES), lambda b, i: (b * nq + i, 0)),
        out_shape=jax.ShapeDtypeStruct((T, nh * LANES), BF16),
        compiler_params=_cparams(("parallel", "arbitrary")),
        name="fox_attn",
    )(qct, kf, vct)


def _nsa_kernel(qt_ref, ks_ref, vst_ref, kw_ref, vwt_ref, kc_ref, vct_ref, gt_ref, o_ref, *, n_sel):
    i = pl.program_id(2)
    t = ATT_TILE
    lo = i * t
    nh = qt_ref.shape[0]
    qts = [qt_ref[h, 0] for h in range(nh)]
    tq_row = lo + lax.broadcasted_iota(jnp.int32, (1, t), 1)

    kc = kc_ref[...]
    vct = vct_ref[...]
    nidx = lax.broadcasted_iota(jnp.int32, (LANES, t), 0)
    cmask = nidx * NSA_BLOCK + (NSA_BLOCK - 1) <= tq_row
    importance = jnp.zeros((LANES, t), F32)
    o_cmp = []
    for h in range(nh):
        s = jnp.where(cmask, _dot(kc, qts[h]), NEG_INF)
        e = jnp.where(cmask, jnp.exp2(s - jnp.max(s, axis=0, keepdims=True)), 0.0)
        p = e / jnp.maximum(jnp.sum(e, axis=0, keepdims=True), 1e-30)
        importance = importance + p
        o_cmp.append(_dot(vct, p.astype(BF16)))

    cur = tq_row >> BLOCK_SHIFT
    forced = (nidx == 0) | (nidx == cur) | (nidx == cur - 1)
    score = jnp.where(forced, NSA_FORCED_SCORE, jnp.where(nidx <= cur, importance, -1.0))
    nidx_f = nidx.astype(F32)
    sel = jnp.zeros((LANES, t), F32)
    for _ in range(n_sel):
        best = jnp.max(score, axis=0, keepdims=True)
        first = jnp.min(jnp.where(score == best, nidx_f, float(LANES)), axis=0, keepdims=True)
        hit = nidx_f == first
        sel = jnp.where(hit, 1.0, sel)
        score = jnp.where(hit, -3e38, score)
    sel_b = sel.astype(BF16)

    e_key = lax.broadcasted_iota(jnp.int32, (t, LANES), 0) >> BLOCK_SHIFT
    e_blk = lax.broadcasted_iota(jnp.int32, (t, LANES), 1)
    krow = lax.broadcasted_iota(jnp.int32, (t, t), 0)
    qcol = lax.broadcasted_iota(jnp.int32, (t, t), 1)
    causal = krow <= qcol
    init = tuple(_softmax_init(t, LANES) for _ in range(nh))

    def tile(k_ref, vt_ref, kt, carry, mask):
        off = pl.multiple_of(kt * t, t)
        k = k_ref[pl.ds(off, t), :]
        return _online_t([_dot(k, qts[h]) for h in range(nh)], mask, carry, [vt_ref[kt]] * nh)

    def selected(kt):
        expand = jnp.where(e_blk == e_key + kt * (t // NSA_BLOCK), 1.0, 0.0).astype(BF16)
        return _dot(expand, sel_b) > 0.5

    carry = lax.fori_loop(0, i, lambda kt, c: tile(ks_ref, vst_ref, kt, c, selected(kt)), init)
    carry = tile(ks_ref, vst_ref, i, carry, selected(i) & causal)
    o_slc = [acc / l for (_, l, acc) in carry]

    def in_window(kt):
        return krow + kt * t > (qcol + lo) - NSA_WINDOW

    kt0 = jnp.maximum(i - NSA_WINDOW // t, 0)
    carry = lax.fori_loop(kt0, i, lambda kt, c: tile(kw_ref, vwt_ref, kt, c, in_window(kt)), init)
    carry = tile(kw_ref, vwt_ref, i, carry, causal)
    o_win = [acc / l for (_, l, acc) in carry]

    gt = jax.nn.sigmoid(gt_ref[...])
    for h in range(nh):
        o = (gt[3 * h:3 * h + 1, :] * o_cmp[h] + gt[3 * h + 1:3 * h + 2, :] * o_slc[h]
             + gt[3 * h + 2:3 * h + 3, :] * o_win[h])
        o_ref[:, h * LANES:(h + 1) * LANES] = o.T.astype(BF16)


def _nsa_attention(qbt, ks, kw, vst, vwt, kc, vct, small_t, *, batch, seq):
    t = ATT_TILE
    nq = seq // t
    T = batch * seq
    n_sel = min(NSA_TOP_N, seq // NSA_BLOCK)
    hw = NSA_HPG * LANES
    k_spec = pl.BlockSpec((seq, LANES), lambda b, g, i: (b, g))
    vt_spec = pl.BlockSpec((None, nq, LANES, t), lambda b, g, i: (g, b, 0, 0))
    c_spec = pl.BlockSpec((None, None, LANES, LANES), lambda b, g, i: (g, b, 0, 0))
    return pl.pallas_call(
        functools.partial(_nsa_kernel, n_sel=n_sel),
        grid=(batch, NSA_GROUPS, nq),
        in_specs=[
            pl.BlockSpec((NSA_HPG, 1, LANES, t), lambda b, g, i: (g, b * nq + i, 0, 0)),
            k_spec, vt_spec, k_spec, vt_spec, c_spec, c_spec,
            pl.BlockSpec((LANES, t), lambda b, g, i: (1 + g, b * nq + i)),
        ],
        out_specs=pl.BlockSpec((t, hw), lambda b, g, i: (b * nq + i, g)),
        out_shape=jax.ShapeDtypeStruct((T, NSA_HEADS * LANES), BF16),
        compiler_params=_cparams(("parallel", "parallel", "arbitrary")),
        name="nsa_attn",
    )(qbt, ks, vst, kw, vwt, kc, vct, small_t)


def _merge_kernel(x_ref, oa_ref, ob_ref, oc_ref, g0_ref, g1_ref, g2_ref,
                  wa_ref, wb_ref, wc_ref, wo_ref, o_ref):
    y = (g0_ref[...].astype(F32) * _dot(oa_ref[...], wa_ref[...])
         + g1_ref[...].astype(F32) * _dot(ob_ref[...], wb_ref[...])
         + g2_ref[...].astype(F32) * _dot(oc_ref[...], wc_ref[...]))
    o_ref[...] = x_ref[...] + _dot(y.astype(BF16), wo_ref[...])


def _merge(x2d, oa, ob, oc, p_gm, wa, wb, wc, wo, layer, *, tm=256):
    T, D = x2d.shape

    def rows(w, c=0):
        return pl.BlockSpec((tm, w), lambda i: (i, c))

    def full(w):
        return pl.BlockSpec((None,) + w.shape[1:], lambda i: (layer, 0, 0))

    return pl.pallas_call(
        _merge_kernel,
        grid=(T // tm,),
        in_specs=[rows(D), rows(oa.shape[1]), rows(ob.shape[1]), rows(oc.shape[1]),
                  rows(D, 0), rows(D, 1), rows(D, 2), full(wa), full(wb), full(wc), full(wo)],
        out_specs=rows(D),
        out_shape=jax.ShapeDtypeStruct((T, D), F32),
        compiler_params=_cparams(("parallel",)),
        name="merge",
    )(x2d, oa, ob, oc, p_gm, p_gm, p_gm, wa, wb, wc, wo)


def _rope_tables(positions, width, rot_dim):
    half = rot_dim // 2
    inv_freq = ROPE_THETA ** (-jnp.arange(0, rot_dim, 2, dtype=F32) / rot_dim)
    ang = positions.astype(F32).reshape(-1, 1) * inv_freq
    cos, sin = jnp.cos(ang), jnp.sin(ang)
    T = ang.shape[0]
    rest = width - rot_dim
    c = jnp.concatenate([cos, cos, jnp.ones((T, rest), F32)], axis=1)
    s1 = jnp.concatenate([-sin, jnp.zeros((T, half + rest), F32)], axis=1)
    s2 = jnp.concatenate([jnp.zeros((T, half), F32), sin, jnp.zeros((T, rest), F32)], axis=1)
    rep = LANES // width
    return [jnp.tile(t, (1, rep)) for t in (c, s1, s2)]


def _pad_lanes(v):
    return jnp.pad(v.astype(F32), (0, LANES - v.shape[0]))


def kernel(x, positions, ffn1_norm, ffn1_w_gate, ffn1_w_up, ffn1_w_down, mix_norm, w_in, da_q_norm, da_k_norm, da_lambda, da_out_norm, nsa_q_norm, nsa_k_norm, nsa_cmp_pos, nsa_cmp_w1, nsa_cmp_w2, fox_q_norm, fox_k_norm, fox_f_bias, w_branch_a, w_branch_b, w_branch_c, w_out, ffn2_norm, ffn2_w_gate, ffn2_w_up, ffn2_w_down):
    B, S, D = x.shape
    depth = w_in.shape[0]
    T = B * S
    G, L, d = NSA_GROUPS, NSA_BLOCK, HEAD_DIM
    nb = S // L
    assert S % ATT_TILE == 0 and nb <= LANES

    rope_tab = jnp.stack(_rope_tables(positions, DA_QK_DIM, DA_QK_DIM // ROPE_FRACTION)
                         + _rope_tables(positions, HEAD_DIM, HEAD_DIM // ROPE_FRACTION))

    widths = [512, 512, 512, 1024, 1536, 24, 512, 512, 512, 4, 3 * D]
    offs = [0]
    for w in widths:
        offs.append(offs[-1] + w)
    (o_aq, o_ak, o_av, o_bq, o_bkv, o_bg, o_cq, o_ck, o_cv, o_cf, o_gm, o_end) = offs

    w_qkv = jnp.concatenate([w_in[:, :, o_aq:o_bg], w_in[:, :, o_cq:o_cf]], axis=2).astype(BF16)
    w_gm = w_in[:, :, o_gm:o_end].astype(BF16)
    zpad = lambda n: jnp.zeros((depth, D, n), F32)
    w_small = jnp.concatenate([
        w_in[:, :, o_cf:o_gm], zpad(LANES - FOX_HEADS),
        w_in[:, :, o_bg:o_bg + 12], zpad(LANES - 12),
        w_in[:, :, o_bg + 12:o_bg + 24], zpad(LANES - 12)], axis=2)
    w_small_hi = w_small.astype(BF16)
    w_small_lo = (w_small - w_small_hi.astype(F32)).astype(BF16)
    ffn_w = [(_to_bf16(wg, rows=256), _to_bf16(wu, rows=256), _to_bf16(wd, rows=512))
             for wg, wu, wd in ((ffn1_w_gate, ffn1_w_up, ffn1_w_down), (ffn2_w_gate, ffn2_w_up, ffn2_w_down))]
    ffn_g = [ffn1_norm.reshape(depth, 1, D), ffn2_norm.reshape(depth, 1, D)]
    g_mix = mix_norm.reshape(depth, 1, D)
    w_a, w_b, w_c, w_o = (_to_bf16(w, rows=512) for w in (w_branch_a, w_branch_b, w_branch_c, w_out))
    cmp_pos = nsa_cmp_pos.reshape(depth, 2, 1, L * d)

    xc = x.reshape(T, D)
    for l in range(depth):
        lam_init = 0.8 - 0.6 * math.exp(-0.3 * l)
        gains = jnp.stack([
            jnp.tile(da_q_norm[l], 2), jnp.tile(da_k_norm[l], 2), nsa_q_norm[l],
            nsa_k_norm[l, 0], nsa_k_norm[l, 1], nsa_k_norm[l, 2], fox_q_norm[l], fox_k_norm[l],
            _pad_lanes(fox_f_bias[l]), da_out_norm[l]] + [jnp.zeros((LANES,), F32)] * 6).astype(F32)

        xc = _ffn(xc, ffn_g[0], *ffn_w[0], l)

        p_qkv = _proj(xc, g_mix, w_qkv, l, act=False, name="proj_qkv")
        p_gm = _proj(xc, g_mix, w_gm, l, act=True, name="proj_gate")
        small, small_t = _proj_small(xc, g_mix, w_small_hi, w_small_lo, l)

        (qat, ka, vat, qbt, ks, kw, vst, vwt, qct, kf, vct, kv_cmp) = _prep(
            p_qkv, small, rope_tab, gains, seq=S)

        kc, vcmp_t = _compress(kv_cmp.reshape(2, G, B * nb, L * d), cmp_pos, nsa_cmp_w1, nsa_cmp_w2,
                               nsa_k_norm[l, 0].reshape(1, d), l, batch=B)

        o_a = _diff_attention(qat, ka, vat, da_lambda[l], gains, batch=B, seq=S, lam_init=lam_init)
        o_b = _nsa_attention(qbt, ks, kw, vst, vwt, kc, vcmp_t, small_t, batch=B, seq=S)
        o_c = _fox_attention(qct, kf, vct, batch=B, seq=S)

        xc = _merge(xc, o_a, o_b, o_c, p_gm, w_a, w_b, w_c, w_o, l)

        xc = _ffn(xc, ffn_g[1], *ffn_w[1], l)
    return xc.reshape(B, S, D)
```

```python
import functools
import math

import jax
import jax.numpy as jnp
from jax import lax
from jax.experimental import pallas as pl
from jax.experimental.pallas import tpu as pltpu

F32 = jnp.float32
BF16 = jnp.bfloat16

HEAD_DIM = 128
LANES = 128
ROPE_THETA = 500000.0
ROPE_FRACTION = 4
EPS = 1e-6
NEG_INF = -1e30
LOG2E = math.log2(math.e)

DA_HEADS = 4
DA_QK_DIM = 64
NSA_HEADS = 8
NSA_GROUPS = 2
NSA_HPG = NSA_HEADS // NSA_GROUPS
NSA_BLOCK = 64
BLOCK_SHIFT = 6
NSA_TOP_N = 8
NSA_WINDOW = 512
NSA_FORCED_SCORE = 1e4
FOX_HEADS = 4

ATT_TILE = 256
FOX_AUG = 2 * LANES
VT_ROWS = LANES + 16

BLK_AQ, BLK_AK, BLK_AV, BLK_BQ, BLK_BKV, BLK_CQ, BLK_CK, BLK_CV = 0, 4, 8, 12, 20, 32, 36, 40
QKV_COLS = 44 * LANES
SMALL_COLS = 3 * LANES

G_DAQ, G_DAK, G_NQ, G_NK0, G_NK1, G_NK2, G_FQ, G_FK, G_FBIAS, G_DAOUT = range(10)

VMEM_LIMIT = 56 * 1024 * 1024


def _cparams(sem):
    return pltpu.CompilerParams(dimension_semantics=sem, vmem_limit_bytes=VMEM_LIMIT)


def _dot(a, b):
    return jnp.dot(a, b, preferred_element_type=F32)


def _dot_t(a, b):
    return lax.dot_general(a, b, (((1,), (1,)), ((), ())), preferred_element_type=F32)


def _rms_rows(xf, gain):
    return xf * lax.rsqrt(jnp.mean(xf * xf, axis=-1, keepdims=True) + EPS) * gain


def _ffn_kernel(x_ref, g_ref, wg_ref, wu_ref, wd_ref, o_ref, h_scr):
    @pl.when(pl.program_id(1) == 0)
    def _():
        xf = x_ref[...]
        h_scr[...] = _rms_rows(xf, g_ref[...]).astype(BF16)
        o_ref[...] = xf

    h = h_scr[...]
    g = _dot(h, wg_ref[...])
    u = _dot(h, wu_ref[...])
    a = (g * jax.nn.sigmoid(g)) * (0.5 * u)
    o_ref[...] += _dot(a.astype(BF16), wd_ref[...])


def _ffn(x2d, gain, wg, wu, wd, layer, *, tm=1024, tf=512):
    T, D = x2d.shape
    F = wg.shape[2]
    tm = min(tm, T)
    return pl.pallas_call(
        _ffn_kernel,
        grid=(T // tm, F // tf),
        in_specs=[
            pl.BlockSpec((tm, D), lambda i, j: (i, 0)),
            pl.BlockSpec((None, 1, D), lambda i, j: (layer, 0, 0)),
            pl.BlockSpec((None, D, tf), lambda i, j: (layer, 0, j)),
            pl.BlockSpec((None, D, tf), lambda i, j: (layer, 0, j)),
            pl.BlockSpec((None, tf, D), lambda i, j: (layer, j, 0)),
        ],
        out_specs=pl.BlockSpec((tm, D), lambda i, j: (i, 0)),
        out_shape=jax.ShapeDtypeStruct((T, D), F32),
        scratch_shapes=[pltpu.VMEM((tm, D), BF16)],
        compiler_params=_cparams(("parallel", "arbitrary")),
        name="ffn",
    )(x2d, gain, wg, wu, wd)


def _cast_kernel(x_ref, o_ref):
    o_ref[...] = x_ref[...].astype(BF16)


def _to_bf16(w, *, rows):
    depth, R, C = w.shape
    spec = pl.BlockSpec((None, rows, C), lambda l, i: (l, i, 0))
    return pl.pallas_call(
        _cast_kernel,
        grid=(depth, R // rows),
        in_specs=[spec],
        out_specs=spec,
        out_shape=jax.ShapeDtypeStruct(w.shape, BF16),
        compiler_params=_cparams(("parallel", "parallel")),
        name="cast_bf16",
    )(w)


def _repack_kernel(w_ref, *o_refs, segments):
    for o_ref, segs in zip(o_refs, segments):
        dst = 0
        for lo, hi in segs:
            o_ref[:, dst:dst + hi - lo] = w_ref[:, lo:hi].astype(BF16)
            dst += hi - lo


def _repack(w, segments, *, rows=128):
    depth, R, C = w.shape
    widths = [sum(hi - lo for lo, hi in segs) for segs in segments]
    return pl.pallas_call(
        functools.partial(_repack_kernel, segments=segments),
        grid=(depth, R // rows),
        in_specs=[pl.BlockSpec((None, rows, C), lambda l, i: (l, i, 0))],
        out_specs=tuple(pl.BlockSpec((None, rows, n), lambda l, i: (l, i, 0)) for n in widths),
        out_shape=tuple(jax.ShapeDtypeStruct((depth, R, n), BF16) for n in widths),
        compiler_params=_cparams(("parallel", "parallel")),
        name="repack_w_in",
    )(w)


def _proj_kernel(x_ref, g_ref, w_ref, o_ref, h_scr, *, act):
    @pl.when(pl.program_id(1) == 0)
    def _():
        h_scr[...] = _rms_rows(x_ref[...], g_ref[...]).astype(BF16)

    r = _dot(h_scr[...], w_ref[...])
    if act:
        r = jax.nn.sigmoid(r)
    o_ref[...] = r.astype(o_ref.dtype)


def _proj(x2d, gain, w, layer, *, act, name, tm=1024, tn=512):
    T, D = x2d.shape
    N = w.shape[2]
    tm = min(tm, T)
    return pl.pallas_call(
        functools.partial(_proj_kernel, act=act),
        grid=(T // tm, N // tn),
        in_specs=[
            pl.BlockSpec((tm, D), lambda i, j: (i, 0)),
            pl.BlockSpec((None, 1, D), lambda i, j: (layer, 0, 0)),
            pl.BlockSpec((None, D, tn), lambda i, j: (layer, 0, j)),
        ],
        out_specs=pl.BlockSpec((tm, tn), lambda i, j: (i, j)),
        out_shape=jax.ShapeDtypeStruct((T, N), BF16),
        scratch_shapes=[pltpu.VMEM((tm, D), BF16)],
        compiler_params=_cparams(("parallel", "arbitrary")),
        name=name,
    )(x2d, gain, w)


def _split3(v):
    hi = v.astype(BF16)
    r = v - hi.astype(F32)
    mid = r.astype(BF16)
    lo = (r - mid.astype(F32)).astype(BF16)
    return hi, mid, lo


def _proj_small_kernel(x_ref, g_ref, whi_ref, wlo_ref, o_ref, ot_ref):
    h = _rms_rows(x_ref[...], g_ref[...])
    hi, mid, _ = _split3(h)
    whi = whi_ref[...]
    r = _dot(hi, whi) + (_dot(mid, whi) + _dot(hi, wlo_ref[...]))
    o_ref[...] = r
    for c in range(r.shape[1] // LANES):
        ot_ref[c * LANES:(c + 1) * LANES, :] = r[:, c * LANES:(c + 1) * LANES].T


def _proj_small(x2d, gain, whi, wlo, layer, *, tm=512):
    T, D = x2d.shape
    N = whi.shape[2]
    return pl.pallas_call(
        _proj_small_kernel,
        grid=(T // tm,),
        in_specs=[
            pl.BlockSpec((tm, D), lambda i: (i, 0)),
            pl.BlockSpec((None, 1, D), lambda i: (layer, 0, 0)),
            pl.BlockSpec((None, D, N), lambda i: (layer, 0, 0)),
            pl.BlockSpec((None, D, N), lambda i: (layer, 0, 0)),
        ],
        out_specs=(pl.BlockSpec((tm, N), lambda i: (i, 0)), pl.BlockSpec((N, tm), lambda i: (0, i))),
        out_shape=(jax.ShapeDtypeStruct((T, N), F32), jax.ShapeDtypeStruct((N, T), F32)),
        compiler_params=_cparams(("parallel",)),
        name="proj_small",
    )(x2d, gain, whi, wlo)


def _prep_kernel(p_ref, sm_ref, rope_ref, gains_ref,
                 qat_ref, ka_ref, vat_ref, qbt_ref, ks_ref, kw_ref, vst_ref, vwt_ref,
                 qct_ref, kf_ref, vct_ref, kvc_ref,
                 carry_scr, *, tiles_per_seq):
    i = pl.program_id(0)
    tm = p_ref.shape[0]
    lane = lax.broadcasted_iota(jnp.int32, (tm, LANES), 1)
    src = lax.broadcasted_iota(jnp.int32, (LANES, LANES), 0)
    dst = lax.broadcasted_iota(jnp.int32, (LANES, LANES), 1)

    def const(cond, val=1.0):
        return jnp.where(cond, val, 0.0).astype(BF16)

    def group_mean(width):
        shift = width.bit_length() - 1
        return const((src >> shift) == (dst >> shift), 1.0 / width)

    def swap_halves(width, half):
        d = dst & (width - 1)
        return const(((d < half) & (src == dst + half)) | ((d >= half) & (d < 2 * half) & (src == dst - half)))

    mean128, mean64 = group_mean(HEAD_DIM), group_mean(DA_QK_DIM)
    eye = const(src == dst)

    def dot2(v, mat):
        hi = v.astype(BF16)
        lo = (v - hi.astype(F32)).astype(BF16)
        return _dot(hi, mat) + _dot(lo, mat)

    def raw(c):
        return p_ref[:, c * LANES:(c + 1) * LANES]

    def chunk(c):
        return raw(c).astype(F32)

    def gain(r):
        return gains_ref[r:r + 1, :]


    def tr(y):
        return _dot_t(eye, y.astype(BF16)).astype(BF16)

    def put_vt(ref, n, c):
        ref[n, 0, 0:LANES, :] = tr(raw(c))
        ref[n, 0, LANES:VT_ROWS, :] = jnp.ones((VT_ROWS - LANES, tm), BF16)

    half_a = swap_halves(DA_QK_DIM, DA_QK_DIM // ROPE_FRACTION // 2)
    half_b = swap_halves(HEAD_DIM, HEAD_DIM // ROPE_FRACTION // 2)
    scale_a = DA_QK_DIM ** -0.5 * LOG2E
    scale_b = HEAD_DIM ** -0.5 * LOG2E

    def run_group(specs):
        xs = [chunk(c) for c, *_ in specs]
        ms = [None if nm is None else dot2(x * x, nm[1]) for x, (_, nm, *_) in zip(xs, specs)]
        ys = [x if nm is None else x * lax.rsqrt(m + EPS) * gain(nm[0])
              for x, m, (_, nm, *_) in zip(xs, ms, specs)]
        ps = [None if rp is None else dot2(y, rp[1]) for y, (_, _, rp, *_) in zip(ys, specs)]
        rs = [y if rp is None else y * rope_ref[rp[0]] + p * rope_ref[rp[0] + 1]
              for y, p, (_, _, rp, *_) in zip(ys, ps, specs)]
        rs = [r if sc == 1.0 else r * sc for r, (_, _, _, sc, _, _) in zip(rs, specs)]
        outs = [tr(r) if tp else r.astype(BF16) for r, (_, _, _, _, tp, _) in zip(rs, specs)]
        for o, spec in zip(outs, specs):
            spec[-1](o)

    def nat_sink(ref, lo):
        def sink(o):
            ref[:, lo:lo + LANES] = o
        return sink

    def tile_sink(ref, n, rows=slice(None)):
        def sink(o):
            ref[n, 0, rows, :] = o
        return sink

    def grp_sink(n, g):
        def sink(o):
            kvc_ref[n, g] = o
        return sink

    rope_a, rope_b = (0, half_a), (2, half_b)
    run_group([(BLK_AQ + h, (G_DAQ, mean64), rope_a, scale_a, True, tile_sink(qat_ref, h))
               for h in range(DA_HEADS)]
              + [(BLK_AK + h, (G_DAK, mean64), rope_a, 1.0, False, nat_sink(ka_ref, h * LANES))
                 for h in range(DA_HEADS)])
    run_group([(BLK_BQ + h, (G_NQ, mean128), rope_b, scale_b, True, tile_sink(qbt_ref, h))
               for h in range(NSA_HEADS)])
    run_group([(BLK_BKV + 0 + g, None, rope_b, 1.0, False, grp_sink(0, g)) for g in range(NSA_GROUPS)]
              + [(BLK_BKV + 4 + g, (G_NK1, mean128), rope_b, 1.0, False, nat_sink(ks_ref, g * LANES))
                 for g in range(NSA_GROUPS)]
              + [(BLK_BKV + 8 + g, (G_NK2, mean128), rope_b, 1.0, False, nat_sink(kw_ref, g * LANES))
                 for g in range(NSA_GROUPS)])
    run_group([(BLK_CQ + h, (G_FQ, mean128), None, scale_b, True, tile_sink(qct_ref, h, slice(0, LANES)))
               for h in range(FOX_HEADS)]
              + [(BLK_CK + h, (G_FK, mean128), None, 1.0, False, nat_sink(kf_ref, h * FOX_AUG))
                 for h in range(FOX_HEADS)])
    for g in range(NSA_GROUPS):
        kvc_ref[1, g] = raw(BLK_BKV + 2 + g)
        put_vt(vst_ref, g, BLK_BKV + 6 + g)
        put_vt(vwt_ref, g, BLK_BKV + 10 + g)
    for h in range(DA_HEADS):
        put_vt(vat_ref, h, BLK_AV + h)
    for h in range(FOX_HEADS):
        put_vt(vct_ref, h, BLK_CV + h)

    @pl.when(i % tiles_per_seq == 0)
    def _():
        carry_scr[...] = jnp.zeros_like(carry_scr)

    z = sm_ref[...] + gain(G_FBIAS)
    logf = jnp.minimum(z, 0.0) - jnp.log(1.0 + jnp.exp(-jnp.abs(z)))
    r_id = lax.broadcasted_iota(jnp.int32, (tm, tm), 0)
    c_id = lax.broadcasted_iota(jnp.int32, (tm, tm), 1)
    tri = jnp.where(r_id >= c_id, 1.0, 0.0).astype(BF16)
    hi, mid, lo = _split3(logf)
    cum = _dot(tri, hi) + (_dot(tri, mid) + _dot(tri, lo)) + carry_scr[0:1, :]
    carry_scr[0:1, :] = cum[tm - 1:tm, :]

    for h in range(FOX_HEADS):
        c = jnp.broadcast_to(cum[:, h:h + 1], (tm, LANES)) * LOG2E
        c_hi, c_mid, c_lo = (v.astype(F32) for v in _split3(c))
        ones = jnp.where(lane < 6, 1.0, 0.0)
        q_aug = jnp.where(lane == 0, c_hi, jnp.where(lane == 1, c_mid, jnp.where(lane == 2, c_lo, ones)))
        k_aug = jnp.where(lane == 3, -c_hi, jnp.where(lane == 4, -c_mid, jnp.where(lane == 5, -c_lo, ones)))
        qct_ref[h, 0, LANES:FOX_AUG, :] = tr(q_aug)
        kf_ref[:, h * FOX_AUG + LANES:(h + 1) * FOX_AUG] = k_aug.astype(BF16)


def _prep(p_qkv, small, rope_tab, gains, *, seq):
    tm = ATT_TILE
    T = p_qkv.shape[0]
    nt = T // tm
    G = NSA_GROUPS

    def rows(w):
        return pl.BlockSpec((tm, w), lambda i: (i, 0))

    def tiles_t(n, d=LANES):
        return (jax.ShapeDtypeStruct((n, nt, d, tm), BF16), pl.BlockSpec((n, 1, d, tm), lambda i: (0, i, 0, 0)))

    def nat(w):
        return (jax.ShapeDtypeStruct((T, w), BF16), rows(w))

    def grp():
        return (jax.ShapeDtypeStruct((2, G, T, LANES), BF16),
                pl.BlockSpec((2, G, tm, LANES), lambda i: (0, 0, i, 0)))

    outs = [
        tiles_t(DA_HEADS), nat(DA_HEADS * LANES), tiles_t(DA_HEADS, VT_ROWS),
        tiles_t(NSA_HEADS), nat(G * LANES), nat(G * LANES),
        tiles_t(G, VT_ROWS), tiles_t(G, VT_ROWS),
        tiles_t(FOX_HEADS, FOX_AUG), nat(FOX_HEADS * FOX_AUG), tiles_t(FOX_HEADS, VT_ROWS),
        grp(),
    ]
    return pl.pallas_call(
        functools.partial(_prep_kernel, tiles_per_seq=seq // tm),
        grid=(nt,),
        in_specs=[
            rows(QKV_COLS),
            rows(LANES),
            pl.BlockSpec((4, tm, LANES), lambda i: (0, i, 0)),
            pl.BlockSpec(gains.shape, lambda i: (0, 0)),
        ],
        out_specs=tuple(o[1] for o in outs),
        out_shape=tuple(o[0] for o in outs),
        scratch_shapes=[pltpu.VMEM((8, LANES), F32)],
        compiler_params=_cparams(("arbitrary",)),
        name="prep",
    )(p_qkv, small, rope_tab, gains)


def _compress_kernel(x_ref, pos_ref, w1_ref, w2_ref, gain_ref, kc_ref, vct_ref, *, batch):
    def mlp(kv):
        blk = (x_ref[kv].astype(F32) + pos_ref[kv]).astype(BF16)
        hid = jax.nn.gelu(_dot(blk, w1_ref[kv].astype(BF16)))
        return _dot(hid.astype(BF16), w2_ref[kv].astype(BF16))

    kc = _rms_rows(mlp(0), gain_ref[...])
    vc = mlp(1)
    nb = kc.shape[0] // batch
    pad = jnp.zeros((LANES - nb, LANES), F32)
    for b in range(batch):
        kc_ref[b] = jnp.concatenate([kc[b * nb:(b + 1) * nb], pad], axis=0).astype(BF16)
        vct_ref[b] = jnp.concatenate([vc[b * nb:(b + 1) * nb], pad], axis=0).T.astype(BF16)


def _compress(xkv, pos, w1, w2, gain, layer, *, batch):
    _, G, R, K = xkv.shape
    d = w1.shape[-1]
    out = jax.ShapeDtypeStruct((G, batch, LANES, LANES), BF16)
    ospec = pl.BlockSpec((None, batch, LANES, LANES), lambda g: (g, 0, 0, 0))
    return pl.pallas_call(
        functools.partial(_compress_kernel, batch=batch),
        grid=(G,),
        in_specs=[
            pl.BlockSpec((2, None, R, K), lambda g: (0, g, 0, 0)),
            pl.BlockSpec((None, 2, 1, K), lambda g: (layer, 0, 0, 0)),
            pl.BlockSpec((None, 2, K, d), lambda g: (layer, 0, 0, 0)),
            pl.BlockSpec((None, 2, d, d), lambda g: (layer, 0, 0, 0)),
            pl.BlockSpec((1, d), lambda g: (0, 0)),
        ],
        out_specs=(ospec, ospec),
        out_shape=(out, out),
        compiler_params=_cparams(("arbitrary",)),
        name="compress",
    )(xkv, pos, w1, w2, gain)


def _online_t(s_ts, mask, carry, v_ts):
    stats = []
    for s_t, (m, _) in zip(s_ts, carry):
        if mask is not None:
            s_t = jnp.where(mask, s_t, NEG_INF)
        m_new = jnp.maximum(m, jnp.max(s_t, axis=0, keepdims=True))
        stats.append((m_new, jnp.exp2(m - m_new), jnp.exp2(s_t - m_new).astype(BF16)))
    return tuple((m_new, alpha * acc_t + _dot(v_t, p))
                 for (m_new, alpha, p), (_, acc_t), v_t in zip(stats, carry, v_ts))


def _attend(scores, values, loop_mask, last_mask, kt0, kt_last, n):
    t = ATT_TILE
    init = tuple((jnp.full((1, t), NEG_INF, F32), jnp.zeros((VT_ROWS, t), F32)) for _ in range(n))

    def body(kt, carry):
        mask = None if loop_mask is None else loop_mask(kt)
        return _online_t(scores(kt), mask, carry, values(kt))

    carry = lax.fori_loop(kt0, kt_last, body, init)
    carry = _online_t(scores(kt_last), last_mask, carry, values(kt_last))
    return [acc[0:LANES] / acc[LANES:LANES + 1] for (_, acc) in carry]


def _causal_mask_t(t):
    return lax.broadcasted_iota(jnp.int32, (t, t), 0) <= lax.broadcasted_iota(jnp.int32, (t, t), 1)


def _diff_kernel(qt_ref, k_ref, vt_ref, lam_ref, gains_ref, o_ref, *, lam_init):
    i = pl.program_id(2)
    t = ATT_TILE
    nh = qt_ref.shape[0]
    drow = lax.broadcasted_iota(jnp.int32, (LANES, t), 0)
    qts = []
    for h in range(nh):
        q = qt_ref[h, 0]
        qts.append(jnp.where(drow < DA_QK_DIM, q, jnp.zeros_like(q)))
        qts.append(jnp.where(drow < DA_QK_DIM, jnp.zeros_like(q), q))

    def scores(kt):
        off = pl.multiple_of(kt * t, t)
        ks = [k_ref[pl.ds(off, t), h * LANES:(h + 1) * LANES] for h in range(nh)]
        return tuple(_dot(ks[c // 2], qts[c]) for c in range(2 * nh))

    def values(kt):
        return [vt_ref[c // 2, kt] for c in range(2 * nh)]

    outs = _attend(scores, values, None, _causal_mask_t(t), 0, i, 2 * nh)

    lp = lam_ref[...]
    lam = (jnp.exp(jnp.sum(lp[0:1] * lp[1:2], axis=-1, keepdims=True))
           - jnp.exp(jnp.sum(lp[2:3] * lp[3:4], axis=-1, keepdims=True)) + lam_init)
    for h in range(nh):
        o = (outs[2 * h] - lam * outs[2 * h + 1]).T
        o = _rms_rows(o, gains_ref[G_DAOUT:G_DAOUT + 1, :]) * (1.0 - lam_init)
        o_ref[:, h * LANES:(h + 1) * LANES] = o.astype(BF16)


def _diff_attention(qat, ka, vat, da_lambda, gains, *, batch, seq, lam_init, heads_per_step=2):
    t = ATT_TILE
    nq = seq // t
    T = batch * seq
    hp = heads_per_step
    return pl.pallas_call(
        functools.partial(_diff_kernel, lam_init=lam_init),
        grid=(batch, DA_HEADS // hp, nq),
        in_specs=[
            pl.BlockSpec((hp, 1, LANES, t), lambda b, p, i: (p, b * nq + i, 0, 0)),
            pl.BlockSpec((seq, hp * LANES), lambda b, p, i: (b, p)),
            pl.BlockSpec((hp, nq, VT_ROWS, t), lambda b, p, i: (p, b, 0, 0)),
            pl.BlockSpec(da_lambda.shape, lambda b, p, i: (0, 0)),
            pl.BlockSpec(gains.shape, lambda b, p, i: (0, 0)),
        ],
        out_specs=pl.BlockSpec((t, hp * LANES), lambda b, p, i: (b * nq + i, p)),
        out_shape=jax.ShapeDtypeStruct((T, DA_HEADS * LANES), BF16),
        compiler_params=_cparams(("parallel", "parallel", "arbitrary")),
        name="diff_attn",
    )(qat, ka, vat, da_lambda, gains)


def _fox_kernel(qt_ref, k_ref, vt_ref, o_ref):
    i = pl.program_id(1)
    t = ATT_TILE
    nh = qt_ref.shape[0]
    qts = [qt_ref[h, 0] for h in range(nh)]

    def scores(kt):
        off = pl.multiple_of(kt * t, t)
        return tuple(_dot(k_ref[pl.ds(off, t), h * FOX_AUG:(h + 1) * FOX_AUG], qts[h]) for h in range(nh))

    def values(kt):
        return [vt_ref[h, kt] for h in range(nh)]

    outs = _attend(scores, values, None, _causal_mask_t(t), 0, i, nh)
    for h in range(nh):
        o_ref[:, h * LANES:(h + 1) * LANES] = outs[h].T.astype(BF16)


def _fox_attention(qct, kf, vct, *, batch, seq):
    t = ATT_TILE
    nq = seq // t
    T = batch * seq
    nh = FOX_HEADS
    return pl.pallas_call(
        _fox_kernel,
        grid=(batch, nq),
        in_specs=[
            pl.BlockSpec((nh, 1, FOX_AUG, t), lambda b, i: (0, b * nq + i, 0, 0)),
            pl.BlockSpec((seq, nh * FOX_AUG), lambda b, i: (b, 0)),
            pl.BlockSpec((nh, nq, VT_ROWS, t), lambda b, i: (0, b, 0, 0)),
        ],
        out_specs=pl.BlockSpec((t, nh * LANES), lambda b, i: (b * nq + i, 0)),
        out_shape=jax.ShapeDtypeStruct((T, nh * LANES), BF16),
        compiler_params=_cparams(("parallel", "arbitrary")),
        name="fox_attn",
    )(qct, kf, vct)


def _nsa_kernel(qt_ref, ks_ref, vst_ref, kw_ref, vwt_ref, kc_ref, vct_ref, gt_ref, o_ref, *, n_sel):
    i = pl.program_id(2)
    t = ATT_TILE
    lo = i * t
    nh = qt_ref.shape[0]
    qts = [qt_ref[h, 0] for h in range(nh)]
    tq_row = lo + lax.broadcasted_iota(jnp.int32, (1, t), 1)

    kc = kc_ref[...]
    vct = vct_ref[...]
    nidx = lax.broadcasted_iota(jnp.int32, (LANES, t), 0)
    cmask = nidx * NSA_BLOCK + (NSA_BLOCK - 1) <= tq_row
    importance = jnp.zeros((LANES, t), F32)
    o_cmp = []
    for h in range(nh):
        s = jnp.where(cmask, _dot(kc, qts[h]), NEG_INF)
        e = jnp.where(cmask, jnp.exp2(s - jnp.max(s, axis=0, keepdims=True)), 0.0)
        p = e / jnp.maximum(jnp.sum(e, axis=0, keepdims=True), 1e-30)
        importance = importance + p
        o_cmp.append(_dot(vct, p.astype(BF16)))

    cur = tq_row >> BLOCK_SHIFT
    forced = (nidx == 0) | (nidx == cur) | (nidx == cur - 1)
    score = jnp.where(forced, NSA_FORCED_SCORE, jnp.where(nidx <= cur, importance, -1.0))
    nidx_f = nidx.astype(F32)
    sel = jnp.zeros((LANES, t), F32)
    for _ in range(n_sel):
        best = jnp.max(score, axis=0, keepdims=True)
        first = jnp.min(jnp.where(score == best, nidx_f, float(LANES)), axis=0, keepdims=True)
        hit = nidx_f == first
        sel = jnp.where(hit, 1.0, sel)
        score = jnp.where(hit, -3e38, score)
    sel_b = sel.astype(BF16)

    e_key = lax.broadcasted_iota(jnp.int32, (t, LANES), 0) >> BLOCK_SHIFT
    e_blk = lax.broadcasted_iota(jnp.int32, (t, LANES), 1)
    krow = lax.broadcasted_iota(jnp.int32, (t, t), 0)
    qcol = lax.broadcasted_iota(jnp.int32, (t, t), 1)
    causal = krow <= qcol

    def scores_of(k_ref):
        def scores(kt):
            k = k_ref[pl.ds(pl.multiple_of(kt * t, t), t), :]
            return tuple(_dot(k, qts[h]) for h in range(nh))
        return scores

    def selected(kt):
        expand = jnp.where(e_blk == e_key + kt * (t // NSA_BLOCK), 1.0, 0.0).astype(BF16)
        return _dot(expand, sel_b) > 0.5

    def in_window(kt):
        return krow + kt * t > (qcol + lo) - NSA_WINDOW

    o_slc = _attend(scores_of(ks_ref), lambda kt: [vst_ref[kt]] * nh, selected,
                    selected(i) & causal, 0, i, nh)
    o_win = _attend(scores_of(kw_ref), lambda kt: [vwt_ref[kt]] * nh, in_window,
                    causal, jnp.maximum(i - NSA_WINDOW // t, 0), i, nh)

    gt = jax.nn.sigmoid(gt_ref[...])
    for h in range(nh):
        o = (gt[3 * h:3 * h + 1, :] * o_cmp[h] + gt[3 * h + 1:3 * h + 2, :] * o_slc[h]
             + gt[3 * h + 2:3 * h + 3, :] * o_win[h])
        o_ref[:, h * LANES:(h + 1) * LANES] = o.T.astype(BF16)


def _nsa_attention(qbt, ks, kw, vst, vwt, kc, vct, small_t, *, batch, seq):
    t = ATT_TILE
    nq = seq // t
    T = batch * seq
    n_sel = min(NSA_TOP_N, seq // NSA_BLOCK)
    hw = NSA_HPG * LANES
    k_spec = pl.BlockSpec((seq, LANES), lambda b, g, i: (b, g))
    vt_spec = pl.BlockSpec((None, nq, VT_ROWS, t), lambda b, g, i: (g, b, 0, 0))
    c_spec = pl.BlockSpec((None, None, LANES, LANES), lambda b, g, i: (g, b, 0, 0))
    return pl.pallas_call(
        functools.partial(_nsa_kernel, n_sel=n_sel),
        grid=(batch, NSA_GROUPS, nq),
        in_specs=[
            pl.BlockSpec((NSA_HPG, 1, LANES, t), lambda b, g, i: (g, b * nq + i, 0, 0)),
            k_spec, vt_spec, k_spec, vt_spec, c_spec, c_spec,
            pl.BlockSpec((LANES, t), lambda b, g, i: (1 + g, b * nq + i)),
        ],
        out_specs=pl.BlockSpec((t, hw), lambda b, g, i: (b * nq + i, g)),
        out_shape=jax.ShapeDtypeStruct((T, NSA_HEADS * LANES), BF16),
        compiler_params=_cparams(("parallel", "parallel", "arbitrary")),
        name="nsa_attn",
    )(qbt, ks, vst, kw, vwt, kc, vct, small_t)


def _merge_kernel(x_ref, oa_ref, ob_ref, oc_ref, g0_ref, g1_ref, g2_ref,
                  wa_ref, wb_ref, wc_ref, wo_ref, o_ref):
    y = (g0_ref[...].astype(F32) * _dot(oa_ref[...], wa_ref[...])
         + g1_ref[...].astype(F32) * _dot(ob_ref[...], wb_ref[...])
         + g2_ref[...].astype(F32) * _dot(oc_ref[...], wc_ref[...]))
    o_ref[...] = x_ref[...] + _dot(y.astype(BF16), wo_ref[...])


def _merge(x2d, oa, ob, oc, p_gm, wa, wb, wc, wo, layer, *, tm=256):
    T, D = x2d.shape

    def rows(w, c=0):
        return pl.BlockSpec((tm, w), lambda i: (i, c))

    def full(w):
        return pl.BlockSpec((None,) + w.shape[1:], lambda i: (layer, 0, 0))

    return pl.pallas_call(
        _merge_kernel,
        grid=(T // tm,),
        in_specs=[rows(D), rows(oa.shape[1]), rows(ob.shape[1]), rows(oc.shape[1]),
                  rows(D, 0), rows(D, 1), rows(D, 2), full(wa), full(wb), full(wc), full(wo)],
        out_specs=rows(D),
        out_shape=jax.ShapeDtypeStruct((T, D), F32),
        compiler_params=_cparams(("parallel",)),
        name="merge",
    )(x2d, oa, ob, oc, p_gm, p_gm, p_gm, wa, wb, wc, wo)


def _rope_tables(positions, width, rot_dim):
    inv_freq = ROPE_THETA ** (-jnp.arange(0, rot_dim, 2, dtype=F32) / rot_dim)
    ang = positions.astype(F32).reshape(-1, 1) * inv_freq
    cos, sin = jnp.cos(ang), jnp.sin(ang)
    T = ang.shape[0]
    rest = width - rot_dim
    c = jnp.concatenate([cos, cos, jnp.ones((T, rest), F32)], axis=1)
    s = jnp.concatenate([-sin, sin, jnp.zeros((T, rest), F32)], axis=1)
    rep = LANES // width
    return [jnp.tile(t, (1, rep)) for t in (c, s)]


def _pad_lanes(v):
    return jnp.pad(v.astype(F32), (0, LANES - v.shape[0]))


def kernel(x, positions, ffn1_norm, ffn1_w_gate, ffn1_w_up, ffn1_w_down, mix_norm, w_in, da_q_norm, da_k_norm, da_lambda, da_out_norm, nsa_q_norm, nsa_k_norm, nsa_cmp_pos, nsa_cmp_w1, nsa_cmp_w2, fox_q_norm, fox_k_norm, fox_f_bias, w_branch_a, w_branch_b, w_branch_c, w_out, ffn2_norm, ffn2_w_gate, ffn2_w_up, ffn2_w_down):
    B, S, D = x.shape
    depth = w_in.shape[0]
    T = B * S
    G, L, d = NSA_GROUPS, NSA_BLOCK, HEAD_DIM
    nb = S // L
    assert S % ATT_TILE == 0 and nb <= LANES

    rope_tab = jnp.stack(_rope_tables(positions, DA_QK_DIM, DA_QK_DIM // ROPE_FRACTION)
                         + _rope_tables(positions, HEAD_DIM, HEAD_DIM // ROPE_FRACTION))

    widths = [512, 512, 512, 1024, 1536, 24, 512, 512, 512, 4, 3 * D]
    offs = [0]
    for w in widths:
        offs.append(offs[-1] + w)
    (o_aq, o_ak, o_av, o_bq, o_bkv, o_bg, o_cq, o_ck, o_cv, o_cf, o_gm, o_end) = offs

    w_qkv, w_gm = _repack(w_in, [[(o_aq, o_bg), (o_cq, o_cf)], [(o_gm, o_end)]])
    zpad = lambda n: jnp.zeros((depth, D, n), F32)
    w_small = jnp.concatenate([
        w_in[:, :, o_cf:o_gm], zpad(LANES - FOX_HEADS),
        w_in[:, :, o_bg:o_bg + 12], zpad(LANES - 12),
        w_in[:, :, o_bg + 12:o_bg + 24], zpad(LANES - 12)], axis=2)
    w_small_hi = w_small.astype(BF16)
    w_small_lo = (w_small - w_small_hi.astype(F32)).astype(BF16)
    ffn_w = [(_to_bf16(wg, rows=256), _to_bf16(wu, rows=256), _to_bf16(wd, rows=512))
             for wg, wu, wd in ((ffn1_w_gate, ffn1_w_up, ffn1_w_down), (ffn2_w_gate, ffn2_w_up, ffn2_w_down))]
    ffn_g = [ffn1_norm.reshape(depth, 1, D), ffn2_norm.reshape(depth, 1, D)]
    g_mix = mix_norm.reshape(depth, 1, D)
    w_a, w_b, w_c, w_o = (_to_bf16(w, rows=512) for w in (w_branch_a, w_branch_b, w_branch_c, w_out))
    cmp_pos = nsa_cmp_pos.reshape(depth, 2, 1, L * d)

    xc = x.reshape(T, D)
    for l in range(depth):
        lam_init = 0.8 - 0.6 * math.exp(-0.3 * l)
        gains = jnp.stack([
            jnp.tile(da_q_norm[l], 2), jnp.tile(da_k_norm[l], 2), nsa_q_norm[l],
            nsa_k_norm[l, 0], nsa_k_norm[l, 1], nsa_k_norm[l, 2], fox_q_norm[l], fox_k_norm[l],
            _pad_lanes(fox_f_bias[l]), da_out_norm[l]] + [jnp.zeros((LANES,), F32)] * 6).astype(F32)

        xc = _ffn(xc, ffn_g[0], *ffn_w[0], l)

        p_qkv = _proj(xc, g_mix, w_qkv, l, act=False, name="proj_qkv")
        p_gm = _proj(xc, g_mix, w_gm, l, act=True, name="proj_gate")
        small, small_t = _proj_small(xc, g_mix, w_small_hi, w_small_lo, l)

        (qat, ka, vat, qbt, ks, kw, vst, vwt, qct, kf, vct, kv_cmp) = _prep(
            p_qkv, small, rope_tab, gains, seq=S)

        kc, vcmp_t = _compress(kv_cmp.reshape(2, G, B * nb, L * d), cmp_pos, nsa_cmp_w1, nsa_cmp_w2,
                               nsa_k_norm[l, 0].reshape(1, d), l, batch=B)

        o_a = _diff_attention(qat, ka, vat, da_lambda[l], gains, batch=B, seq=S, lam_init=lam_init)
        o_b = _nsa_attention(qbt, ks, kw, vst, vwt, kc, vcmp_t, small_t, batch=B, seq=S)
        o_c = _fox_attention(qct, kf, vct, batch=B, seq=S)

        xc = _merge(xc, o_a, o_b, o_c, p_gm, w_a, w_b, w_c, w_o, l)

        xc = _ffn(xc, ffn_g[1], *ffn_w[1], l)
    return xc.reshape(B, S, D)
```

```python
import functools
import math

import jax
import jax.numpy as jnp
from jax import lax
from jax.experimental import pallas as pl
from jax.experimental.pallas import tpu as pltpu

F32 = jnp.float32
BF16 = jnp.bfloat16

HEAD_DIM = 128
LANES = 128
ROPE_THETA = 500000.0
ROPE_FRACTION = 4
EPS = 1e-6
NEG_INF = -1e30
LOG2E = math.log2(math.e)

DA_HEADS = 4
DA_QK_DIM = 64
NSA_HEADS = 8
NSA_GROUPS = 2
NSA_HPG = NSA_HEADS // NSA_GROUPS
NSA_BLOCK = 64
BLOCK_SHIFT = 6
NSA_TOP_N = 8
NSA_WINDOW = 512
NSA_FORCED_SCORE = 1e4
FOX_HEADS = 4

ATT_TILE = 256
FOX_AUG = 2 * LANES
VT_ROWS = LANES + 16

BLK_AQ, BLK_AK, BLK_AV, BLK_BQ, BLK_BKV, BLK_CQ, BLK_CK, BLK_CV = 0, 4, 8, 12, 20, 32, 36, 40
QKV_COLS = 44 * LANES
SMALL_COLS = 3 * LANES

G_DAQ, G_DAK, G_NQ, G_NK0, G_NK1, G_NK2, G_FQ, G_FK, G_FBIAS, G_DAOUT = range(10)

VMEM_LIMIT = 56 * 1024 * 1024


def _cparams(sem):
    return pltpu.CompilerParams(dimension_semantics=sem, vmem_limit_bytes=VMEM_LIMIT)


def _dot(a, b):
    return jnp.dot(a, b, preferred_element_type=F32)


def _dot_t(a, b):
    return lax.dot_general(a, b, (((1,), (1,)), ((), ())), preferred_element_type=F32)


def _rms_rows(xf, gain):
    return xf * lax.rsqrt(jnp.mean(xf * xf, axis=-1, keepdims=True) + EPS) * gain


def _ffn_kernel(x_ref, g_ref, wg_ref, wu_ref, wd_ref, o_ref, h_scr):
    @pl.when(pl.program_id(1) == 0)
    def _():
        xf = x_ref[...]
        h_scr[...] = _rms_rows(xf, g_ref[...]).astype(BF16)
        o_ref[...] = xf

    h = h_scr[...]
    g = _dot(h, wg_ref[...])
    u = _dot(h, wu_ref[...])
    a = (g * jax.nn.sigmoid(g)) * (0.5 * u)
    o_ref[...] += _dot(a.astype(BF16), wd_ref[...])


def _ffn(x2d, gain, wg, wu, wd, layer, *, tm=1024, tf=512):
    T, D = x2d.shape
    F = wg.shape[2]
    tm = min(tm, T)
    return pl.pallas_call(
        _ffn_kernel,
        grid=(T // tm, F // tf),
        in_specs=[
            pl.BlockSpec((tm, D), lambda i, j: (i, 0)),
            pl.BlockSpec((None, 1, D), lambda i, j: (layer, 0, 0)),
            pl.BlockSpec((None, D, tf), lambda i, j: (layer, 0, j)),
            pl.BlockSpec((None, D, tf), lambda i, j: (layer, 0, j)),
            pl.BlockSpec((None, tf, D), lambda i, j: (layer, j, 0)),
        ],
        out_specs=pl.BlockSpec((tm, D), lambda i, j: (i, 0)),
        out_shape=jax.ShapeDtypeStruct((T, D), F32),
        scratch_shapes=[pltpu.VMEM((tm, D), BF16)],
        compiler_params=_cparams(("parallel", "arbitrary")),
        name="ffn",
    )(x2d, gain, wg, wu, wd)


def _cast_kernel(x_ref, o_ref):
    o_ref[...] = x_ref[...].astype(BF16)


def _to_bf16(w, *, rows):
    depth, R, C = w.shape
    spec = pl.BlockSpec((None, rows, C), lambda l, i: (l, i, 0))
    return pl.pallas_call(
        _cast_kernel,
        grid=(depth, R // rows),
        in_specs=[spec],
        out_specs=spec,
        out_shape=jax.ShapeDtypeStruct(w.shape, BF16),
        compiler_params=_cparams(("parallel", "parallel")),
        name="cast_bf16",
    )(w)


def _proj_kernel(x_ref, g_ref, w_ref, o_ref, h_scr, *, act):
    @pl.when(pl.program_id(1) == 0)
    def _():
        h_scr[...] = _rms_rows(x_ref[...], g_ref[...]).astype(BF16)

    r = _dot(h_scr[...], w_ref[...])
    if act:
        r = jax.nn.sigmoid(r)
    o_ref[...] = r.astype(o_ref.dtype)


def _proj(x2d, gain, w, layer, *, act, name, tm=1024, tn=512):
    T, D = x2d.shape
    N = w.shape[2]
    tm = min(tm, T)
    return pl.pallas_call(
        functools.partial(_proj_kernel, act=act),
        grid=(T // tm, N // tn),
        in_specs=[
            pl.BlockSpec((tm, D), lambda i, j: (i, 0)),
            pl.BlockSpec((None, 1, D), lambda i, j: (layer, 0, 0)),
            pl.BlockSpec((None, D, tn), lambda i, j: (layer, 0, j)),
        ],
        out_specs=pl.BlockSpec((tm, tn), lambda i, j: (i, j)),
        out_shape=jax.ShapeDtypeStruct((T, N), BF16),
        scratch_shapes=[pltpu.VMEM((tm, D), BF16)],
        compiler_params=_cparams(("parallel", "arbitrary")),
        name=name,
    )(x2d, gain, w)


def _split3(v):
    hi = v.astype(BF16)
    r = v - hi.astype(F32)
    mid = r.astype(BF16)
    lo = (r - mid.astype(F32)).astype(BF16)
    return hi, mid, lo


def _proj_small_kernel(x_ref, g_ref, whi_ref, wlo_ref, o_ref, ot_ref):
    h = _rms_rows(x_ref[...], g_ref[...])
    hi, mid, _ = _split3(h)
    whi = whi_ref[...]
    r = _dot(hi, whi) + (_dot(mid, whi) + _dot(hi, wlo_ref[...]))
    o_ref[...] = r
    for c in range(r.shape[1] // LANES):
        ot_ref[c * LANES:(c + 1) * LANES, :] = r[:, c * LANES:(c + 1) * LANES].T


def _proj_small(x2d, gain, whi, wlo, layer, *, tm=512):
    T, D = x2d.shape
    N = whi.shape[2]
    return pl.pallas_call(
        _proj_small_kernel,
        grid=(T // tm,),
        in_specs=[
            pl.BlockSpec((tm, D), lambda i: (i, 0)),
            pl.BlockSpec((None, 1, D), lambda i: (layer, 0, 0)),
            pl.BlockSpec((None, D, N), lambda i: (layer, 0, 0)),
            pl.BlockSpec((None, D, N), lambda i: (layer, 0, 0)),
        ],
        out_specs=(pl.BlockSpec((tm, N), lambda i: (i, 0)), pl.BlockSpec((N, tm), lambda i: (0, i))),
        out_shape=(jax.ShapeDtypeStruct((T, N), F32), jax.ShapeDtypeStruct((N, T), F32)),
        compiler_params=_cparams(("parallel",)),
        name="proj_small",
    )(x2d, gain, whi, wlo)


def _prep_kernel(p_ref, sm_ref, rope_ref, gains_ref,
                 qat_ref, ka_ref, vat_ref, qbt_ref, ks_ref, kw_ref, vst_ref, vwt_ref,
                 qct_ref, kf_ref, vct_ref, kvc_ref,
                 carry_scr, *, tiles_per_seq):
    i = pl.program_id(0)
    tm = p_ref.shape[0]
    lane = lax.broadcasted_iota(jnp.int32, (tm, LANES), 1)
    src = lax.broadcasted_iota(jnp.int32, (LANES, LANES), 0)
    dst = lax.broadcasted_iota(jnp.int32, (LANES, LANES), 1)

    def const(cond, val=1.0):
        return jnp.where(cond, val, 0.0).astype(BF16)

    def group_mean(width):
        shift = width.bit_length() - 1
        return const((src >> shift) == (dst >> shift), 1.0 / width)

    def swap_halves(width, half):
        d = dst & (width - 1)
        return const(((d < half) & (src == dst + half)) | ((d >= half) & (d < 2 * half) & (src == dst - half)))

    mean128, mean64 = group_mean(HEAD_DIM), group_mean(DA_QK_DIM)
    eye = const(src == dst)

    def dot2(v, mat):
        hi = v.astype(BF16)
        lo = (v - hi.astype(F32)).astype(BF16)
        return _dot(hi, mat) + _dot(lo, mat)

    def raw(c):
        return p_ref[:, c * LANES:(c + 1) * LANES]

    def chunk(c):
        return raw(c).astype(F32)

    def gain(r):
        return gains_ref[r:r + 1, :]


    def tr(y):
        return _dot_t(eye, y.astype(BF16)).astype(BF16)

    def put_vt(ref, n, c):
        ref[n, 0, 0:LANES, :] = tr(raw(c))
        ref[n, 0, LANES:VT_ROWS, :] = jnp.ones((VT_ROWS - LANES, tm), BF16)

    half_a = swap_halves(DA_QK_DIM, DA_QK_DIM // ROPE_FRACTION // 2)
    half_b = swap_halves(HEAD_DIM, HEAD_DIM // ROPE_FRACTION // 2)
    scale_a = DA_QK_DIM ** -0.5 * LOG2E
    scale_b = HEAD_DIM ** -0.5 * LOG2E

    def run_group(specs):
        xs = [chunk(c) for c, *_ in specs]
        ms = [None if nm is None else dot2(x * x, nm[1]) for x, (_, nm, *_) in zip(xs, specs)]
        ys = [x if nm is None else x * lax.rsqrt(m + EPS) * gain(nm[0])
              for x, m, (_, nm, *_) in zip(xs, ms, specs)]
        ps = [None if rp is None else dot2(y, rp[1]) for y, (_, _, rp, *_) in zip(ys, specs)]
        rs = [y if rp is None else y * rope_ref[rp[0]] + p * rope_ref[rp[0] + 1]
              for y, p, (_, _, rp, *_) in zip(ys, ps, specs)]
        rs = [r if sc == 1.0 else r * sc for r, (_, _, _, sc, _, _) in zip(rs, specs)]
        outs = [tr(r) if tp else r.astype(BF16) for r, (_, _, _, _, tp, _) in zip(rs, specs)]
        for o, spec in zip(outs, specs):
            spec[-1](o)

    def nat_sink(ref, lo):
        def sink(o):
            ref[:, lo:lo + LANES] = o
        return sink

    def tile_sink(ref, n, rows=slice(None)):
        def sink(o):
            ref[n, 0, rows, :] = o
        return sink

    def grp_sink(n, g):
        def sink(o):
            kvc_ref[n, g] = o
        return sink

    rope_a, rope_b = (0, half_a), (2, half_b)
    run_group([(BLK_AQ + h, (G_DAQ, mean64), rope_a, scale_a, True, tile_sink(qat_ref, h))
               for h in range(DA_HEADS)]
              + [(BLK_AK + h, (G_DAK, mean64), rope_a, 1.0, False, nat_sink(ka_ref, h * LANES))
                 for h in range(DA_HEADS)])
    run_group([(BLK_BQ + h, (G_NQ, mean128), rope_b, scale_b, True, tile_sink(qbt_ref, h))
               for h in range(NSA_HEADS)])
    run_group([(BLK_BKV + 0 + g, None, rope_b, 1.0, False, grp_sink(0, g)) for g in range(NSA_GROUPS)]
              + [(BLK_BKV + 4 + g, (G_NK1, mean128), rope_b, 1.0, False, nat_sink(ks_ref, g * LANES))
                 for g in range(NSA_GROUPS)]
              + [(BLK_BKV + 8 + g, (G_NK2, mean128), rope_b, 1.0, False, nat_sink(kw_ref, g * LANES))
                 for g in range(NSA_GROUPS)])
    run_group([(BLK_CQ + h, (G_FQ, mean128), None, scale_b, True, tile_sink(qct_ref, h, slice(0, LANES)))
               for h in range(FOX_HEADS)]
              + [(BLK_CK + h, (G_FK, mean128), None, 1.0, False, nat_sink(kf_ref, h * FOX_AUG))
                 for h in range(FOX_HEADS)])
    for g in range(NSA_GROUPS):
        kvc_ref[1, g] = raw(BLK_BKV + 2 + g)
        put_vt(vst_ref, g, BLK_BKV + 6 + g)
        put_vt(vwt_ref, g, BLK_BKV + 10 + g)
    for h in range(DA_HEADS):
        put_vt(vat_ref, h, BLK_AV + h)
    for h in range(FOX_HEADS):
        put_vt(vct_ref, h, BLK_CV + h)

    @pl.when(i % tiles_per_seq == 0)
    def _():
        carry_scr[...] = jnp.zeros_like(carry_scr)

    z = sm_ref[...] + gain(G_FBIAS)
    logf = jnp.minimum(z, 0.0) - jnp.log(1.0 + jnp.exp(-jnp.abs(z)))
    r_id = lax.broadcasted_iota(jnp.int32, (tm, tm), 0)
    c_id = lax.broadcasted_iota(jnp.int32, (tm, tm), 1)
    tri = jnp.where(r_id >= c_id, 1.0, 0.0).astype(BF16)
    hi, mid, lo = _split3(logf)
    cum = _dot(tri, hi) + (_dot(tri, mid) + _dot(tri, lo)) + carry_scr[0:1, :]
    carry_scr[0:1, :] = cum[tm - 1:tm, :]

    for h in range(FOX_HEADS):
        c = jnp.broadcast_to(cum[:, h:h + 1], (tm, LANES)) * LOG2E
        c_hi, c_mid, c_lo = (v.astype(F32) for v in _split3(c))
        ones = jnp.where(lane < 6, 1.0, 0.0)
        q_aug = jnp.where(lane == 0, c_hi, jnp.where(lane == 1, c_mid, jnp.where(lane == 2, c_lo, ones)))
        k_aug = jnp.where(lane == 3, -c_hi, jnp.where(lane == 4, -c_mid, jnp.where(lane == 5, -c_lo, ones)))
        qct_ref[h, 0, LANES:FOX_AUG, :] = tr(q_aug)
        kf_ref[:, h * FOX_AUG + LANES:(h + 1) * FOX_AUG] = k_aug.astype(BF16)


def _prep(p_qkv, small, rope_tab, gains, *, seq):
    tm = ATT_TILE
    T = p_qkv.shape[0]
    nt = T // tm
    G = NSA_GROUPS

    def rows(w):
        return pl.BlockSpec((tm, w), lambda i: (i, 0))

    def tiles_t(n, d=LANES):
        return (jax.ShapeDtypeStruct((n, nt, d, tm), BF16), pl.BlockSpec((n, 1, d, tm), lambda i: (0, i, 0, 0)))

    def nat(w):
        return (jax.ShapeDtypeStruct((T, w), BF16), rows(w))

    def grp():
        return (jax.ShapeDtypeStruct((2, G, T, LANES), BF16),
                pl.BlockSpec((2, G, tm, LANES), lambda i: (0, 0, i, 0)))

    outs = [
        tiles_t(DA_HEADS), nat(DA_HEADS * LANES), tiles_t(DA_HEADS, VT_ROWS),
        tiles_t(NSA_HEADS), nat(G * LANES), nat(G * LANES),
        tiles_t(G, VT_ROWS), tiles_t(G, VT_ROWS),
        tiles_t(FOX_HEADS, FOX_AUG), nat(FOX_HEADS * FOX_AUG), tiles_t(FOX_HEADS, VT_ROWS),
        grp(),
    ]
    return pl.pallas_call(
        functools.partial(_prep_kernel, tiles_per_seq=seq // tm),
        grid=(nt,),
        in_specs=[
            rows(QKV_COLS),
            rows(LANES),
            pl.BlockSpec((4, tm, LANES), lambda i: (0, i, 0)),
            pl.BlockSpec(gains.shape, lambda i: (0, 0)),
        ],
        out_specs=tuple(o[1] for o in outs),
        out_shape=tuple(o[0] for o in outs),
        scratch_shapes=[pltpu.VMEM((8, LANES), F32)],
        compiler_params=_cparams(("arbitrary",)),
        name="prep",
    )(p_qkv, small, rope_tab, gains)


def _compress_kernel(x_ref, pos_ref, w1_ref, w2_ref, gain_ref, kc_ref, vct_ref, *, batch):
    def mlp(kv):
        blk = (x_ref[kv].astype(F32) + pos_ref[kv]).astype(BF16)
        hid = jax.nn.gelu(_dot(blk, w1_ref[kv].astype(BF16)))
        return _dot(hid.astype(BF16), w2_ref[kv].astype(BF16))

    kc = _rms_rows(mlp(0), gain_ref[...])
    vc = mlp(1)
    nb = kc.shape[0] // batch
    pad = jnp.zeros((LANES - nb, LANES), F32)
    for b in range(batch):
        kc_ref[b] = jnp.concatenate([kc[b * nb:(b + 1) * nb], pad], axis=0).astype(BF16)
        vct_ref[b] = jnp.concatenate([vc[b * nb:(b + 1) * nb], pad], axis=0).T.astype(BF16)


def _compress(xkv, pos, w1, w2, gain, layer, *, batch):
    _, G, R, K = xkv.shape
    d = w1.shape[-1]
    out = jax.ShapeDtypeStruct((G, batch, LANES, LANES), BF16)
    ospec = pl.BlockSpec((None, batch, LANES, LANES), lambda g: (g, 0, 0, 0))
    return pl.pallas_call(
        functools.partial(_compress_kernel, batch=batch),
        grid=(G,),
        in_specs=[
            pl.BlockSpec((2, None, R, K), lambda g: (0, g, 0, 0)),
            pl.BlockSpec((None, 2, 1, K), lambda g: (layer, 0, 0, 0)),
            pl.BlockSpec((None, 2, K, d), lambda g: (layer, 0, 0, 0)),
            pl.BlockSpec((None, 2, d, d), lambda g: (layer, 0, 0, 0)),
            pl.BlockSpec((1, d), lambda g: (0, 0)),
        ],
        out_specs=(ospec, ospec),
        out_shape=(out, out),
        compiler_params=_cparams(("arbitrary",)),
        name="compress",
    )(xkv, pos, w1, w2, gain)


def _online_t(s_ts, mask, carry, v_ts):
    masks = mask if isinstance(mask, (list, tuple)) else [mask] * len(s_ts)
    stats = []
    for s_t, (m, _), mk in zip(s_ts, carry, masks):
        if mk is not None:
            s_t = jnp.where(mk, s_t, NEG_INF)
        m_new = jnp.maximum(m, jnp.max(s_t, axis=0, keepdims=True))
        stats.append((m_new, jnp.exp2(m - m_new), jnp.exp2(s_t - m_new).astype(BF16)))
    return tuple((m_new, alpha * acc_t + _dot(v_t, p))
                 for (m_new, alpha, p), (_, acc_t), v_t in zip(stats, carry, v_ts))


def _attend(scores, values, loop_mask, last_mask, kt0, kt_last, n):
    t = ATT_TILE
    init = tuple((jnp.full((1, t), NEG_INF, F32), jnp.zeros((VT_ROWS, t), F32)) for _ in range(n))

    def body(kt, carry):
        mask = None if loop_mask is None else loop_mask(kt)
        return _online_t(scores(kt), mask, carry, values(kt))

    carry = lax.fori_loop(kt0, kt_last, body, init)
    carry = _online_t(scores(kt_last), last_mask, carry, values(kt_last))
    return [acc[0:LANES] / acc[LANES:LANES + 1] for (_, acc) in carry]


def _causal_mask_t(t):
    return lax.broadcasted_iota(jnp.int32, (t, t), 0) <= lax.broadcasted_iota(jnp.int32, (t, t), 1)


def _diff_kernel(qt_ref, k_ref, vt_ref, lam_ref, gains_ref, o_ref, *, lam_init):
    i = pl.program_id(2)
    t = ATT_TILE
    nh = qt_ref.shape[0]
    drow = lax.broadcasted_iota(jnp.int32, (LANES, t), 0)
    qts = []
    for h in range(nh):
        q = qt_ref[h, 0]
        qts.append(jnp.where(drow < DA_QK_DIM, q, jnp.zeros_like(q)))
        qts.append(jnp.where(drow < DA_QK_DIM, jnp.zeros_like(q), q))

    def scores(kt):
        off = pl.multiple_of(kt * t, t)
        ks = [k_ref[pl.ds(off, t), h * LANES:(h + 1) * LANES] for h in range(nh)]
        return tuple(_dot(ks[c // 2], qts[c]) for c in range(2 * nh))

    def values(kt):
        return [vt_ref[c // 2, kt] for c in range(2 * nh)]

    outs = _attend(scores, values, None, _causal_mask_t(t), 0, i, 2 * nh)

    lp = lam_ref[...]
    lam = (jnp.exp(jnp.sum(lp[0:1] * lp[1:2], axis=-1, keepdims=True))
           - jnp.exp(jnp.sum(lp[2:3] * lp[3:4], axis=-1, keepdims=True)) + lam_init)
    for h in range(nh):
        o = (outs[2 * h] - lam * outs[2 * h + 1]).T
        o = _rms_rows(o, gains_ref[G_DAOUT:G_DAOUT + 1, :]) * (1.0 - lam_init)
        o_ref[:, h * LANES:(h + 1) * LANES] = o.astype(BF16)


def _diff_attention(qat, ka, vat, da_lambda, gains, *, batch, seq, lam_init, heads_per_step=4):
    t = ATT_TILE
    nq = seq // t
    T = batch * seq
    hp = heads_per_step
    return pl.pallas_call(
        functools.partial(_diff_kernel, lam_init=lam_init),
        grid=(batch, DA_HEADS // hp, nq),
        in_specs=[
            pl.BlockSpec((hp, 1, LANES, t), lambda b, p, i: (p, b * nq + i, 0, 0)),
            pl.BlockSpec((seq, hp * LANES), lambda b, p, i: (b, p)),
            pl.BlockSpec((hp, nq, VT_ROWS, t), lambda b, p, i: (p, b, 0, 0)),
            pl.BlockSpec(da_lambda.shape, lambda b, p, i: (0, 0)),
            pl.BlockSpec(gains.shape, lambda b, p, i: (0, 0)),
        ],
        out_specs=pl.BlockSpec((t, hp * LANES), lambda b, p, i: (b * nq + i, p)),
        out_shape=jax.ShapeDtypeStruct((T, DA_HEADS * LANES), BF16),
        compiler_params=_cparams(("parallel", "parallel", "arbitrary")),
        name="diff_attn",
    )(qat, ka, vat, da_lambda, gains)


def _fox_kernel(qt_ref, k_ref, vt_ref, o_ref):
    i = pl.program_id(1)
    t = ATT_TILE
    nh, bs = qt_ref.shape[0], qt_ref.shape[1]
    chains = [(b, h) for b in range(bs) for h in range(nh)]
    qts = [qt_ref[h, b, 0] for b, h in chains]

    def scores(kt):
        off = pl.multiple_of(kt * t, t)
        return tuple(_dot(k_ref[b, pl.ds(off, t), h * FOX_AUG:(h + 1) * FOX_AUG], q)
                     for (b, h), q in zip(chains, qts))

    def values(kt):
        return [vt_ref[h, b, kt] for b, h in chains]

    outs = _attend(scores, values, None, _causal_mask_t(t), 0, i, len(chains))
    for (b, h), o in zip(chains, outs):
        o_ref[b, :, h * LANES:(h + 1) * LANES] = o.T.astype(BF16)


def _fox_attention(qct, kf, vct, *, batch, seq):
    t = ATT_TILE
    nq = seq // t
    nh = FOX_HEADS
    bs = 2 if batch % 2 == 0 else 1
    nbp = batch // bs
    out = pl.pallas_call(
        _fox_kernel,
        grid=(nbp, nq),
        in_specs=[
            pl.BlockSpec((nh, None, bs, 1, FOX_AUG, t), lambda p, i: (0, p, 0, i, 0, 0)),
            pl.BlockSpec((None, bs, seq, nh * FOX_AUG), lambda p, i: (p, 0, 0, 0)),
            pl.BlockSpec((nh, None, bs, nq, VT_ROWS, t), lambda p, i: (0, p, 0, 0, 0, 0)),
        ],
        out_specs=pl.BlockSpec((None, bs, t, nh * LANES), lambda p, i: (p, 0, i, 0)),
        out_shape=jax.ShapeDtypeStruct((nbp, bs, seq, nh * LANES), BF16),
        compiler_params=_cparams(("parallel", "arbitrary")),
        name="fox_attn",
    )(qct.reshape(nh, nbp, bs, nq, FOX_AUG, t), kf.reshape(nbp, bs, seq, nh * FOX_AUG),
      vct.reshape(nh, nbp, bs, nq, VT_ROWS, t))
    return out.reshape(batch * seq, nh * LANES)


def _nsa_kernel(qt_ref, ks_ref, vst_ref, kw_ref, vwt_ref, kc_ref, vct_ref, gt_ref, o_ref, *, n_sel):
    i = pl.program_id(1)
    t = ATT_TILE
    lo = i * t
    nh = qt_ref.shape[0]
    hpg = nh // NSA_GROUPS
    qts = [qt_ref[h, 0] for h in range(nh)]
    tq_row = lo + lax.broadcasted_iota(jnp.int32, (1, t), 1)
    nidx = lax.broadcasted_iota(jnp.int32, (LANES, t), 0)
    nidx_f = nidx.astype(F32)
    cmask = nidx * NSA_BLOCK + (NSA_BLOCK - 1) <= tq_row
    cur = tq_row >> BLOCK_SHIFT
    forced = (nidx == 0) | (nidx == cur) | (nidx == cur - 1)

    o_cmp, sel_b = [], []
    for g in range(NSA_GROUPS):
        kc, vct = kc_ref[g], vct_ref[g]
        importance = jnp.zeros((LANES, t), F32)
        for h in range(g * hpg, (g + 1) * hpg):
            s = jnp.where(cmask, _dot(kc, qts[h]), NEG_INF)
            e = jnp.where(cmask, jnp.exp2(s - jnp.max(s, axis=0, keepdims=True)), 0.0)
            p = e / jnp.maximum(jnp.sum(e, axis=0, keepdims=True), 1e-30)
            importance = importance + p
            o_cmp.append(_dot(vct, p.astype(BF16)))

        score = jnp.where(forced, NSA_FORCED_SCORE, jnp.where(nidx <= cur, importance, -1.0))
        sel = jnp.zeros((LANES, t), F32)
        for _ in range(n_sel):
            best = jnp.max(score, axis=0, keepdims=True)
            first = jnp.min(jnp.where(score == best, nidx_f, float(LANES)), axis=0, keepdims=True)
            hit = nidx_f == first
            sel = jnp.where(hit, 1.0, sel)
            score = jnp.where(hit, -3e38, score)
        sel_b.append(sel.astype(BF16))

    e_key = lax.broadcasted_iota(jnp.int32, (t, LANES), 0) >> BLOCK_SHIFT
    e_blk = lax.broadcasted_iota(jnp.int32, (t, LANES), 1)
    krow = lax.broadcasted_iota(jnp.int32, (t, t), 0)
    qcol = lax.broadcasted_iota(jnp.int32, (t, t), 1)
    causal = krow <= qcol

    def scores_of(k_ref):
        def scores(kt):
            off = pl.multiple_of(kt * t, t)
            ks = [k_ref[pl.ds(off, t), g * LANES:(g + 1) * LANES] for g in range(NSA_GROUPS)]
            return tuple(_dot(ks[h // hpg], qts[h]) for h in range(nh))
        return scores

    def values_of(vt_ref):
        return lambda kt: [vt_ref[h // hpg, kt] for h in range(nh)]

    def selected(kt, extra=None):
        expand = jnp.where(e_blk == e_key + kt * (t // NSA_BLOCK), 1.0, 0.0).astype(BF16)
        per_group = [_dot(expand, sb) > 0.5 for sb in sel_b]
        if extra is not None:
            per_group = [m & extra for m in per_group]
        return [per_group[h // hpg] for h in range(nh)]

    def in_window(kt):
        return krow + kt * t > (qcol + lo) - NSA_WINDOW

    o_slc = _attend(scores_of(ks_ref), values_of(vst_ref), selected, selected(i, causal), 0, i, nh)
    o_win = _attend(scores_of(kw_ref), values_of(vwt_ref), in_window,
                    causal, jnp.maximum(i - NSA_WINDOW // t, 0), i, nh)

    for h in range(nh):
        g, r = h // hpg, 3 * (h % hpg)
        gt = jax.nn.sigmoid(gt_ref[(1 + g) * LANES + r:(1 + g) * LANES + r + 3, :])
        o = gt[0:1] * o_cmp[h] + gt[1:2] * o_slc[h] + gt[2:3] * o_win[h]
        o_ref[:, h * LANES:(h + 1) * LANES] = o.T.astype(BF16)


def _nsa_attention(qbt, ks, kw, vst, vwt, kc, vct, small_t, *, batch, seq):
    t = ATT_TILE
    nq = seq // t
    T = batch * seq
    n_sel = min(NSA_TOP_N, seq // NSA_BLOCK)
    G = NSA_GROUPS
    k_spec = pl.BlockSpec((seq, G * LANES), lambda b, i: (b, 0))
    vt_spec = pl.BlockSpec((G, nq, VT_ROWS, t), lambda b, i: (0, b, 0, 0))
    c_spec = pl.BlockSpec((G, None, LANES, LANES), lambda b, i: (0, b, 0, 0))
    return pl.pallas_call(
        functools.partial(_nsa_kernel, n_sel=n_sel),
        grid=(batch, nq),
        in_specs=[
            pl.BlockSpec((NSA_HEADS, 1, LANES, t), lambda b, i: (0, b * nq + i, 0, 0)),
            k_spec, vt_spec, k_spec, vt_spec, c_spec, c_spec,
            pl.BlockSpec((SMALL_COLS, t), lambda b, i: (0, b * nq + i)),
        ],
        out_specs=pl.BlockSpec((t, NSA_HEADS * LANES), lambda b, i: (b * nq + i, 0)),
        out_shape=jax.ShapeDtypeStruct((T, NSA_HEADS * LANES), BF16),
        compiler_params=_cparams(("parallel", "arbitrary")),
        name="nsa_attn",
    )(qbt, ks, vst, kw, vwt, kc, vct, small_t)


def _merge_kernel(x_ref, oa_ref, ob_ref, oc_ref, g0_ref, g1_ref, g2_ref,
                  wa_ref, wb_ref, wc_ref, wo_ref, o_ref):
    y = (g0_ref[...].astype(F32) * _dot(oa_ref[...], wa_ref[...])
         + g1_ref[...].astype(F32) * _dot(ob_ref[...], wb_ref[...])
         + g2_ref[...].astype(F32) * _dot(oc_ref[...], wc_ref[...]))
    o_ref[...] = x_ref[...] + _dot(y.astype(BF16), wo_ref[...])


def _merge(x2d, oa, ob, oc, p_gm, wa, wb, wc, wo, layer, *, tm=256):
    T, D = x2d.shape

    def rows(w, c=0):
        return pl.BlockSpec((tm, w), lambda i: (i, c))

    def full(w):
        return pl.BlockSpec((None,) + w.shape[1:], lambda i: (layer, 0, 0))

    return pl.pallas_call(
        _merge_kernel,
        grid=(T // tm,),
        in_specs=[rows(D), rows(oa.shape[1]), rows(ob.shape[1]), rows(oc.shape[1]),
                  rows(D, 0), rows(D, 1), rows(D, 2), full(wa), full(wb), full(wc), full(wo)],
        out_specs=rows(D),
        out_shape=jax.ShapeDtypeStruct((T, D), F32),
        compiler_params=_cparams(("parallel",)),
        name="merge",
    )(x2d, oa, ob, oc, p_gm, p_gm, p_gm, wa, wb, wc, wo)


def _rope_tables(positions, width, rot_dim):
    inv_freq = ROPE_THETA ** (-jnp.arange(0, rot_dim, 2, dtype=F32) / rot_dim)
    ang = positions.astype(F32).reshape(-1, 1) * inv_freq
    cos, sin = jnp.cos(ang), jnp.sin(ang)
    T = ang.shape[0]
    rest = width - rot_dim
    c = jnp.concatenate([cos, cos, jnp.ones((T, rest), F32)], axis=1)
    s = jnp.concatenate([-sin, sin, jnp.zeros((T, rest), F32)], axis=1)
    rep = LANES // width
    return [jnp.tile(t, (1, rep)) for t in (c, s)]


def _pad_lanes(v):
    return jnp.pad(v.astype(F32), (0, LANES - v.shape[0]))


def kernel(x, positions, ffn1_norm, ffn1_w_gate, ffn1_w_up, ffn1_w_down, mix_norm, w_in, da_q_norm, da_k_norm, da_lambda, da_out_norm, nsa_q_norm, nsa_k_norm, nsa_cmp_pos, nsa_cmp_w1, nsa_cmp_w2, fox_q_norm, fox_k_norm, fox_f_bias, w_branch_a, w_branch_b, w_branch_c, w_out, ffn2_norm, ffn2_w_gate, ffn2_w_up, ffn2_w_down):
    B, S, D = x.shape
    depth = w_in.shape[0]
    T = B * S
    G, L, d = NSA_GROUPS, NSA_BLOCK, HEAD_DIM
    nb = S // L
    assert S % ATT_TILE == 0 and nb <= LANES

    rope_tab = jnp.stack(_rope_tables(positions, DA_QK_DIM, DA_QK_DIM // ROPE_FRACTION)
                         + _rope_tables(positions, HEAD_DIM, HEAD_DIM // ROPE_FRACTION))

    widths = [512, 512, 512, 1024, 1536, 24, 512, 512, 512, 4, 3 * D]
    offs = [0]
    for w in widths:
        offs.append(offs[-1] + w)
    (o_aq, o_ak, o_av, o_bq, o_bkv, o_bg, o_cq, o_ck, o_cv, o_cf, o_gm, o_end) = offs

    w_qkv = jnp.concatenate([w_in[:, :, o_aq:o_bg], w_in[:, :, o_cq:o_cf]], axis=2).astype(BF16)
    w_gm = w_in[:, :, o_gm:o_end].astype(BF16)
    zpad = lambda n: jnp.zeros((depth, D, n), F32)
    w_small = jnp.concatenate([
        w_in[:, :, o_cf:o_gm], zpad(LANES - FOX_HEADS),
        w_in[:, :, o_bg:o_bg + 12], zpad(LANES - 12),
        w_in[:, :, o_bg + 12:o_bg + 24], zpad(LANES - 12)], axis=2)
    w_small_hi = w_small.astype(BF16)
    w_small_lo = (w_small - w_small_hi.astype(F32)).astype(BF16)
    ffn_w = [(_to_bf16(wg, rows=256), _to_bf16(wu, rows=256), _to_bf16(wd, rows=512))
             for wg, wu, wd in ((ffn1_w_gate, ffn1_w_up, ffn1_w_down), (ffn2_w_gate, ffn2_w_up, ffn2_w_down))]
    ffn_g = [ffn1_norm.reshape(depth, 1, D), ffn2_norm.reshape(depth, 1, D)]
    g_mix = mix_norm.reshape(depth, 1, D)
    w_a, w_b, w_c, w_o = (_to_bf16(w, rows=512) for w in (w_branch_a, w_branch_b, w_branch_c, w_out))
    cmp_pos = nsa_cmp_pos.reshape(depth, 2, 1, L * d)

    xc = x.reshape(T, D)
    for l in range(depth):
        lam_init = 0.8 - 0.6 * math.exp(-0.3 * l)
        gains = jnp.stack([
            jnp.tile(da_q_norm[l], 2), jnp.tile(da_k_norm[l], 2), nsa_q_norm[l],
            nsa_k_norm[l, 0], nsa_k_norm[l, 1], nsa_k_norm[l, 2], fox_q_norm[l], fox_k_norm[l],
            _pad_lanes(fox_f_bias[l]), da_out_norm[l]] + [jnp.zeros((LANES,), F32)] * 6).astype(F32)

        xc = _ffn(xc, ffn_g[0], *ffn_w[0], l)

        p_qkv = _proj(xc, g_mix, w_qkv, l, act=False, name="proj_qkv", tn=QKV_COLS // 4)
        p_gm = _proj(xc, g_mix, w_gm, l, act=True, name="proj_gate", tn=D // 2)
        small, small_t = _proj_small(xc, g_mix, w_small_hi, w_small_lo, l)

        (qat, ka, vat, qbt, ks, kw, vst, vwt, qct, kf, vct, kv_cmp) = _prep(
            p_qkv, small, rope_tab, gains, seq=S)

        kc, vcmp_t = _compress(kv_cmp.reshape(2, G, B * nb, L * d), cmp_pos, nsa_cmp_w1, nsa_cmp_w2,
                               nsa_k_norm[l, 0].reshape(1, d), l, batch=B)

        o_a = _diff_attention(qat, ka, vat, da_lambda[l], gains, batch=B, seq=S, lam_init=lam_init)
        o_b = _nsa_attention(qbt, ks, kw, vst, vwt, kc, vcmp_t, small_t, batch=B, seq=S)
        o_c = _fox_attention(qct, kf, vct, batch=B, seq=S)

        xc = _merge(xc, o_a, o_b, o_c, p_gm, w_a, w_b, w_c, w_o, l)

        xc = _ffn(xc, ffn_g[1], *ffn_w[1], l)
    return xc.reshape(B, S, D)
```

```python
import functools
import math

import jax
import jax.numpy as jnp
from jax import lax
from jax.experimental import pallas as pl
from jax.experimental.pallas import tpu as pltpu

F32 = jnp.float32
BF16 = jnp.bfloat16

HEAD_DIM = 128
LANES = 128
ROPE_THETA = 500000.0
ROPE_FRACTION = 4
EPS = 1e-6
NEG_INF = -1e30
LOG2E = math.log2(math.e)

DA_HEADS = 4
DA_QK_DIM = 64
NSA_HEADS = 8
NSA_GROUPS = 2
NSA_HPG = NSA_HEADS // NSA_GROUPS
NSA_BLOCK = 64
BLOCK_SHIFT = 6
NSA_TOP_N = 8
NSA_WINDOW = 512
NSA_FORCED_SCORE = 1e4
FOX_HEADS = 4

ATT_TILE = 256
FOX_AUG = 2 * LANES
VT_ROWS = LANES + 16

BLK_AQ, BLK_AK, BLK_AV, BLK_BQ, BLK_BKV, BLK_CQ, BLK_CK, BLK_CV = 0, 4, 8, 12, 20, 32, 36, 40
QKV_COLS = 44 * LANES
SMALL_COLS = 3 * LANES

G_DAQ, G_DAK, G_NQ, G_NK0, G_NK1, G_NK2, G_FQ, G_FK, G_FBIAS, G_DAOUT = range(10)

VMEM_BYTES_V7X = 64 * 1024 * 1024
VMEM_LIMIT = VMEM_BYTES_V7X - 8 * 1024 * 1024
VMEM_LIMIT_FFN = VMEM_BYTES_V7X - 3 * 1024 * 1024


def _cparams(sem, limit=VMEM_LIMIT):
    return pltpu.CompilerParams(dimension_semantics=sem, vmem_limit_bytes=limit)


def _dot(a, b):
    return jnp.dot(a, b, preferred_element_type=F32)


def _dot_t(a, b):
    return lax.dot_general(a, b, (((1,), (1,)), ((), ())), preferred_element_type=F32)


def _rms_rows(xf, gain):
    return xf * lax.rsqrt(jnp.mean(xf * xf, axis=-1, keepdims=True) + EPS) * gain


def _ffn_kernel(x_ref, g_ref, wg_ref, wu_ref, wd_ref, *rest):
    n_cast = (len(rest) - 2) // 2
    o_ref, h_scr = rest[n_cast], rest[-1]

    @pl.when(pl.program_id(1) == 0)
    def _():
        xf = x_ref[...]
        h_scr[...] = _rms_rows(xf, g_ref[...]).astype(BF16)
        o_ref[...] = xf

    h = h_scr[...]
    g = _dot(h, wg_ref[...])
    u = _dot(h, wu_ref[...])
    a = (g * jax.nn.sigmoid(g)) * (0.5 * u)
    o_ref[...] += _dot(a.astype(BF16), wd_ref[...])

    for src, dst in zip(rest[:n_cast], rest[n_cast + 1:-1]):
        dst[...] = src[...].astype(BF16)


def _ffn(x2d, gain, wg, wu, wd, layer, cast_next=None, *, tm=1024, tf=512):
    T, D = x2d.shape
    F = wg.shape[2]
    tm = min(tm, T)
    ni, nj = T // tm, F // tf
    in_specs = [
        pl.BlockSpec((tm, D), lambda i, j: (i, 0)),
        pl.BlockSpec((None, 1, D), lambda i, j: (layer, 0, 0)),
        pl.BlockSpec((None, D, tf), lambda i, j: (0, 0, j)),
        pl.BlockSpec((None, D, tf), lambda i, j: (0, 0, j)),
        pl.BlockSpec((None, tf, D), lambda i, j: (0, j, 0)),
    ]
    out_specs = [pl.BlockSpec((tm, D), lambda i, j: (i, 0))]
    out_shape = [jax.ShapeDtypeStruct((T, D), F32)]
    args = [x2d, gain, wg, wu, wd]
    if cast_next is not None:
        nl, *nws = cast_next
        for w in nws:
            _, R, C = w.shape
            blk, imap = (((R // ni, C // nj), lambda i, j: (nl, i, j)) if C == F
                         else ((R // nj, C // ni), lambda i, j: (nl, j, i)))
            in_specs.append(pl.BlockSpec((None,) + blk, imap))
            omap = (lambda i, j: (0, i, j)) if C == F else (lambda i, j: (0, j, i))
            out_specs.append(pl.BlockSpec((None,) + blk, omap))
            out_shape.append(jax.ShapeDtypeStruct((1, R, C), BF16))
            args.append(w)
    outs = pl.pallas_call(
        _ffn_kernel,
        grid=(ni, nj),
        in_specs=in_specs,
        out_specs=tuple(out_specs),
        out_shape=tuple(out_shape),
        scratch_shapes=[pltpu.VMEM((tm, D), BF16)],
        compiler_params=_cparams(("parallel", "arbitrary"), VMEM_LIMIT_FFN),
        name="ffn",
    )(*args)
    return outs[0], tuple(outs[1:])


def _cast_kernel(x_ref, o_ref):
    o_ref[...] = x_ref[...].astype(BF16)


def _to_bf16(w, *, rows, layer=None):
    depth, R, C = w.shape
    first, count = (0, depth) if layer is None else (layer, 1)
    return pl.pallas_call(
        _cast_kernel,
        grid=(count, R // rows),
        in_specs=[pl.BlockSpec((None, rows, C), lambda l, i: (first + l, i, 0))],
        out_specs=pl.BlockSpec((None, rows, C), lambda l, i: (l, i, 0)),
        out_shape=jax.ShapeDtypeStruct((count, R, C), BF16),
        compiler_params=_cparams(("parallel", "parallel")),
        name="cast_bf16",
    )(w)


def _proj_kernel(x_ref, g_ref, w_ref, o_ref, h_scr, *, act):
    @pl.when(pl.program_id(1) == 0)
    def _():
        h_scr[...] = _rms_rows(x_ref[...], g_ref[...]).astype(BF16)

    r = _dot(h_scr[...], w_ref[...])
    if act:
        r = jax.nn.sigmoid(r)
    o_ref[...] = r.astype(o_ref.dtype)


def _proj(x2d, gain, w, layer, *, act, name, tm=1024, tn=512):
    T, D = x2d.shape
    N = w.shape[2]
    tm = min(tm, T)
    return pl.pallas_call(
        functools.partial(_proj_kernel, act=act),
        grid=(T // tm, N // tn),
        in_specs=[
            pl.BlockSpec((tm, D), lambda i, j: (i, 0)),
            pl.BlockSpec((None, 1, D), lambda i, j: (layer, 0, 0)),
            pl.BlockSpec((None, D, tn), lambda i, j: (layer, 0, j)),
        ],
        out_specs=pl.BlockSpec((tm, tn), lambda i, j: (i, j)),
        out_shape=jax.ShapeDtypeStruct((T, N), BF16),
        scratch_shapes=[pltpu.VMEM((tm, D), BF16)],
        compiler_params=_cparams(("parallel", "arbitrary")),
        name=name,
    )(x2d, gain, w)


def _split3(v):
    hi = v.astype(BF16)
    r = v - hi.astype(F32)
    mid = r.astype(BF16)
    lo = (r - mid.astype(F32)).astype(BF16)
    return hi, mid, lo


def _proj_small_kernel(x_ref, g_ref, whi_ref, wlo_ref, o_ref, ot_ref):
    h = _rms_rows(x_ref[...], g_ref[...])
    hi, mid, _ = _split3(h)
    whi = whi_ref[...]
    r = _dot(hi, whi) + (_dot(mid, whi) + _dot(hi, wlo_ref[...]))
    o_ref[...] = r
    for c in range(r.shape[1] // LANES):
        ot_ref[c * LANES:(c + 1) * LANES, :] = r[:, c * LANES:(c + 1) * LANES].T


def _proj_small(x2d, gain, whi, wlo, layer, *, tm=512):
    T, D = x2d.shape
    N = whi.shape[2]
    return pl.pallas_call(
        _proj_small_kernel,
        grid=(T // tm,),
        in_specs=[
            pl.BlockSpec((tm, D), lambda i: (i, 0)),
            pl.BlockSpec((None, 1, D), lambda i: (layer, 0, 0)),
            pl.BlockSpec((None, D, N), lambda i: (layer, 0, 0)),
            pl.BlockSpec((None, D, N), lambda i: (layer, 0, 0)),
        ],
        out_specs=(pl.BlockSpec((tm, N), lambda i: (i, 0)), pl.BlockSpec((N, tm), lambda i: (0, i))),
        out_shape=(jax.ShapeDtypeStruct((T, N), F32), jax.ShapeDtypeStruct((N, T), F32)),
        compiler_params=_cparams(("parallel",)),
        name="proj_small",
    )(x2d, gain, whi, wlo)


def _prep_kernel(p_ref, sm_ref, rope_ref, gains_ref,
                 qat_ref, ka_ref, vat_ref, qbt_ref, ks_ref, kw_ref, vst_ref, vwt_ref,
                 qct_ref, kf_ref, vct_ref, kvc_ref,
                 carry_scr, *, tiles_per_seq):
    i = pl.program_id(0)
    tm = p_ref.shape[0]
    lane = lax.broadcasted_iota(jnp.int32, (tm, LANES), 1)
    src = lax.broadcasted_iota(jnp.int32, (LANES, LANES), 0)
    dst = lax.broadcasted_iota(jnp.int32, (LANES, LANES), 1)

    def const(cond, val=1.0):
        return jnp.where(cond, val, 0.0).astype(BF16)

    def group_mean(width):
        shift = width.bit_length() - 1
        return const((src >> shift) == (dst >> shift), 1.0 / width)

    def swap_halves(width, half):
        d = dst & (width - 1)
        return const(((d < half) & (src == dst + half)) | ((d >= half) & (d < 2 * half) & (src == dst - half)))

    mean128, mean64 = group_mean(HEAD_DIM), group_mean(DA_QK_DIM)
    eye = const(src == dst)

    def dot2(v, mat):
        hi = v.astype(BF16)
        lo = (v - hi.astype(F32)).astype(BF16)
        return _dot(hi, mat) + _dot(lo, mat)

    def raw(c):
        return p_ref[:, c * LANES:(c + 1) * LANES]

    def chunk(c):
        return raw(c).astype(F32)

    def gain(r):
        return gains_ref[r:r + 1, :]


    def tr(y):
        return _dot_t(eye, y.astype(BF16)).astype(BF16)

    def put_vt(ref, n, c):
        ref[n, 0, 0:LANES, :] = tr(raw(c))
        ref[n, 0, LANES:VT_ROWS, :] = jnp.ones((VT_ROWS - LANES, tm), BF16)

    half_a = swap_halves(DA_QK_DIM, DA_QK_DIM // ROPE_FRACTION // 2)
    half_b = swap_halves(HEAD_DIM, HEAD_DIM // ROPE_FRACTION // 2)
    scale_a = DA_QK_DIM ** -0.5 * LOG2E
    scale_b = HEAD_DIM ** -0.5 * LOG2E

    def run_group(specs):
        xs = [chunk(c) for c, *_ in specs]
        ms = [None if nm is None else dot2(x * x, nm[1]) for x, (_, nm, *_) in zip(xs, specs)]
        ys = [x if nm is None else x * lax.rsqrt(m + EPS) * gain(nm[0])
              for x, m, (_, nm, *_) in zip(xs, ms, specs)]
        ps = [None if rp is None else dot2(y, rp[1]) for y, (_, _, rp, *_) in zip(ys, specs)]
        rs = [y if rp is None else y * rope_ref[rp[0]] + p * rope_ref[rp[0] + 1]
              for y, p, (_, _, rp, *_) in zip(ys, ps, specs)]
        rs = [r if sc == 1.0 else r * sc for r, (_, _, _, sc, _, _) in zip(rs, specs)]
        outs = [tr(r) if tp else r.astype(BF16) for r, (_, _, _, _, tp, _) in zip(rs, specs)]
        for o, spec in zip(outs, specs):
            spec[-1](o)

    def nat_sink(ref, lo):
        def sink(o):
            ref[:, lo:lo + LANES] = o
        return sink

    def tile_sink(ref, n, rows=slice(None)):
        def sink(o):
            ref[n, 0, rows, :] = o
        return sink

    def grp_sink(n, g):
        def sink(o):
            kvc_ref[n, g] = o
        return sink

    rope_a, rope_b = (0, half_a), (2, half_b)
    run_group([(BLK_AQ + h, (G_DAQ, mean64), rope_a, scale_a, True, tile_sink(qat_ref, h))
               for h in range(DA_HEADS)]
              + [(BLK_AK + h, (G_DAK, mean64), rope_a, 1.0, False, nat_sink(ka_ref, h * LANES))
                 for h in range(DA_HEADS)])
    run_group([(BLK_BQ + h, (G_NQ, mean128), rope_b, scale_b, True, tile_sink(qbt_ref, h))
               for h in range(NSA_HEADS)])
    run_group([(BLK_BKV + 0 + g, None, rope_b, 1.0, False, grp_sink(0, g)) for g in range(NSA_GROUPS)]
              + [(BLK_BKV + 4 + g, (G_NK1, mean128), rope_b, 1.0, False, nat_sink(ks_ref, g * LANES))
                 for g in range(NSA_GROUPS)]
              + [(BLK_BKV + 8 + g, (G_NK2, mean128), rope_b, 1.0, False, nat_sink(kw_ref, g * LANES))
                 for g in range(NSA_GROUPS)])
    run_group([(BLK_CQ + h, (G_FQ, mean128), None, scale_b, True, tile_sink(qct_ref, h, slice(0, LANES)))
               for h in range(FOX_HEADS)]
              + [(BLK_CK + h, (G_FK, mean128), None, 1.0, False, nat_sink(kf_ref, h * FOX_AUG))
                 for h in range(FOX_HEADS)])
    for g in range(NSA_GROUPS):
        kvc_ref[1, g] = raw(BLK_BKV + 2 + g)
        put_vt(vst_ref, g, BLK_BKV + 6 + g)
        put_vt(vwt_ref, g, BLK_BKV + 10 + g)
    for h in range(DA_HEADS):
        put_vt(vat_ref, h, BLK_AV + h)
    for h in range(FOX_HEADS):
        put_vt(vct_ref, h, BLK_CV + h)

    @pl.when(i % tiles_per_seq == 0)
    def _():
        carry_scr[...] = jnp.zeros_like(carry_scr)

    z = sm_ref[...] + gain(G_FBIAS)
    logf = jnp.minimum(z, 0.0) - jnp.log(1.0 + jnp.exp(-jnp.abs(z)))
    r_id = lax.broadcasted_iota(jnp.int32, (tm, tm), 0)
    c_id = lax.broadcasted_iota(jnp.int32, (tm, tm), 1)
    tri = jnp.where(r_id >= c_id, 1.0, 0.0).astype(BF16)
    hi, mid, lo = _split3(logf)
    cum = _dot(tri, hi) + (_dot(tri, mid) + _dot(tri, lo)) + carry_scr[0:1, :]
    carry_scr[0:1, :] = cum[tm - 1:tm, :]

    for h in range(FOX_HEADS):
        c = jnp.broadcast_to(cum[:, h:h + 1], (tm, LANES)) * LOG2E
        c_hi, c_mid, c_lo = (v.astype(F32) for v in _split3(c))
        ones = jnp.where(lane < 6, 1.0, 0.0)
        q_aug = jnp.where(lane == 0, c_hi, jnp.where(lane == 1, c_mid, jnp.where(lane == 2, c_lo, ones)))
        k_aug = jnp.where(lane == 3, -c_hi, jnp.where(lane == 4, -c_mid, jnp.where(lane == 5, -c_lo, ones)))
        qct_ref[h, 0, LANES:FOX_AUG, :] = tr(q_aug)
        kf_ref[:, h * FOX_AUG + LANES:(h + 1) * FOX_AUG] = k_aug.astype(BF16)


def _prep(p_qkv, small, rope_tab, gains, *, seq):
    tm = ATT_TILE
    T = p_qkv.shape[0]
    nt = T // tm
    G = NSA_GROUPS

    def rows(w):
        return pl.BlockSpec((tm, w), lambda i: (i, 0))

    def tiles_t(n, d=LANES):
        return (jax.ShapeDtypeStruct((n, nt, d, tm), BF16), pl.BlockSpec((n, 1, d, tm), lambda i: (0, i, 0, 0)))

    def nat(w):
        return (jax.ShapeDtypeStruct((T, w), BF16), rows(w))

    def grp():
        return (jax.ShapeDtypeStruct((2, G, T, LANES), BF16),
                pl.BlockSpec((2, G, tm, LANES), lambda i: (0, 0, i, 0)))

    outs = [
        tiles_t(DA_HEADS), nat(DA_HEADS * LANES), tiles_t(DA_HEADS, VT_ROWS),
        tiles_t(NSA_HEADS), nat(G * LANES), nat(G * LANES),
        tiles_t(G, VT_ROWS), tiles_t(G, VT_ROWS),
        tiles_t(FOX_HEADS, FOX_AUG), nat(FOX_HEADS * FOX_AUG), tiles_t(FOX_HEADS, VT_ROWS),
        grp(),
    ]
    return pl.pallas_call(
        functools.partial(_prep_kernel, tiles_per_seq=seq // tm),
        grid=(nt,),
        in_specs=[
            rows(QKV_COLS),
            rows(LANES),
            pl.BlockSpec((4, tm, LANES), lambda i: (0, i, 0)),
            pl.BlockSpec(gains.shape, lambda i: (0, 0)),
        ],
        out_specs=tuple(o[1] for o in outs),
        out_shape=tuple(o[0] for o in outs),
        scratch_shapes=[pltpu.VMEM((8, LANES), F32)],
        compiler_params=_cparams(("arbitrary",)),
        name="prep",
    )(p_qkv, small, rope_tab, gains)


def _compress_kernel(x_ref, pos_ref, w1_ref, w2_ref, gain_ref, kc_ref, vct_ref, *, batch):
    def mlp(kv):
        blk = (x_ref[kv].astype(F32) + pos_ref[kv]).astype(BF16)
        hid = jax.nn.gelu(_dot(blk, w1_ref[kv].astype(BF16)))
        return _dot(hid.astype(BF16), w2_ref[kv].astype(BF16))

    kc = _rms_rows(mlp(0), gain_ref[...])
    vc = mlp(1)
    nb = kc.shape[0] // batch
    pad = jnp.zeros((LANES - nb, LANES), F32)
    for b in range(batch):
        kc_ref[b] = jnp.concatenate([kc[b * nb:(b + 1) * nb], pad], axis=0).astype(BF16)
        vct_ref[b] = jnp.concatenate([vc[b * nb:(b + 1) * nb], pad], axis=0).T.astype(BF16)


def _compress(xkv, pos, w1, w2, gain, layer, *, batch):
    _, G, R, K = xkv.shape
    d = w1.shape[-1]
    out = jax.ShapeDtypeStruct((G, batch, LANES, LANES), BF16)
    ospec = pl.BlockSpec((None, batch, LANES, LANES), lambda g: (g, 0, 0, 0))
    return pl.pallas_call(
        functools.partial(_compress_kernel, batch=batch),
        grid=(G,),
        in_specs=[
            pl.BlockSpec((2, None, R, K), lambda g: (0, g, 0, 0)),
            pl.BlockSpec((None, 2, 1, K), lambda g: (layer, 0, 0, 0)),
            pl.BlockSpec((None, 2, K, d), lambda g: (layer, 0, 0, 0)),
            pl.BlockSpec((None, 2, d, d), lambda g: (layer, 0, 0, 0)),
            pl.BlockSpec((1, d), lambda g: (0, 0)),
        ],
        out_specs=(ospec, ospec),
        out_shape=(out, out),
        compiler_params=_cparams(("arbitrary",)),
        name="compress",
    )(xkv, pos, w1, w2, gain)


def _online_t(s_ts, mask, carry, v_ts):
    masks = mask if isinstance(mask, (list, tuple)) else [mask] * len(s_ts)
    stats = []
    for s_t, (m, _), mk in zip(s_ts, carry, masks):
        if mk is not None:
            s_t = jnp.where(mk, s_t, NEG_INF)
        m_new = jnp.maximum(m, jnp.max(s_t, axis=0, keepdims=True))
        stats.append((m_new, jnp.exp2(m - m_new), jnp.exp2(s_t - m_new).astype(BF16)))
    return tuple((m_new, alpha * acc_t + _dot(v_t, p))
                 for (m_new, alpha, p), (_, acc_t), v_t in zip(stats, carry, v_ts))


def _attend(scores, values, loop_mask, last_mask, kt0, kt_last, n):
    t = ATT_TILE
    init = tuple((jnp.full((1, t), NEG_INF, F32), jnp.zeros((VT_ROWS, t), F32)) for _ in range(n))

    def body(kt, carry):
        mask = None if loop_mask is None else loop_mask(kt)
        return _online_t(scores(kt), mask, carry, values(kt))

    carry = lax.fori_loop(kt0, kt_last, body, init)
    carry = _online_t(scores(kt_last), last_mask, carry, values(kt_last))
    return [acc[0:LANES] / acc[LANES:LANES + 1] for (_, acc) in carry]


def _causal_mask_t(t):
    return lax.broadcasted_iota(jnp.int32, (t, t), 0) <= lax.broadcasted_iota(jnp.int32, (t, t), 1)


def _diff_kernel(qt_ref, k_ref, vt_ref, lam_ref, gains_ref, o_ref, *, lam_init):
    i = pl.program_id(2)
    t = ATT_TILE
    nh = qt_ref.shape[0]
    drow = lax.broadcasted_iota(jnp.int32, (LANES, t), 0)
    qts = []
    for h in range(nh):
        q = qt_ref[h, 0]
        qts.append(jnp.where(drow < DA_QK_DIM, q, jnp.zeros_like(q)))
        qts.append(jnp.where(drow < DA_QK_DIM, jnp.zeros_like(q), q))

    def scores(kt):
        off = pl.multiple_of(kt * t, t)
        ks = [k_ref[pl.ds(off, t), h * LANES:(h + 1) * LANES] for h in range(nh)]
        return tuple(_dot(ks[c // 2], qts[c]) for c in range(2 * nh))

    def values(kt):
        return [vt_ref[c // 2, kt] for c in range(2 * nh)]

    outs = _attend(scores, values, None, _causal_mask_t(t), 0, i, 2 * nh)

    lp = lam_ref[...]
    lam = (jnp.exp(jnp.sum(lp[0:1] * lp[1:2], axis=-1, keepdims=True))
           - jnp.exp(jnp.sum(lp[2:3] * lp[3:4], axis=-1, keepdims=True)) + lam_init)
    for h in range(nh):
        o = (outs[2 * h] - lam * outs[2 * h + 1]).T
        o = _rms_rows(o, gains_ref[G_DAOUT:G_DAOUT + 1, :]) * (1.0 - lam_init)
        o_ref[:, h * LANES:(h + 1) * LANES] = o.astype(BF16)


def _diff_attention(qat, ka, vat, da_lambda, gains, *, batch, seq, lam_init, heads_per_step=4):
    t = ATT_TILE
    nq = seq // t
    T = batch * seq
    hp = heads_per_step
    return pl.pallas_call(
        functools.partial(_diff_kernel, lam_init=lam_init),
        grid=(batch, DA_HEADS // hp, nq),
        in_specs=[
            pl.BlockSpec((hp, 1, LANES, t), lambda b, p, i: (p, b * nq + i, 0, 0)),
            pl.BlockSpec((seq, hp * LANES), lambda b, p, i: (b, p)),
            pl.BlockSpec((hp, nq, VT_ROWS, t), lambda b, p, i: (p, b, 0, 0)),
            pl.BlockSpec(da_lambda.shape, lambda b, p, i: (0, 0)),
            pl.BlockSpec(gains.shape, lambda b, p, i: (0, 0)),
        ],
        out_specs=pl.BlockSpec((t, hp * LANES), lambda b, p, i: (b * nq + i, p)),
        out_shape=jax.ShapeDtypeStruct((T, DA_HEADS * LANES), BF16),
        compiler_params=_cparams(("parallel", "parallel", "arbitrary")),
        name="diff_attn",
    )(qat, ka, vat, da_lambda, gains)


def _fox_kernel(qt_ref, k_ref, vt_ref, o_ref):
    i = pl.program_id(1)
    t = ATT_TILE
    nh, bs = qt_ref.shape[0], qt_ref.shape[1]
    chains = [(b, h) for b in range(bs) for h in range(nh)]
    qts = [qt_ref[h, b, 0] for b, h in chains]

    def scores(kt):
        off = pl.multiple_of(kt * t, t)
        return tuple(_dot(k_ref[b, pl.ds(off, t), h * FOX_AUG:(h + 1) * FOX_AUG], q)
                     for (b, h), q in zip(chains, qts))

    def values(kt):
        return [vt_ref[h, b, kt] for b, h in chains]

    outs = _attend(scores, values, None, _causal_mask_t(t), 0, i, len(chains))
    for (b, h), o in zip(chains, outs):
        o_ref[b, :, h * LANES:(h + 1) * LANES] = o.T.astype(BF16)


def _fox_attention(qct, kf, vct, *, batch, seq):
    t = ATT_TILE
    nq = seq // t
    nh = FOX_HEADS
    bs = 2 if batch % 2 == 0 else 1
    nbp = batch // bs
    out = pl.pallas_call(
        _fox_kernel,
        grid=(nbp, nq),
        in_specs=[
            pl.BlockSpec((nh, None, bs, 1, FOX_AUG, t), lambda p, i: (0, p, 0, i, 0, 0)),
            pl.BlockSpec((None, bs, seq, nh * FOX_AUG), lambda p, i: (p, 0, 0, 0)),
            pl.BlockSpec((nh, None, bs, nq, VT_ROWS, t), lambda p, i: (0, p, 0, 0, 0, 0)),
        ],
        out_specs=pl.BlockSpec((None, bs, t, nh * LANES), lambda p, i: (p, 0, i, 0)),
        out_shape=jax.ShapeDtypeStruct((nbp, bs, seq, nh * LANES), BF16),
        compiler_params=_cparams(("parallel", "arbitrary")),
        name="fox_attn",
    )(qct.reshape(nh, nbp, bs, nq, FOX_AUG, t), kf.reshape(nbp, bs, seq, nh * FOX_AUG),
      vct.reshape(nh, nbp, bs, nq, VT_ROWS, t))
    return out.reshape(batch * seq, nh * LANES)


def _nsa_kernel(qt_ref, ks_ref, vst_ref, kw_ref, vwt_ref, kc_ref, vct_ref, gt_ref, o_ref, *, n_sel):
    i = pl.program_id(1)
    t = ATT_TILE
    lo = i * t
    nh = qt_ref.shape[0]
    hpg = nh // NSA_GROUPS
    qts = [qt_ref[h, 0] for h in range(nh)]
    tq_row = lo + lax.broadcasted_iota(jnp.int32, (1, t), 1)
    nidx = lax.broadcasted_iota(jnp.int32, (LANES, t), 0)
    nidx_f = nidx.astype(F32)
    cmask = nidx * NSA_BLOCK + (NSA_BLOCK - 1) <= tq_row
    cur = tq_row >> BLOCK_SHIFT
    forced = (nidx == 0) | (nidx == cur) | (nidx == cur - 1)

    o_cmp, sel_b = [], []
    for g in range(NSA_GROUPS):
        kc, vct = kc_ref[g], vct_ref[g]
        importance = jnp.zeros((LANES, t), F32)
        for h in range(g * hpg, (g + 1) * hpg):
            s = jnp.where(cmask, _dot(kc, qts[h]), NEG_INF)
            e = jnp.where(cmask, jnp.exp2(s - jnp.max(s, axis=0, keepdims=True)), 0.0)
            p = e / jnp.maximum(jnp.sum(e, axis=0, keepdims=True), 1e-30)
            importance = importance + p
            o_cmp.append(_dot(vct, p.astype(BF16)))

        score = jnp.where(forced, NSA_FORCED_SCORE, jnp.where(nidx <= cur, importance, -1.0))
        sel = jnp.zeros((LANES, t), F32)
        for _ in range(n_sel):
            best = jnp.max(score, axis=0, keepdims=True)
            first = jnp.min(jnp.where(score == best, nidx_f, float(LANES)), axis=0, keepdims=True)
            hit = nidx_f == first
            sel = jnp.where(hit, 1.0, sel)
            score = jnp.where(hit, -3e38, score)
        sel_b.append(sel.astype(BF16))

    e_key = lax.broadcasted_iota(jnp.int32, (t, LANES), 0) >> BLOCK_SHIFT
    e_blk = lax.broadcasted_iota(jnp.int32, (t, LANES), 1)
    krow = lax.broadcasted_iota(jnp.int32, (t, t), 0)
    qcol = lax.broadcasted_iota(jnp.int32, (t, t), 1)
    causal = krow <= qcol

    def scores_of(k_ref):
        def scores(kt):
            off = pl.multiple_of(kt * t, t)
            ks = [k_ref[pl.ds(off, t), g * LANES:(g + 1) * LANES] for g in range(NSA_GROUPS)]
            return tuple(_dot(ks[h // hpg], qts[h]) for h in range(nh))
        return scores

    def values_of(vt_ref):
        return lambda kt: [vt_ref[h // hpg, kt] for h in range(nh)]

    def selected(kt, extra=None):
        expand = jnp.where(e_blk == e_key + kt * (t // NSA_BLOCK), 1.0, 0.0).astype(BF16)
        per_group = [_dot(expand, sb) > 0.5 for sb in sel_b]
        if extra is not None:
            per_group = [m & extra for m in per_group]
        return [per_group[h // hpg] for h in range(nh)]

    def in_window(kt):
        return krow + kt * t > (qcol + lo) - NSA_WINDOW

    o_slc = _attend(scores_of(ks_ref), values_of(vst_ref), selected, selected(i, causal), 0, i, nh)
    o_win = _attend(scores_of(kw_ref), values_of(vwt_ref), in_window,
                    causal, jnp.maximum(i - NSA_WINDOW // t, 0), i, nh)

    for h in range(nh):
        g, r = h // hpg, 3 * (h % hpg)
        gt = jax.nn.sigmoid(gt_ref[(1 + g) * LANES + r:(1 + g) * LANES + r + 3, :])
        o = gt[0:1] * o_cmp[h] + gt[1:2] * o_slc[h] + gt[2:3] * o_win[h]
        o_ref[:, h * LANES:(h + 1) * LANES] = o.T.astype(BF16)


def _nsa_attention(qbt, ks, kw, vst, vwt, kc, vct, small_t, *, batch, seq):
    t = ATT_TILE
    nq = seq // t
    T = batch * seq
    n_sel = min(NSA_TOP_N, seq // NSA_BLOCK)
    G = NSA_GROUPS
    k_spec = pl.BlockSpec((seq, G * LANES), lambda b, i: (b, 0))
    vt_spec = pl.BlockSpec((G, nq, VT_ROWS, t), lambda b, i: (0, b, 0, 0))
    c_spec = pl.BlockSpec((G, None, LANES, LANES), lambda b, i: (0, b, 0, 0))
    return pl.pallas_call(
        functools.partial(_nsa_kernel, n_sel=n_sel),
        grid=(batch, nq),
        in_specs=[
            pl.BlockSpec((NSA_HEADS, 1, LANES, t), lambda b, i: (0, b * nq + i, 0, 0)),
            k_spec, vt_spec, k_spec, vt_spec, c_spec, c_spec,
            pl.BlockSpec((SMALL_COLS, t), lambda b, i: (0, b * nq + i)),
        ],
        out_specs=pl.BlockSpec((t, NSA_HEADS * LANES), lambda b, i: (b * nq + i, 0)),
        out_shape=jax.ShapeDtypeStruct((T, NSA_HEADS * LANES), BF16),
        compiler_params=_cparams(("parallel", "arbitrary")),
        name="nsa_attn",
    )(qbt, ks, vst, kw, vwt, kc, vct, small_t)


def _merge_kernel(x_ref, oa_ref, ob_ref, oc_ref, g0_ref, g1_ref, g2_ref,
                  wa_ref, wb_ref, wc_ref, wo_ref, o_ref):
    y = (g0_ref[...].astype(F32) * _dot(oa_ref[...], wa_ref[...])
         + g1_ref[...].astype(F32) * _dot(ob_ref[...], wb_ref[...])
         + g2_ref[...].astype(F32) * _dot(oc_ref[...], wc_ref[...]))
    o_ref[...] = x_ref[...] + _dot(y.astype(BF16), wo_ref[...])


def _merge(x2d, oa, ob, oc, p_gm, wa, wb, wc, wo, layer, *, tm=256):
    T, D = x2d.shape

    def rows(w, c=0):
        return pl.BlockSpec((tm, w), lambda i: (i, c))

    def full(w):
        return pl.BlockSpec((None,) + w.shape[1:], lambda i: (layer, 0, 0))

    return pl.pallas_call(
        _merge_kernel,
        grid=(T // tm,),
        in_specs=[rows(D), rows(oa.shape[1]), rows(ob.shape[1]), rows(oc.shape[1]),
                  rows(D, 0), rows(D, 1), rows(D, 2), full(wa), full(wb), full(wc), full(wo)],
        out_specs=rows(D),
        out_shape=jax.ShapeDtypeStruct((T, D), F32),
        compiler_params=_cparams(("parallel",)),
        name="merge",
    )(x2d, oa, ob, oc, p_gm, p_gm, p_gm, wa, wb, wc, wo)


def _rope_tables(positions, width, rot_dim):
    inv_freq = ROPE_THETA ** (-jnp.arange(0, rot_dim, 2, dtype=F32) / rot_dim)
    ang = positions.astype(F32).reshape(-1, 1) * inv_freq
    cos, sin = jnp.cos(ang), jnp.sin(ang)
    T = ang.shape[0]
    rest = width - rot_dim
    c = jnp.concatenate([cos, cos, jnp.ones((T, rest), F32)], axis=1)
    s = jnp.concatenate([-sin, sin, jnp.zeros((T, rest), F32)], axis=1)
    rep = LANES // width
    return [jnp.tile(t, (1, rep)) for t in (c, s)]


def _pad_lanes(v):
    return jnp.pad(v.astype(F32), (0, LANES - v.shape[0]))


def kernel(x, positions, ffn1_norm, ffn1_w_gate, ffn1_w_up, ffn1_w_down, mix_norm, w_in, da_q_norm, da_k_norm, da_lambda, da_out_norm, nsa_q_norm, nsa_k_norm, nsa_cmp_pos, nsa_cmp_w1, nsa_cmp_w2, fox_q_norm, fox_k_norm, fox_f_bias, w_branch_a, w_branch_b, w_branch_c, w_out, ffn2_norm, ffn2_w_gate, ffn2_w_up, ffn2_w_down):
    B, S, D = x.shape
    depth = w_in.shape[0]
    T = B * S
    G, L, d = NSA_GROUPS, NSA_BLOCK, HEAD_DIM
    nb = S // L
    assert S % ATT_TILE == 0 and nb <= LANES

    rope_tab = jnp.stack(_rope_tables(positions, DA_QK_DIM, DA_QK_DIM // ROPE_FRACTION)
                         + _rope_tables(positions, HEAD_DIM, HEAD_DIM // ROPE_FRACTION))

    widths = [512, 512, 512, 1024, 1536, 24, 512, 512, 512, 4, 3 * D]
    offs = [0]
    for w in widths:
        offs.append(offs[-1] + w)
    (o_aq, o_ak, o_av, o_bq, o_bkv, o_bg, o_cq, o_ck, o_cv, o_cf, o_gm, o_end) = offs

    w_qkv = jnp.concatenate([w_in[:, :, o_aq:o_bg], w_in[:, :, o_cq:o_cf]], axis=2).astype(BF16)
    w_gm = w_in[:, :, o_gm:o_end].astype(BF16)
    zpad = lambda n: jnp.zeros((depth, D, n), F32)
    w_small = jnp.concatenate([
        w_in[:, :, o_cf:o_gm], zpad(LANES - FOX_HEADS),
        w_in[:, :, o_bg:o_bg + 12], zpad(LANES - 12),
        w_in[:, :, o_bg + 12:o_bg + 24], zpad(LANES - 12)], axis=2)
    w_small_hi = w_small.astype(BF16)
    w_small_lo = (w_small - w_small_hi.astype(F32)).astype(BF16)
    ffn_f32 = [(ffn1_w_gate, ffn1_w_up, ffn1_w_down), (ffn2_w_gate, ffn2_w_up, ffn2_w_down)]
    ffn_g = [ffn1_norm.reshape(depth, 1, D), ffn2_norm.reshape(depth, 1, D)]
    ffn_next = [(k // 2, *ffn_f32[k % 2]) for k in range(1, 2 * depth)] + [None]
    ffn_w = tuple(_to_bf16(w, rows=256, layer=0) for w in ffn_f32[0])
    g_mix = mix_norm.reshape(depth, 1, D)
    w_a, w_b, w_c, w_o = (_to_bf16(w, rows=512) for w in (w_branch_a, w_branch_b, w_branch_c, w_out))
    cmp_pos = nsa_cmp_pos.reshape(depth, 2, 1, L * d)

    xc = x.reshape(T, D)
    for l in range(depth):
        lam_init = 0.8 - 0.6 * math.exp(-0.3 * l)
        gains = jnp.stack([
            jnp.tile(da_q_norm[l], 2), jnp.tile(da_k_norm[l], 2), nsa_q_norm[l],
            nsa_k_norm[l, 0], nsa_k_norm[l, 1], nsa_k_norm[l, 2], fox_q_norm[l], fox_k_norm[l],
            _pad_lanes(fox_f_bias[l]), da_out_norm[l]] + [jnp.zeros((LANES,), F32)] * 6).astype(F32)

        xc, ffn_w = _ffn(xc, ffn_g[0], *ffn_w, l, ffn_next[2 * l])

        p_qkv = _proj(xc, g_mix, w_qkv, l, act=False, name="proj_qkv", tn=QKV_COLS // 4)
        p_gm = _proj(xc, g_mix, w_gm, l, act=True, name="proj_gate", tn=D // 2)
        small, small_t = _proj_small(xc, g_mix, w_small_hi, w_small_lo, l)

        (qat, ka, vat, qbt, ks, kw, vst, vwt, qct, kf, vct, kv_cmp) = _prep(
            p_qkv, small, rope_tab, gains, seq=S)

        kc, vcmp_t = _compress(kv_cmp.reshape(2, G, B * nb, L * d), cmp_pos, nsa_cmp_w1, nsa_cmp_w2,
                               nsa_k_norm[l, 0].reshape(1, d), l, batch=B)

        o_a = _diff_attention(qat, ka, vat, da_lambda[l], gains, batch=B, seq=S, lam_init=lam_init)
        o_b = _nsa_attention(qbt, ks, kw, vst, vwt, kc, vcmp_t, small_t, batch=B, seq=S)
        o_c = _fox_attention(qct, kf, vct, batch=B, seq=S)

        xc = _merge(xc, o_a, o_b, o_c, p_gm, w_a, w_b, w_c, w_o, l)

        xc, ffn_w = _ffn(xc, ffn_g[1], *ffn_w, l, ffn_next[2 * l + 1])
    return xc.reshape(B, S, D)
```

```python
import functools
import math

import jax
import jax.numpy as jnp
from jax import lax
from jax.experimental import pallas as pl
from jax.experimental.pallas import tpu as pltpu

F32 = jnp.float32
BF16 = jnp.bfloat16

HEAD_DIM = 128
LANES = 128
ROPE_THETA = 500000.0
ROPE_FRACTION = 4
EPS = 1e-6
NEG_INF = -1e30
LOG2E = math.log2(math.e)

DA_HEADS = 4
DA_QK_DIM = 64
NSA_HEADS = 8
NSA_GROUPS = 2
NSA_HPG = NSA_HEADS // NSA_GROUPS
NSA_BLOCK = 64
BLOCK_SHIFT = 6
NSA_TOP_N = 8
NSA_WINDOW = 512
NSA_FORCED_SCORE = 1e4
FOX_HEADS = 4

ATT_TILE = 256
FOX_AUG = 2 * LANES
VT_ROWS = LANES + 16

BLK_AQ, BLK_AK, BLK_AV, BLK_BQ, BLK_BKV, BLK_CQ, BLK_CK, BLK_CV = 0, 4, 8, 12, 20, 32, 36, 40
QKV_COLS = 44 * LANES
SMALL_COLS = 3 * LANES

G_DAQ, G_DAK, G_NQ, G_NK0, G_NK1, G_NK2, G_FQ, G_FK, G_FBIAS, G_DAOUT = range(10)

VMEM_BYTES_V7X = 64 * 1024 * 1024
VMEM_LIMIT = VMEM_BYTES_V7X - 8 * 1024 * 1024
VMEM_LIMIT_FFN = VMEM_BYTES_V7X - 3 * 1024 * 1024


def _cparams(sem, limit=VMEM_LIMIT):
    return pltpu.CompilerParams(dimension_semantics=sem, vmem_limit_bytes=limit)


def _dot(a, b):
    return jnp.dot(a, b, preferred_element_type=F32)


def _dot_t(a, b):
    return lax.dot_general(a, b, (((1,), (1,)), ((), ())), preferred_element_type=F32)


def _rms_rows(xf, gain):
    return xf * lax.rsqrt(jnp.mean(xf * xf, axis=-1, keepdims=True) + EPS) * gain


def _ffn_kernel(x_ref, g_ref, wg_ref, wu_ref, wd_ref, *rest):
    n_cast = (len(rest) - 2) // 2
    o_ref, h_scr = rest[n_cast], rest[-1]

    @pl.when(pl.program_id(1) == 0)
    def _():
        xf = x_ref[...]
        h_scr[...] = _rms_rows(xf, g_ref[...]).astype(BF16)
        o_ref[...] = xf

    h = h_scr[...]
    g = _dot(h, wg_ref[...])
    u = _dot(h, wu_ref[...])
    a = (g * jax.nn.sigmoid(g)) * (0.5 * u)
    o_ref[...] += _dot(a.astype(BF16), wd_ref[...])

    for src, dst in zip(rest[:n_cast], rest[n_cast + 1:-1]):
        dst[...] = src[...].astype(BF16)


def _ffn(x2d, gain, wg, wu, wd, layer, cast_next=None, *, tm=1024, tf=512):
    T, D = x2d.shape
    F = wg.shape[2]
    tm = min(tm, T)
    ni, nj = T // tm, F // tf
    in_specs = [
        pl.BlockSpec((tm, D), lambda i, j: (i, 0)),
        pl.BlockSpec((None, 1, D), lambda i, j: (layer, 0, 0)),
        pl.BlockSpec((None, D, tf), lambda i, j: (0, 0, j)),
        pl.BlockSpec((None, D, tf), lambda i, j: (0, 0, j)),
        pl.BlockSpec((None, tf, D), lambda i, j: (0, j, 0)),
    ]
    out_specs = [pl.BlockSpec((tm, D), lambda i, j: (i, 0))]
    out_shape = [jax.ShapeDtypeStruct((T, D), F32)]
    args = [x2d, gain, wg, wu, wd]
    if cast_next is not None:
        nl, *nws = cast_next
        for w in nws:
            _, R, C = w.shape
            blk, imap = (((R // ni, C // nj), lambda i, j: (nl, i, j)) if C == F
                         else ((R // nj, C // ni), lambda i, j: (nl, j, i)))
            in_specs.append(pl.BlockSpec((None,) + blk, imap))
            omap = (lambda i, j: (0, i, j)) if C == F else (lambda i, j: (0, j, i))
            out_specs.append(pl.BlockSpec((None,) + blk, omap))
            out_shape.append(jax.ShapeDtypeStruct((1, R, C), BF16))
            args.append(w)
    outs = pl.pallas_call(
        _ffn_kernel,
        grid=(ni, nj),
        in_specs=in_specs,
        out_specs=tuple(out_specs),
        out_shape=tuple(out_shape),
        scratch_shapes=[pltpu.VMEM((tm, D), BF16)],
        compiler_params=_cparams(("parallel", "arbitrary"), VMEM_LIMIT_FFN),
        name="ffn",
    )(*args)
    return outs[0], tuple(outs[1:])


def _cast_kernel(x_ref, o_ref):
    o_ref[...] = x_ref[...].astype(BF16)


def _to_bf16(w, *, rows, layer=None):
    depth, R, C = w.shape
    first, count = (0, depth) if layer is None else (layer, 1)
    return pl.pallas_call(
        _cast_kernel,
        grid=(count, R // rows),
        in_specs=[pl.BlockSpec((None, rows, C), lambda l, i: (first + l, i, 0))],
        out_specs=pl.BlockSpec((None, rows, C), lambda l, i: (l, i, 0)),
        out_shape=jax.ShapeDtypeStruct((count, R, C), BF16),
        compiler_params=_cparams(("parallel", "parallel")),
        name="cast_bf16",
    )(w)


def _proj_gate_kernel(x_ref, g_ref, w_ref, o_ref, h_scr):
    @pl.when(pl.program_id(1) == 0)
    def _():
        h_scr[...] = _rms_rows(x_ref[...], g_ref[...]).astype(BF16)

    o_ref[...] = jax.nn.sigmoid(_dot(h_scr[...], w_ref[...])).astype(o_ref.dtype)


def _proj_gate(x2d, gain, w, layer, *, tm=1024, tn):
    T, D = x2d.shape
    N = w.shape[2]
    tm = min(tm, T)
    return pl.pallas_call(
        _proj_gate_kernel,
        grid=(T // tm, N // tn),
        in_specs=[
            pl.BlockSpec((tm, D), lambda i, j: (i, 0)),
            pl.BlockSpec((None, 1, D), lambda i, j: (layer, 0, 0)),
            pl.BlockSpec((None, D, tn), lambda i, j: (layer, 0, j)),
        ],
        out_specs=pl.BlockSpec((tm, tn), lambda i, j: (i, j)),
        out_shape=jax.ShapeDtypeStruct((T, N), BF16),
        scratch_shapes=[pltpu.VMEM((tm, D), BF16)],
        compiler_params=_cparams(("parallel", "arbitrary")),
        name="proj_gate",
    )(x2d, gain, w)


def _split3(v):
    hi = v.astype(BF16)
    r = v - hi.astype(F32)
    mid = r.astype(BF16)
    lo = (r - mid.astype(F32)).astype(BF16)
    return hi, mid, lo


def _proj_qkv_kernel(x_ref, g_ref, w_ref, whi_ref, wlo_ref, o_ref, sm_ref, smt_ref, h_scr):
    @pl.when(pl.program_id(1) == 0)
    def _():
        h = _rms_rows(x_ref[...], g_ref[...])
        hi, mid, _ = _split3(h)
        h_scr[...] = hi
        whi = whi_ref[...]
        r = _dot(hi, whi) + (_dot(mid, whi) + _dot(hi, wlo_ref[...]))
        sm_ref[...] = r
        for c in range(r.shape[1] // LANES):
            smt_ref[c * LANES:(c + 1) * LANES, :] = r[:, c * LANES:(c + 1) * LANES].T

    o_ref[...] = _dot(h_scr[...], w_ref[...]).astype(o_ref.dtype)


def _proj_qkv(x2d, gain, w, whi, wlo, layer, *, tm=1024, tn):
    T, D = x2d.shape
    N, Ns = w.shape[2], whi.shape[2]
    tm = min(tm, T)
    small_w = pl.BlockSpec((None, D, Ns), lambda i, j: (layer, 0, 0))
    return pl.pallas_call(
        _proj_qkv_kernel,
        grid=(T // tm, N // tn),
        in_specs=[
            pl.BlockSpec((tm, D), lambda i, j: (i, 0)),
            pl.BlockSpec((None, 1, D), lambda i, j: (layer, 0, 0)),
            pl.BlockSpec((None, D, tn), lambda i, j: (layer, 0, j)),
            small_w, small_w,
        ],
        out_specs=(pl.BlockSpec((tm, tn), lambda i, j: (i, j)),
                   pl.BlockSpec((tm, Ns), lambda i, j: (i, 0)),
                   pl.BlockSpec((Ns, tm), lambda i, j: (0, i))),
        out_shape=(jax.ShapeDtypeStruct((T, N), BF16), jax.ShapeDtypeStruct((T, Ns), F32),
                   jax.ShapeDtypeStruct((Ns, T), F32)),
        scratch_shapes=[pltpu.VMEM((tm, D), BF16)],
        compiler_params=_cparams(("parallel", "arbitrary"), VMEM_LIMIT_FFN),
        name="proj_qkv",
    )(x2d, gain, w, whi, wlo)


def _prep_kernel(p_ref, sm_ref, rope_ref, gains_ref,
                 qat_ref, ka_ref, vat_ref, qbt_ref, ks_ref, kw_ref, vst_ref, vwt_ref,
                 qct_ref, kf_ref, vct_ref, kvc_ref,
                 carry_scr, *, tiles_per_seq):
    i = pl.program_id(0)
    tm = p_ref.shape[0]
    lane = lax.broadcasted_iota(jnp.int32, (tm, LANES), 1)
    src = lax.broadcasted_iota(jnp.int32, (LANES, LANES), 0)
    dst = lax.broadcasted_iota(jnp.int32, (LANES, LANES), 1)

    def const(cond, val=1.0):
        return jnp.where(cond, val, 0.0).astype(BF16)

    def group_mean(width):
        shift = width.bit_length() - 1
        return const((src >> shift) == (dst >> shift), 1.0 / width)

    def swap_halves(width, half):
        d = dst & (width - 1)
        return const(((d < half) & (src == dst + half)) | ((d >= half) & (d < 2 * half) & (src == dst - half)))

    mean128, mean64 = group_mean(HEAD_DIM), group_mean(DA_QK_DIM)
    eye = const(src == dst)

    def dot2(v, mat):
        hi = v.astype(BF16)
        lo = (v - hi.astype(F32)).astype(BF16)
        return _dot(hi, mat) + _dot(lo, mat)

    def raw(c):
        return p_ref[:, c * LANES:(c + 1) * LANES]

    def chunk(c):
        return raw(c).astype(F32)

    def gain(r):
        return gains_ref[r:r + 1, :]

    def rope_tab(r):
        return rope_ref[:, r * LANES:(r + 1) * LANES]


    def tr(y):
        return _dot_t(eye, y.astype(BF16)).astype(BF16)

    def put_vt(ref, n, c):
        ref[n, 0, 0:LANES, :] = tr(raw(c))
        ref[n, 0, LANES:VT_ROWS, :] = jnp.ones((VT_ROWS - LANES, tm), BF16)

    half_a = swap_halves(DA_QK_DIM, DA_QK_DIM // ROPE_FRACTION // 2)
    half_b = swap_halves(HEAD_DIM, HEAD_DIM // ROPE_FRACTION // 2)
    scale_a = DA_QK_DIM ** -0.5 * LOG2E
    scale_b = HEAD_DIM ** -0.5 * LOG2E

    def run_group(specs):
        xs = [chunk(c) for c, *_ in specs]
        ms = [None if nm is None else dot2(x * x, nm[1]) for x, (_, nm, *_) in zip(xs, specs)]
        ys = [x if nm is None else x * lax.rsqrt(m + EPS) * gain(nm[0])
              for x, m, (_, nm, *_) in zip(xs, ms, specs)]
        ps = [None if rp is None else dot2(y, rp[1]) for y, (_, _, rp, *_) in zip(ys, specs)]
        rs = [y if rp is None else y * rope_tab(rp[0]) + p * rope_tab(rp[0] + 1)
              for y, p, (_, _, rp, *_) in zip(ys, ps, specs)]
        rs = [r if sc == 1.0 else r * sc for r, (_, _, _, sc, _, _) in zip(rs, specs)]
        outs = [tr(r) if tp else r.astype(BF16) for r, (_, _, _, _, tp, _) in zip(rs, specs)]
        for o, spec in zip(outs, specs):
            spec[-1](o)

    def nat_sink(ref, lo):
        def sink(o):
            ref[:, lo:lo + LANES] = o
        return sink

    def tile_sink(ref, n, rows=slice(None)):
        def sink(o):
            ref[n, 0, rows, :] = o
        return sink

    def grp_sink(n, g):
        def sink(o):
            kvc_ref[n, g] = o
        return sink

    rope_a, rope_b = (0, half_a), (2, half_b)
    run_group([(BLK_AQ + h, (G_DAQ, mean64), rope_a, scale_a, True, tile_sink(qat_ref, h))
               for h in range(DA_HEADS)]
              + [(BLK_AK + h, (G_DAK, mean64), rope_a, 1.0, False, nat_sink(ka_ref, h * LANES))
                 for h in range(DA_HEADS)])
    run_group([(BLK_BQ + h, (G_NQ, mean128), rope_b, scale_b, True, tile_sink(qbt_ref, h))
               for h in range(NSA_HEADS)])
    run_group([(BLK_BKV + 0 + g, None, rope_b, 1.0, False, grp_sink(0, g)) for g in range(NSA_GROUPS)]
              + [(BLK_BKV + 4 + g, (G_NK1, mean128), rope_b, 1.0, False, nat_sink(ks_ref, g * LANES))
                 for g in range(NSA_GROUPS)]
              + [(BLK_BKV + 8 + g, (G_NK2, mean128), rope_b, 1.0, False, nat_sink(kw_ref, g * LANES))
                 for g in range(NSA_GROUPS)])
    run_group([(BLK_CQ + h, (G_FQ, mean128), None, scale_b, True, tile_sink(qct_ref, h, slice(0, LANES)))
               for h in range(FOX_HEADS)]
              + [(BLK_CK + h, (G_FK, mean128), None, 1.0, False, nat_sink(kf_ref, h * FOX_AUG))
                 for h in range(FOX_HEADS)])
    for g in range(NSA_GROUPS):
        kvc_ref[1, g] = raw(BLK_BKV + 2 + g)
        put_vt(vst_ref, g, BLK_BKV + 6 + g)
        put_vt(vwt_ref, g, BLK_BKV + 10 + g)
    for h in range(DA_HEADS):
        put_vt(vat_ref, h, BLK_AV + h)
    for h in range(FOX_HEADS):
        put_vt(vct_ref, h, BLK_CV + h)

    @pl.when(i % tiles_per_seq == 0)
    def _():
        carry_scr[...] = jnp.zeros_like(carry_scr)

    z = sm_ref[...] + gain(G_FBIAS)
    logf = jnp.minimum(z, 0.0) - jnp.log(1.0 + jnp.exp(-jnp.abs(z)))
    r_id = lax.broadcasted_iota(jnp.int32, (tm, tm), 0)
    c_id = lax.broadcasted_iota(jnp.int32, (tm, tm), 1)
    tri = jnp.where(r_id >= c_id, 1.0, 0.0).astype(BF16)
    hi, mid, lo = _split3(logf)
    cum = _dot(tri, hi) + (_dot(tri, mid) + _dot(tri, lo)) + carry_scr[0:1, :]
    carry_scr[0:1, :] = cum[tm - 1:tm, :]

    for h in range(FOX_HEADS):
        c = jnp.broadcast_to(cum[:, h:h + 1], (tm, LANES)) * LOG2E
        c_hi, c_mid, c_lo = (v.astype(F32) for v in _split3(c))
        ones = jnp.where(lane < 6, 1.0, 0.0)
        q_aug = jnp.where(lane == 0, c_hi, jnp.where(lane == 1, c_mid, jnp.where(lane == 2, c_lo, ones)))
        k_aug = jnp.where(lane == 3, -c_hi, jnp.where(lane == 4, -c_mid, jnp.where(lane == 5, -c_lo, ones)))
        qct_ref[h, 0, LANES:FOX_AUG, :] = tr(q_aug)
        kf_ref[:, h * FOX_AUG + LANES:(h + 1) * FOX_AUG] = k_aug.astype(BF16)


def _prep(p_qkv, small, rope_tab, gains, *, seq):
    tm = ATT_TILE
    T = p_qkv.shape[0]
    nt = T // tm
    G = NSA_GROUPS

    def rows(w):
        return pl.BlockSpec((tm, w), lambda i: (i, 0))

    def tiles_t(n, d=LANES):
        return (jax.ShapeDtypeStruct((n, nt, d, tm), BF16), pl.BlockSpec((n, 1, d, tm), lambda i: (0, i, 0, 0)))

    def nat(w):
        return (jax.ShapeDtypeStruct((T, w), BF16), rows(w))

    def grp():
        return (jax.ShapeDtypeStruct((2, G, T, LANES), BF16),
                pl.BlockSpec((2, G, tm, LANES), lambda i: (0, 0, i, 0)))

    outs = [
        tiles_t(DA_HEADS), nat(DA_HEADS * LANES), tiles_t(DA_HEADS, VT_ROWS),
        tiles_t(NSA_HEADS), nat(G * LANES), nat(G * LANES),
        tiles_t(G, VT_ROWS), tiles_t(G, VT_ROWS),
        tiles_t(FOX_HEADS, FOX_AUG), nat(FOX_HEADS * FOX_AUG), tiles_t(FOX_HEADS, VT_ROWS),
        grp(),
    ]
    return pl.pallas_call(
        functools.partial(_prep_kernel, tiles_per_seq=seq // tm),
        grid=(nt,),
        in_specs=[
            rows(QKV_COLS),
            rows(LANES),
            rows(4 * LANES),
            pl.BlockSpec(gains.shape, lambda i: (0, 0)),
        ],
        out_specs=tuple(o[1] for o in outs),
        out_shape=tuple(o[0] for o in outs),
        scratch_shapes=[pltpu.VMEM((8, LANES), F32)],
        compiler_params=_cparams(("arbitrary",)),
        name="prep",
    )(p_qkv, small, rope_tab, gains)


def _compress_kernel(x_ref, pos_ref, w1_ref, w2_ref, gain_ref, kc_ref, vct_ref, *, batch):
    def mlp(kv):
        blk = (x_ref[kv].astype(F32) + pos_ref[kv]).astype(BF16)
        hid = jax.nn.gelu(_dot(blk, w1_ref[kv].astype(BF16)))
        return _dot(hid.astype(BF16), w2_ref[kv].astype(BF16))

    kc = _rms_rows(mlp(0), gain_ref[...])
    vc = mlp(1)
    nb = kc.shape[0] // batch
    pad = jnp.zeros((LANES - nb, LANES), F32)
    for b in range(batch):
        kc_ref[b] = jnp.concatenate([kc[b * nb:(b + 1) * nb], pad], axis=0).astype(BF16)
        vct_ref[b] = jnp.concatenate([vc[b * nb:(b + 1) * nb], pad], axis=0).T.astype(BF16)


def _compress(xkv, pos, w1, w2, gain, layer, *, batch):
    _, G, R, K = xkv.shape
    d = w1.shape[-1]
    out = jax.ShapeDtypeStruct((G, batch, LANES, LANES), BF16)
    ospec = pl.BlockSpec((None, batch, LANES, LANES), lambda g: (g, 0, 0, 0))
    return pl.pallas_call(
        functools.partial(_compress_kernel, batch=batch),
        grid=(G,),
        in_specs=[
            pl.BlockSpec((2, None, R, K), lambda g: (0, g, 0, 0)),
            pl.BlockSpec((None, 2, 1, K), lambda g: (layer, 0, 0, 0)),
            pl.BlockSpec((None, 2, K, d), lambda g: (layer, 0, 0, 0)),
            pl.BlockSpec((None, 2, d, d), lambda g: (layer, 0, 0, 0)),
            pl.BlockSpec((1, d), lambda g: (0, 0)),
        ],
        out_specs=(ospec, ospec),
        out_shape=(out, out),
        compiler_params=_cparams(("arbitrary",)),
        name="compress",
    )(xkv, pos, w1, w2, gain)


def _online_t(s_ts, mask, carry, v_ts):
    masks = mask if isinstance(mask, (list, tuple)) else [mask] * len(s_ts)
    stats = []
    for s_t, (m, _), mk in zip(s_ts, carry, masks):
        if mk is not None:
            s_t = jnp.where(mk, s_t, NEG_INF)
        m_new = jnp.maximum(m, jnp.max(s_t, axis=0, keepdims=True))
        stats.append((m_new, jnp.exp2(m - m_new), jnp.exp2(s_t - m_new).astype(BF16)))
    return tuple((m_new, alpha * acc_t + _dot(v_t, p))
                 for (m_new, alpha, p), (_, acc_t), v_t in zip(stats, carry, v_ts))


def _attend(scores, values, loop_mask, last_mask, kt0, kt_last, n):
    t = ATT_TILE
    init = tuple((jnp.full((1, t), NEG_INF, F32), jnp.zeros((VT_ROWS, t), F32)) for _ in range(n))

    def body(kt, carry):
        mask = None if loop_mask is None else loop_mask(kt)
        return _online_t(scores(kt), mask, carry, values(kt))

    carry = lax.fori_loop(kt0, kt_last, body, init)
    return _normalise(_online_t(scores(kt_last), last_mask, carry, values(kt_last)))


def _normalise(carry):
    return [acc[0:LANES] / acc[LANES:LANES + 1] for (_, acc) in carry]


def _causal_mask_t(t):
    return lax.broadcasted_iota(jnp.int32, (t, t), 0) <= lax.broadcasted_iota(jnp.int32, (t, t), 1)


def _diff_kernel(qt_ref, k_ref, vt_ref, lam_ref, gains_ref, o_ref, *, lam_init):
    i = pl.program_id(2)
    t = ATT_TILE
    nh = qt_ref.shape[0]
    drow = lax.broadcasted_iota(jnp.int32, (LANES, t), 0)
    qts = []
    for h in range(nh):
        q = qt_ref[h, 0]
        qts.append(jnp.where(drow < DA_QK_DIM, q, jnp.zeros_like(q)))
        qts.append(jnp.where(drow < DA_QK_DIM, jnp.zeros_like(q), q))

    def scores(kt):
        off = pl.multiple_of(kt * t, t)
        ks = [k_ref[pl.ds(off, t), h * LANES:(h + 1) * LANES] for h in range(nh)]
        return tuple(_dot(ks[c // 2], qts[c]) for c in range(2 * nh))

    def values(kt):
        return [vt_ref[c // 2, kt] for c in range(2 * nh)]

    outs = _attend(scores, values, None, _causal_mask_t(t), 0, i, 2 * nh)

    lp = lam_ref[...]
    lam = (jnp.exp(jnp.sum(lp[0:1] * lp[1:2], axis=-1, keepdims=True))
           - jnp.exp(jnp.sum(lp[2:3] * lp[3:4], axis=-1, keepdims=True)) + lam_init)
    for h in range(nh):
        o = (outs[2 * h] - lam * outs[2 * h + 1]).T
        o = _rms_rows(o, gains_ref[G_DAOUT:G_DAOUT + 1, :]) * (1.0 - lam_init)
        o_ref[:, h * LANES:(h + 1) * LANES] = o.astype(BF16)


def _diff_attention(qat, ka, vat, da_lambda, gains, *, batch, seq, lam_init, heads_per_step=4):
    t = ATT_TILE
    nq = seq // t
    T = batch * seq
    hp = heads_per_step
    return pl.pallas_call(
        functools.partial(_diff_kernel, lam_init=lam_init),
        grid=(batch, DA_HEADS // hp, nq),
        in_specs=[
            pl.BlockSpec((hp, 1, LANES, t), lambda b, p, i: (p, b * nq + i, 0, 0)),
            pl.BlockSpec((seq, hp * LANES), lambda b, p, i: (b, p)),
            pl.BlockSpec((hp, nq, VT_ROWS, t), lambda b, p, i: (p, b, 0, 0)),
            pl.BlockSpec(da_lambda.shape, lambda b, p, i: (0, 0)),
            pl.BlockSpec(gains.shape, lambda b, p, i: (0, 0)),
        ],
        out_specs=pl.BlockSpec((t, hp * LANES), lambda b, p, i: (b * nq + i, p)),
        out_shape=jax.ShapeDtypeStruct((T, DA_HEADS * LANES), BF16),
        compiler_params=_cparams(("parallel", "parallel", "arbitrary")),
        name="diff_attn",
    )(qat, ka, vat, da_lambda, gains)


def _fox_kernel(qt_ref, k_ref, vt_ref, o_ref):
    i = pl.program_id(1)
    t = ATT_TILE
    nh, bs = qt_ref.shape[0], qt_ref.shape[1]
    chains = [(b, h) for b in range(bs) for h in range(nh)]
    qts = [qt_ref[h, b, 0] for b, h in chains]

    def scores(kt):
        off = pl.multiple_of(kt * t, t)
        return tuple(_dot(k_ref[b, pl.ds(off, t), h * FOX_AUG:(h + 1) * FOX_AUG], q)
                     for (b, h), q in zip(chains, qts))

    def values(kt):
        return [vt_ref[h, b, kt] for b, h in chains]

    outs = _attend(scores, values, None, _causal_mask_t(t), 0, i, len(chains))
    for (b, h), o in zip(chains, outs):
        o_ref[b, :, h * LANES:(h + 1) * LANES] = o.T.astype(BF16)


def _fox_attention(qct, kf, vct, *, batch, seq):
    t = ATT_TILE
    nq = seq // t
    nh = FOX_HEADS
    bs = 2 if batch % 2 == 0 else 1
    nbp = batch // bs
    out = pl.pallas_call(
        _fox_kernel,
        grid=(nbp, nq),
        in_specs=[
            pl.BlockSpec((nh, None, bs, 1, FOX_AUG, t), lambda p, i: (0, p, 0, i, 0, 0)),
            pl.BlockSpec((None, bs, seq, nh * FOX_AUG), lambda p, i: (p, 0, 0, 0)),
            pl.BlockSpec((nh, None, bs, nq, VT_ROWS, t), lambda p, i: (0, p, 0, 0, 0, 0)),
        ],
        out_specs=pl.BlockSpec((None, bs, t, nh * LANES), lambda p, i: (p, 0, i, 0)),
        out_shape=jax.ShapeDtypeStruct((nbp, bs, seq, nh * LANES), BF16),
        compiler_params=_cparams(("parallel", "arbitrary")),
        name="fox_attn",
    )(qct.reshape(nh, nbp, bs, nq, FOX_AUG, t), kf.reshape(nbp, bs, seq, nh * FOX_AUG),
      vct.reshape(nh, nbp, bs, nq, VT_ROWS, t))
    return out.reshape(batch * seq, nh * LANES)


def _nsa_kernel(qt_ref, ks_ref, vst_ref, kw_ref, vwt_ref, kc_ref, vct_ref, gt_ref, o_ref, *, n_sel, nb):
    i = pl.program_id(1)
    t = ATT_TILE
    lo = i * t
    nh = qt_ref.shape[0]
    hpg = nh // NSA_GROUPS
    qts = [qt_ref[h, 0] for h in range(nh)]
    tq_row = lo + lax.broadcasted_iota(jnp.int32, (1, t), 1)
    nidx = lax.broadcasted_iota(jnp.int32, (nb, t), 0)
    nidx_f = nidx.astype(F32)
    cmask = nidx * NSA_BLOCK + (NSA_BLOCK - 1) <= tq_row
    cur = tq_row >> BLOCK_SHIFT
    forced = (nidx == 0) | (nidx == cur) | (nidx == cur - 1)
    zpad = jnp.zeros((LANES - nb, t), F32)

    def lane_tile(v):
        return jnp.concatenate([v, zpad], axis=0).astype(BF16)

    o_cmp, sel_b = [], []
    for g in range(NSA_GROUPS):
        kc, vct = kc_ref[g, 0:nb, :], vct_ref[g]
        importance = jnp.zeros((nb, t), F32)
        for h in range(g * hpg, (g + 1) * hpg):
            s = jnp.where(cmask, _dot(kc, qts[h]), NEG_INF)
            e = jnp.where(cmask, jnp.exp2(s - jnp.max(s, axis=0, keepdims=True)), 0.0)
            p = e / jnp.maximum(jnp.sum(e, axis=0, keepdims=True), 1e-30)
            importance = importance + p
            o_cmp.append(_dot(vct, lane_tile(p)))

        score = jnp.where(forced, NSA_FORCED_SCORE, jnp.where(nidx <= cur, importance, -1.0))
        sel = jnp.zeros((nb, t), F32)
        for _ in range(n_sel):
            best = jnp.max(score, axis=0, keepdims=True)
            first = jnp.min(jnp.where(score == best, nidx_f, float(nb)), axis=0, keepdims=True)
            hit = nidx_f == first
            sel = jnp.where(hit, 1.0, sel)
            score = jnp.where(hit, -3e38, score)
        sel_b.append(lane_tile(sel))

    e_key = lax.broadcasted_iota(jnp.int32, (t, LANES), 0) >> BLOCK_SHIFT
    e_blk = lax.broadcasted_iota(jnp.int32, (t, LANES), 1)
    krow = lax.broadcasted_iota(jnp.int32, (t, t), 0)
    qcol = lax.broadcasted_iota(jnp.int32, (t, t), 1)
    causal = krow <= qcol

    def scores_of(k_ref):
        def scores(kt):
            off = pl.multiple_of(kt * t, t)
            ks = [k_ref[pl.ds(off, t), g * LANES:(g + 1) * LANES] for g in range(NSA_GROUPS)]
            return tuple(_dot(ks[h // hpg], qts[h]) for h in range(nh))
        return scores

    def values_of(vt_ref):
        return lambda kt: [vt_ref[h // hpg, kt] for h in range(nh)]

    def selected(kt, extra=None):
        expand = jnp.where(e_blk == e_key + kt * (t // NSA_BLOCK), 1.0, 0.0).astype(BF16)
        per_group = [_dot(expand, sb) > 0.5 for sb in sel_b]
        if extra is not None:
            per_group = [m & extra for m in per_group]
        return [per_group[h // hpg] for h in range(nh)]

    o_slc = _attend(scores_of(ks_ref), values_of(vst_ref), selected, selected(i, causal), 0, i, nh)

    wt = NSA_WINDOW // t
    win_scores, win_values = scores_of(kw_ref), values_of(vwt_ref)
    upper = krow > qcol
    far = jnp.maximum(i - wt, 0)
    penalty = jnp.where(i >= wt, 0.0, NEG_INF)
    s_far, s_diag = win_scores(far), win_scores(i)
    stats = []
    for c in range(nh):
        s = jnp.where(upper, s_far[c] + penalty, s_diag[c])
        m = jnp.max(s, axis=0, keepdims=True)
        stats.append((m, jnp.exp2(s - m).astype(BF16)))
    v_far, v_diag = win_values(far), win_values(i)
    carry = tuple((m, _dot(v_far[c], jnp.where(upper, p, jnp.zeros_like(p)))
                   + _dot(v_diag[c], jnp.where(upper, jnp.zeros_like(p), p)))
                  for c, (m, p) in enumerate(stats))
    carry = lax.fori_loop(jnp.maximum(i - wt + 1, 0), i,
                          lambda kt, cr: _online_t(win_scores(kt), None, cr, win_values(kt)), carry)
    o_win = _normalise(carry)

    for h in range(nh):
        g, r = h // hpg, 3 * (h % hpg)
        gt = jax.nn.sigmoid(gt_ref[(1 + g) * LANES + r:(1 + g) * LANES + r + 3, :])
        o = gt[0:1] * o_cmp[h] + gt[1:2] * o_slc[h] + gt[2:3] * o_win[h]
        o_ref[:, h * LANES:(h + 1) * LANES] = o.T.astype(BF16)


def _nsa_attention(qbt, ks, kw, vst, vwt, kc, vct, small_t, *, batch, seq):
    t = ATT_TILE
    nq = seq // t
    T = batch * seq
    n_sel = min(NSA_TOP_N, seq // NSA_BLOCK)
    G = NSA_GROUPS
    k_spec = pl.BlockSpec((seq, G * LANES), lambda b, i: (b, 0))
    vt_spec = pl.BlockSpec((G, nq, VT_ROWS, t), lambda b, i: (0, b, 0, 0))
    c_spec = pl.BlockSpec((G, None, LANES, LANES), lambda b, i: (0, b, 0, 0))
    return pl.pallas_call(
        functools.partial(_nsa_kernel, n_sel=n_sel, nb=seq // NSA_BLOCK),
        grid=(batch, nq),
        in_specs=[
            pl.BlockSpec((NSA_HEADS, 1, LANES, t), lambda b, i: (0, b * nq + i, 0, 0)),
            k_spec, vt_spec, k_spec, vt_spec, c_spec, c_spec,
            pl.BlockSpec((SMALL_COLS, t), lambda b, i: (0, b * nq + i)),
        ],
        out_specs=pl.BlockSpec((t, NSA_HEADS * LANES), lambda b, i: (b * nq + i, 0)),
        out_shape=jax.ShapeDtypeStruct((T, NSA_HEADS * LANES), BF16),
        compiler_params=_cparams(("parallel", "arbitrary")),
        name="nsa_attn",
    )(qbt, ks, vst, kw, vwt, kc, vct, small_t)


def _merge_kernel(x_ref, oa_ref, ob_ref, oc_ref, g0_ref, g1_ref, g2_ref,
                  wa_ref, wb_ref, wc_ref, wo_ref, o_ref):
    y = (g0_ref[...].astype(F32) * _dot(oa_ref[...], wa_ref[...])
         + g1_ref[...].astype(F32) * _dot(ob_ref[...], wb_ref[...])
         + g2_ref[...].astype(F32) * _dot(oc_ref[...], wc_ref[...]))
    o_ref[...] = x_ref[...] + _dot(y.astype(BF16), wo_ref[...])


def _merge(x2d, oa, ob, oc, p_gm, wa, wb, wc, wo, layer, *, tm=256):
    T, D = x2d.shape

    def rows(w, c=0):
        return pl.BlockSpec((tm, w), lambda i: (i, c))

    def full(w):
        return pl.BlockSpec((None,) + w.shape[1:], lambda i: (layer, 0, 0))

    return pl.pallas_call(
        _merge_kernel,
        grid=(T // tm,),
        in_specs=[rows(D), rows(oa.shape[1]), rows(ob.shape[1]), rows(oc.shape[1]),
                  rows(D, 0), rows(D, 1), rows(D, 2), full(wa), full(wb), full(wc), full(wo)],
        out_specs=rows(D),
        out_shape=jax.ShapeDtypeStruct((T, D), F32),
        compiler_params=_cparams(("parallel",)),
        name="merge",
    )(x2d, oa, ob, oc, p_gm, p_gm, p_gm, wa, wb, wc, wo)


def _rope_tables(positions, width, rot_dim):
    inv_freq = ROPE_THETA ** (-jnp.arange(0, rot_dim, 2, dtype=F32) / rot_dim)
    ang = positions.astype(F32).reshape(-1, 1) * inv_freq
    cos, sin = jnp.cos(ang), jnp.sin(ang)
    T = ang.shape[0]
    rest = width - rot_dim
    c = jnp.concatenate([cos, cos, jnp.ones((T, rest), F32)], axis=1)
    s = jnp.concatenate([-sin, sin, jnp.zeros((T, rest), F32)], axis=1)
    rep = LANES // width
    return [jnp.tile(t, (1, rep)) for t in (c, s)]


def _pad_lanes(v):
    return jnp.pad(v.astype(F32), (0, LANES - v.shape[0]))


def kernel(x, positions, ffn1_norm, ffn1_w_gate, ffn1_w_up, ffn1_w_down, mix_norm, w_in, da_q_norm, da_k_norm, da_lambda, da_out_norm, nsa_q_norm, nsa_k_norm, nsa_cmp_pos, nsa_cmp_w1, nsa_cmp_w2, fox_q_norm, fox_k_norm, fox_f_bias, w_branch_a, w_branch_b, w_branch_c, w_out, ffn2_norm, ffn2_w_gate, ffn2_w_up, ffn2_w_down):
    B, S, D = x.shape
    depth = w_in.shape[0]
    T = B * S
    G, L, d = NSA_GROUPS, NSA_BLOCK, HEAD_DIM
    nb = S // L
    assert S % ATT_TILE == 0 and nb <= LANES and nb % 8 == 0 and NSA_WINDOW % ATT_TILE == 0

    rope_tab = jnp.concatenate(_rope_tables(positions, DA_QK_DIM, DA_QK_DIM // ROPE_FRACTION)
                               + _rope_tables(positions, HEAD_DIM, HEAD_DIM // ROPE_FRACTION), axis=1)

    widths = [512, 512, 512, 1024, 1536, 24, 512, 512, 512, 4, 3 * D]
    offs = [0]
    for w in widths:
        offs.append(offs[-1] + w)
    (o_aq, o_ak, o_av, o_bq, o_bkv, o_bg, o_cq, o_ck, o_cv, o_cf, o_gm, o_end) = offs

    w_qkv = jnp.concatenate([w_in[:, :, o_aq:o_bg], w_in[:, :, o_cq:o_cf]], axis=2).astype(BF16)
    w_gm = w_in[:, :, o_gm:o_end].astype(BF16)
    zpad = lambda n: jnp.zeros((depth, D, n), F32)
    w_small = jnp.concatenate([
        w_in[:, :, o_cf:o_gm], zpad(LANES - FOX_HEADS),
        w_in[:, :, o_bg:o_bg + 12], zpad(LANES - 12),
        w_in[:, :, o_bg + 12:o_bg + 24], zpad(LANES - 12)], axis=2)
    w_small_hi = w_small.astype(BF16)
    w_small_lo = (w_small - w_small_hi.astype(F32)).astype(BF16)
    ffn_f32 = [(ffn1_w_gate, ffn1_w_up, ffn1_w_down), (ffn2_w_gate, ffn2_w_up, ffn2_w_down)]
    ffn_g = [ffn1_norm.reshape(depth, 1, D), ffn2_norm.reshape(depth, 1, D)]
    ffn_next = [(k // 2, *ffn_f32[k % 2]) for k in range(1, 2 * depth)] + [None]
    ffn_w = tuple(_to_bf16(w, rows=256, layer=0) for w in ffn_f32[0])
    g_mix = mix_norm.reshape(depth, 1, D)
    w_a, w_b, w_c, w_o = (_to_bf16(w, rows=512) for w in (w_branch_a, w_branch_b, w_branch_c, w_out))
    cmp_pos = nsa_cmp_pos.reshape(depth, 2, 1, L * d)

    xc = x.reshape(T, D)
    for l in range(depth):
        lam_init = 0.8 - 0.6 * math.exp(-0.3 * l)
        gains = jnp.stack([
            jnp.tile(da_q_norm[l], 2), jnp.tile(da_k_norm[l], 2), nsa_q_norm[l],
            nsa_k_norm[l, 0], nsa_k_norm[l, 1], nsa_k_norm[l, 2], fox_q_norm[l], fox_k_norm[l],
            _pad_lanes(fox_f_bias[l]), da_out_norm[l]] + [jnp.zeros((LANES,), F32)] * 6).astype(F32)

        xc, ffn_w = _ffn(xc, ffn_g[0], *ffn_w, l, ffn_next[2 * l])

        p_qkv, small, small_t = _proj_qkv(xc, g_mix, w_qkv, w_small_hi, w_small_lo, l, tn=4 * LANES)
        p_gm = _proj_gate(xc, g_mix, w_gm, l, tn=D // 2)

        (qat, ka, vat, qbt, ks, kw, vst, vwt, qct, kf, vct, kv_cmp) = _prep(
            p_qkv, small, rope_tab, gains, seq=S)

        kc, vcmp_t = _compress(kv_cmp.reshape(2, G, B * nb, L * d), cmp_pos, nsa_cmp_w1, nsa_cmp_w2,
                               nsa_k_norm[l, 0].reshape(1, d), l, batch=B)

        o_a = _diff_attention(qat, ka, vat, da_lambda[l], gains, batch=B, seq=S, lam_init=lam_init)
        o_b = _nsa_attention(qbt, ks, kw, vst, vwt, kc, vcmp_t, small_t, batch=B, seq=S)
        o_c = _fox_attention(qct, kf, vct, batch=B, seq=S)

        xc = _merge(xc, o_a, o_b, o_c, p_gm, w_a, w_b, w_c, w_o, l)

        xc, ffn_w = _ffn(xc, ffn_g[1], *ffn_w, l, ffn_next[2 * l + 1])
    return xc.reshape(B, S, D)
```

```python
import functools
import math

import jax
import jax.numpy as jnp
from jax import lax
from jax.experimental import pallas as pl
from jax.experimental.pallas import tpu as pltpu

F32 = jnp.float32
BF16 = jnp.bfloat16

HEAD_DIM = 128
LANES = 128
ROPE_THETA = 500000.0
ROPE_FRACTION = 4
EPS = 1e-6
NEG_INF = -1e30
LOG2E = math.log2(math.e)

DA_HEADS = 4
DA_QK_DIM = 64
NSA_HEADS = 8
NSA_GROUPS = 2
NSA_HPG = NSA_HEADS // NSA_GROUPS
NSA_BLOCK = 64
BLOCK_SHIFT = 6
NSA_TOP_N = 8
NSA_WINDOW = 512
NSA_FORCED_SCORE = 1e4
FOX_HEADS = 4

ATT_TILE = 256
FOX_AUG = 2 * LANES
VT_ROWS = LANES + 16

BLK_AQ, BLK_AK, BLK_AV, BLK_BQ, BLK_BKV, BLK_CQ, BLK_CK, BLK_CV = 0, 4, 8, 12, 20, 32, 36, 40
QKV_COLS = 44 * LANES
SMALL_COLS = 3 * LANES

G_DAQ, G_DAK, G_NQ, G_NK0, G_NK1, G_NK2, G_FQ, G_FK, G_FBIAS, G_DAOUT = range(10)

VMEM_BYTES_V7X = 64 * 1024 * 1024
VMEM_LIMIT = VMEM_BYTES_V7X - 8 * 1024 * 1024
VMEM_LIMIT_FFN = VMEM_BYTES_V7X - 3 * 1024 * 1024


def _cparams(sem, limit=VMEM_LIMIT):
    return pltpu.CompilerParams(dimension_semantics=sem, vmem_limit_bytes=limit)


def _dot(a, b):
    return jnp.dot(a, b, preferred_element_type=F32)


def _dot_t(a, b):
    return lax.dot_general(a, b, (((1,), (1,)), ((), ())), preferred_element_type=F32)


def _rms_rows(xf, gain):
    return xf * lax.rsqrt(jnp.mean(xf * xf, axis=-1, keepdims=True) + EPS) * gain


def _ffn_kernel(x_ref, g_ref, wg_ref, wu_ref, wd_ref, *rest):
    n_cast = (len(rest) - 2) // 2
    o_ref, h_scr = rest[n_cast], rest[-1]

    @pl.when(pl.program_id(1) == 0)
    def _():
        xf = x_ref[...]
        h_scr[...] = _rms_rows(xf, g_ref[...]).astype(BF16)
        o_ref[...] = xf

    h = h_scr[...]
    g = _dot(h, wg_ref[...])
    u = _dot(h, wu_ref[...])
    a = (g * jax.nn.sigmoid(g)) * (0.5 * u)
    o_ref[...] += _dot(a.astype(BF16), wd_ref[...])

    for src, dst in zip(rest[:n_cast], rest[n_cast + 1:-1]):
        dst[...] = src[...].astype(BF16)


def _ffn(x2d, gain, wg, wu, wd, layer, cast_next=None, *, tm=1024, tf=512):
    T, D = x2d.shape
    F = wg.shape[2]
    tm = min(tm, T)
    ni, nj = T // tm, F // tf
    in_specs = [
        pl.BlockSpec((tm, D), lambda i, j: (i, 0)),
        pl.BlockSpec((None, 1, D), lambda i, j: (layer, 0, 0)),
        pl.BlockSpec((None, D, tf), lambda i, j: (0, 0, j)),
        pl.BlockSpec((None, D, tf), lambda i, j: (0, 0, j)),
        pl.BlockSpec((None, tf, D), lambda i, j: (0, j, 0)),
    ]
    out_specs = [pl.BlockSpec((tm, D), lambda i, j: (i, 0))]
    out_shape = [jax.ShapeDtypeStruct((T, D), F32)]
    args = [x2d, gain, wg, wu, wd]
    if cast_next is not None:
        nl, *nws = cast_next
        for w in nws:
            _, R, C = w.shape
            blk, imap = (((R // ni, C // nj), lambda i, j: (nl, i, j)) if C == F
                         else ((R // nj, C // ni), lambda i, j: (nl, j, i)))
            in_specs.append(pl.BlockSpec((None,) + blk, imap))
            omap = (lambda i, j: (0, i, j)) if C == F else (lambda i, j: (0, j, i))
            out_specs.append(pl.BlockSpec((None,) + blk, omap))
            out_shape.append(jax.ShapeDtypeStruct((1, R, C), BF16))
            args.append(w)
    outs = pl.pallas_call(
        _ffn_kernel,
        grid=(ni, nj),
        in_specs=in_specs,
        out_specs=tuple(out_specs),
        out_shape=tuple(out_shape),
        scratch_shapes=[pltpu.VMEM((tm, D), BF16)],
        compiler_params=_cparams(("parallel", "arbitrary"), VMEM_LIMIT_FFN),
        name="ffn",
    )(*args)
    return outs[0], tuple(outs[1:])


def _cast_kernel(x_ref, o_ref):
    o_ref[...] = x_ref[...].astype(BF16)


def _to_bf16(w, *, rows, layer=None):
    depth, R, C = w.shape
    first, count = (0, depth) if layer is None else (layer, 1)
    return pl.pallas_call(
        _cast_kernel,
        grid=(count, R // rows),
        in_specs=[pl.BlockSpec((None, rows, C), lambda l, i: (first + l, i, 0))],
        out_specs=pl.BlockSpec((None, rows, C), lambda l, i: (l, i, 0)),
        out_shape=jax.ShapeDtypeStruct((count, R, C), BF16),
        compiler_params=_cparams(("parallel", "parallel")),
        name="cast_bf16",
    )(w)


def _proj_kernel(x_ref, g_ref, w_ref, o_ref, h_scr, *, gate):
    @pl.when(pl.program_id(1) == 0)
    def _():
        h_scr[...] = _rms_rows(x_ref[...], g_ref[...]).astype(BF16)

    r = _dot(h_scr[...], w_ref[...])
    o_ref[...] = (jax.nn.sigmoid(r) if gate else r).astype(o_ref.dtype)


def _proj(x2d, gain, w, layer, *, gate, tm=1024, tn):
    T, D = x2d.shape
    N = w.shape[2]
    tm = min(tm, T)
    return pl.pallas_call(
        functools.partial(_proj_kernel, gate=gate),
        grid=(T // tm, N // tn),
        in_specs=[
            pl.BlockSpec((tm, D), lambda i, j: (i, 0)),
            pl.BlockSpec((None, 1, D), lambda i, j: (layer, 0, 0)),
            pl.BlockSpec((None, D, tn), lambda i, j: (layer, 0, j)),
        ],
        out_specs=pl.BlockSpec((tm, tn), lambda i, j: (i, j)),
        out_shape=jax.ShapeDtypeStruct((T, N), BF16),
        scratch_shapes=[pltpu.VMEM((tm, D), BF16)],
        compiler_params=_cparams(("parallel", "arbitrary")),
        name="proj_gate" if gate else "proj_qkv",
    )(x2d, gain, w)


def _split3(v):
    hi = v.astype(BF16)
    r = v - hi.astype(F32)
    mid = r.astype(BF16)
    lo = (r - mid.astype(F32)).astype(BF16)
    return hi, mid, lo


def _proj_small_kernel(x_ref, g_ref, whi_ref, wlo_ref, o_ref, ot_ref):
    h = _rms_rows(x_ref[...], g_ref[...])
    hi, mid, _ = _split3(h)
    whi = whi_ref[...]
    r = _dot(hi, whi) + (_dot(mid, whi) + _dot(hi, wlo_ref[...]))
    o_ref[...] = r
    for c in range(r.shape[1] // LANES):
        ot_ref[c * LANES:(c + 1) * LANES, :] = r[:, c * LANES:(c + 1) * LANES].T


def _proj_small(x2d, gain, whi, wlo, layer, *, tm=512):
    T, D = x2d.shape
    N = whi.shape[2]
    return pl.pallas_call(
        _proj_small_kernel,
        grid=(T // tm,),
        in_specs=[
            pl.BlockSpec((tm, D), lambda i: (i, 0)),
            pl.BlockSpec((None, 1, D), lambda i: (layer, 0, 0)),
            pl.BlockSpec((None, D, N), lambda i: (layer, 0, 0)),
            pl.BlockSpec((None, D, N), lambda i: (layer, 0, 0)),
        ],
        out_specs=(pl.BlockSpec((tm, N), lambda i: (i, 0)), pl.BlockSpec((N, tm), lambda i: (0, i))),
        out_shape=(jax.ShapeDtypeStruct((T, N), F32), jax.ShapeDtypeStruct((N, T), F32)),
        compiler_params=_cparams(("parallel",)),
        name="proj_small",
    )(x2d, gain, whi, wlo)


def _prep_kernel(p_ref, sm_ref, rope_ref, gains_ref,
                 qat_ref, ka_ref, vat_ref, qbt_ref, ks_ref, kw_ref, vst_ref, vwt_ref,
                 qct_ref, kf_ref, vct_ref, kvc_ref,
                 carry_scr, *, tiles_per_seq):
    i = pl.program_id(0)
    tm = p_ref.shape[0]
    lane = lax.broadcasted_iota(jnp.int32, (tm, LANES), 1)
    src = lax.broadcasted_iota(jnp.int32, (LANES, LANES), 0)
    dst = lax.broadcasted_iota(jnp.int32, (LANES, LANES), 1)

    def const(cond, val=1.0):
        return jnp.where(cond, val, 0.0).astype(BF16)

    def group_mean(width):
        shift = width.bit_length() - 1
        return const((src >> shift) == (dst >> shift), 1.0 / width)

    def swap_halves(width, half):
        d = dst & (width - 1)
        return const(((d < half) & (src == dst + half)) | ((d >= half) & (d < 2 * half) & (src == dst - half)))

    mean128, mean64 = group_mean(HEAD_DIM), group_mean(DA_QK_DIM)
    eye = const(src == dst)

    def dot1(v, mat):
        return _dot(v.astype(BF16), mat)

    def raw(c):
        return p_ref[:, c * LANES:(c + 1) * LANES]

    def chunk(c):
        return raw(c).astype(F32)

    def gain(r):
        return gains_ref[r:r + 1, :]

    def rope_tab(r):
        return rope_ref[r]


    def tr(y):
        return _dot_t(eye, y.astype(BF16)).astype(BF16)

    def put_vt(ref, n, c):
        ref[n, 0, 0:LANES, :] = tr(raw(c))
        ref[n, 0, LANES:VT_ROWS, :] = jnp.ones((VT_ROWS - LANES, tm), BF16)

    half_a = swap_halves(DA_QK_DIM, DA_QK_DIM // ROPE_FRACTION // 2)
    half_b = swap_halves(HEAD_DIM, HEAD_DIM // ROPE_FRACTION // 2)
    scale_a = DA_QK_DIM ** -0.5 * LOG2E
    scale_b = HEAD_DIM ** -0.5 * LOG2E

    def run_group(specs):
        xs = [chunk(c) for c, *_ in specs]
        ms = [None if nm is None else dot1(x * x, nm[1]) for x, (_, nm, *_) in zip(xs, specs)]
        ys = [x if nm is None else x * lax.rsqrt(m + EPS) * gain(nm[0])
              for x, m, (_, nm, *_) in zip(xs, ms, specs)]
        ps = [None if rp is None else dot1(y, rp[1]) for y, (_, _, rp, *_) in zip(ys, specs)]
        rs = [y if rp is None else y * rope_tab(rp[0]) + p * rope_tab(rp[0] + 1)
              for y, p, (_, _, rp, *_) in zip(ys, ps, specs)]
        rs = [r if sc == 1.0 else r * sc for r, (_, _, _, sc, _, _) in zip(rs, specs)]
        outs = [tr(r) if tp else r.astype(BF16) for r, (_, _, _, _, tp, _) in zip(rs, specs)]
        for o, spec in zip(outs, specs):
            spec[-1](o)

    def nat_sink(ref, lo):
        def sink(o):
            ref[:, lo:lo + LANES] = o
        return sink

    def tile_sink(ref, n, rows=slice(None)):
        def sink(o):
            ref[n, 0, rows, :] = o
        return sink

    def grp_sink(n, g):
        def sink(o):
            kvc_ref[n, g] = o
        return sink

    rope_a, rope_b = (0, half_a), (2, half_b)
    run_group([(BLK_AQ + h, (G_DAQ, mean64), rope_a, scale_a, True, tile_sink(qat_ref, h))
               for h in range(DA_HEADS)]
              + [(BLK_AK + h, (G_DAK, mean64), rope_a, 1.0, False, nat_sink(ka_ref, h * LANES))
                 for h in range(DA_HEADS)])
    run_group([(BLK_BQ + h, (G_NQ, mean128), rope_b, scale_b, True, tile_sink(qbt_ref, h))
               for h in range(NSA_HEADS)])
    run_group([(BLK_BKV + 0 + g, None, rope_b, 1.0, False, grp_sink(0, g)) for g in range(NSA_GROUPS)]
              + [(BLK_BKV + 4 + g, (G_NK1, mean128), rope_b, 1.0, False, nat_sink(ks_ref, g * LANES))
                 for g in range(NSA_GROUPS)]
              + [(BLK_BKV + 8 + g, (G_NK2, mean128), rope_b, 1.0, False, nat_sink(kw_ref, g * LANES))
                 for g in range(NSA_GROUPS)])
    run_group([(BLK_CQ + h, (G_FQ, mean128), None, scale_b, True, tile_sink(qct_ref, h, slice(0, LANES)))
               for h in range(FOX_HEADS)]
              + [(BLK_CK + h, (G_FK, mean128), None, 1.0, False, nat_sink(kf_ref, h * FOX_AUG))
                 for h in range(FOX_HEADS)])
    for g in range(NSA_GROUPS):
        kvc_ref[1, g] = raw(BLK_BKV + 2 + g)
        put_vt(vst_ref, g, BLK_BKV + 6 + g)
        put_vt(vwt_ref, g, BLK_BKV + 10 + g)
    for h in range(DA_HEADS):
        put_vt(vat_ref, h, BLK_AV + h)
    for h in range(FOX_HEADS):
        put_vt(vct_ref, h, BLK_CV + h)

    @pl.when(i % tiles_per_seq == 0)
    def _():
        carry_scr[...] = jnp.zeros_like(carry_scr)

    z = sm_ref[...] + gain(G_FBIAS)
    logf = jnp.minimum(z, 0.0) - jnp.log(1.0 + jnp.exp(-jnp.abs(z)))
    r_id = lax.broadcasted_iota(jnp.int32, (tm, tm), 0)
    c_id = lax.broadcasted_iota(jnp.int32, (tm, tm), 1)
    tri = jnp.where(r_id >= c_id, 1.0, 0.0).astype(BF16)
    hi, mid, lo = _split3(logf)
    cum = _dot(tri, hi) + (_dot(tri, mid) + _dot(tri, lo)) + carry_scr[0:1, :]
    carry_scr[0:1, :] = cum[tm - 1:tm, :]

    for h in range(FOX_HEADS):
        c = jnp.broadcast_to(cum[:, h:h + 1], (tm, LANES)) * LOG2E
        c_hi, c_mid, c_lo = (v.astype(F32) for v in _split3(c))
        ones = jnp.where(lane < 6, 1.0, 0.0)
        q_aug = jnp.where(lane == 0, c_hi, jnp.where(lane == 1, c_mid, jnp.where(lane == 2, c_lo, ones)))
        k_aug = jnp.where(lane == 3, -c_hi, jnp.where(lane == 4, -c_mid, jnp.where(lane == 5, -c_lo, ones)))
        qct_ref[h, 0, LANES:FOX_AUG, :] = tr(q_aug)
        kf_ref[:, h * FOX_AUG + LANES:(h + 1) * FOX_AUG] = k_aug.astype(BF16)


def _prep(p_qkv, small, rope_tab, gains, *, seq):
    tm = ATT_TILE
    T = p_qkv.shape[0]
    nt = T // tm
    G = NSA_GROUPS

    def rows(w):
        return pl.BlockSpec((tm, w), lambda i: (i, 0))

    def tiles_t(n, d=LANES):
        return (jax.ShapeDtypeStruct((n, nt, d, tm), BF16), pl.BlockSpec((n, 1, d, tm), lambda i: (0, i, 0, 0)))

    def nat(w):
        return (jax.ShapeDtypeStruct((T, w), BF16), rows(w))

    def grp():
        return (jax.ShapeDtypeStruct((2, G, T, LANES), BF16),
                pl.BlockSpec((2, G, tm, LANES), lambda i: (0, 0, i, 0)))

    outs = [
        tiles_t(DA_HEADS), nat(DA_HEADS * LANES), tiles_t(DA_HEADS, VT_ROWS),
        tiles_t(NSA_HEADS), nat(G * LANES), nat(G * LANES),
        tiles_t(G, VT_ROWS), tiles_t(G, VT_ROWS),
        tiles_t(FOX_HEADS, FOX_AUG), nat(FOX_HEADS * FOX_AUG), tiles_t(FOX_HEADS, VT_ROWS),
        grp(),
    ]
    return pl.pallas_call(
        functools.partial(_prep_kernel, tiles_per_seq=seq // tm),
        grid=(nt,),
        in_specs=[
            rows(QKV_COLS),
            rows(LANES),
            pl.BlockSpec((4, tm, LANES), lambda i: (0, i, 0)),
            pl.BlockSpec(gains.shape, lambda i: (0, 0)),
        ],
        out_specs=tuple(o[1] for o in outs),
        out_shape=tuple(o[0] for o in outs),
        scratch_shapes=[pltpu.VMEM((8, LANES), F32)],
        compiler_params=_cparams(("arbitrary",)),
        name="prep",
    )(p_qkv, small, rope_tab, gains)


def _compress_kernel(x_ref, pos_ref, w1_ref, w2_ref, gain_ref, kc_ref, vct_ref, *, batch):
    def mlp(kv):
        blk = (x_ref[kv].astype(F32) + pos_ref[kv]).astype(BF16)
        hid = jax.nn.gelu(_dot(blk, w1_ref[kv].astype(BF16)))
        return _dot(hid.astype(BF16), w2_ref[kv].astype(BF16))

    kc = _rms_rows(mlp(0), gain_ref[...])
    vc = mlp(1)
    nb = kc.shape[0] // batch
    pad = jnp.zeros((LANES - nb, LANES), F32)
    for b in range(batch):
        kc_ref[b] = jnp.concatenate([kc[b * nb:(b + 1) * nb], pad], axis=0).astype(BF16)
        vct_ref[b] = jnp.concatenate([vc[b * nb:(b + 1) * nb], pad], axis=0).T.astype(BF16)


def _compress(xkv, pos, w1, w2, gain, layer, *, batch):
    _, G, R, K = xkv.shape
    d = w1.shape[-1]
    out = jax.ShapeDtypeStruct((G, batch, LANES, LANES), BF16)
    ospec = pl.BlockSpec((None, batch, LANES, LANES), lambda g: (g, 0, 0, 0))
    return pl.pallas_call(
        functools.partial(_compress_kernel, batch=batch),
        grid=(G,),
        in_specs=[
            pl.BlockSpec((2, None, R, K), lambda g: (0, g, 0, 0)),
            pl.BlockSpec((None, 2, 1, K), lambda g: (layer, 0, 0, 0)),
            pl.BlockSpec((None, 2, K, d), lambda g: (layer, 0, 0, 0)),
            pl.BlockSpec((None, 2, d, d), lambda g: (layer, 0, 0, 0)),
            pl.BlockSpec((1, d), lambda g: (0, 0)),
        ],
        out_specs=(ospec, ospec),
        out_shape=(out, out),
        compiler_params=_cparams(("arbitrary",)),
        name="compress",
    )(xkv, pos, w1, w2, gain)


def _online_t(s_ts, mask, carry, v_ts):
    masks = mask if isinstance(mask, (list, tuple)) else [mask] * len(s_ts)
    stats = []
    for s_t, (m, _), mk in zip(s_ts, carry, masks):
        if mk is not None:
            s_t = jnp.where(mk, s_t, NEG_INF)
        m_new = jnp.maximum(m, jnp.max(s_t, axis=0, keepdims=True))
        stats.append((m_new, jnp.exp2(m - m_new), jnp.exp2(s_t - m_new).astype(BF16)))
    return tuple((m_new, alpha * acc_t + _dot(v_t, p))
                 for (m_new, alpha, p), (_, acc_t), v_t in zip(stats, carry, v_ts))


def _attend(scores, values, loop_mask, last_mask, kt0, kt_last, n):
    t = ATT_TILE
    init = tuple((jnp.full((1, t), NEG_INF, F32), jnp.zeros((VT_ROWS, t), F32)) for _ in range(n))

    def body(kt, carry):
        mask = None if loop_mask is None else loop_mask(kt)
        return _online_t(scores(kt), mask, carry, values(kt))

    carry = lax.fori_loop(kt0, kt_last, body, init)
    return _normalise(_online_t(scores(kt_last), last_mask, carry, values(kt_last)))


def _normalise(carry):
    return [acc[0:LANES] / acc[LANES:LANES + 1] for (_, acc) in carry]


def _causal_mask_t(t):
    return lax.broadcasted_iota(jnp.int32, (t, t), 0) <= lax.broadcasted_iota(jnp.int32, (t, t), 1)


def _diff_kernel(qt_ref, k_ref, vt_ref, lam_ref, gains_ref, o_ref, *, lam_init):
    i = pl.program_id(2)
    t = ATT_TILE
    nh = qt_ref.shape[0]
    drow = lax.broadcasted_iota(jnp.int32, (LANES, t), 0)
    qts = []
    for h in range(nh):
        q = qt_ref[h, 0]
        qts.append(jnp.where(drow < DA_QK_DIM, q, jnp.zeros_like(q)))
        qts.append(jnp.where(drow < DA_QK_DIM, jnp.zeros_like(q), q))

    def scores(kt):
        off = pl.multiple_of(kt * t, t)
        ks = [k_ref[pl.ds(off, t), h * LANES:(h + 1) * LANES] for h in range(nh)]
        return tuple(_dot(ks[c // 2], qts[c]) for c in range(2 * nh))

    def values(kt):
        return [vt_ref[c // 2, kt] for c in range(2 * nh)]

    outs = _attend(scores, values, None, _causal_mask_t(t), 0, i, 2 * nh)

    lp = lam_ref[...]
    lam = (jnp.exp(jnp.sum(lp[0:1] * lp[1:2], axis=-1, keepdims=True))
           - jnp.exp(jnp.sum(lp[2:3] * lp[3:4], axis=-1, keepdims=True)) + lam_init)
    for h in range(nh):
        o = (outs[2 * h] - lam * outs[2 * h + 1]).T
        o = _rms_rows(o, gains_ref[G_DAOUT:G_DAOUT + 1, :]) * (1.0 - lam_init)
        o_ref[:, h * LANES:(h + 1) * LANES] = o.astype(BF16)


def _diff_attention(qat, ka, vat, da_lambda, gains, *, batch, seq, lam_init, heads_per_step=4):
    t = ATT_TILE
    nq = seq // t
    T = batch * seq
    hp = heads_per_step
    return pl.pallas_call(
        functools.partial(_diff_kernel, lam_init=lam_init),
        grid=(batch, DA_HEADS // hp, nq),
        in_specs=[
            pl.BlockSpec((hp, 1, LANES, t), lambda b, p, i: (p, b * nq + i, 0, 0)),
            pl.BlockSpec((seq, hp * LANES), lambda b, p, i: (b, p)),
            pl.BlockSpec((hp, nq, VT_ROWS, t), lambda b, p, i: (p, b, 0, 0)),
            pl.BlockSpec(da_lambda.shape, lambda b, p, i: (0, 0)),
            pl.BlockSpec(gains.shape, lambda b, p, i: (0, 0)),
        ],
        out_specs=pl.BlockSpec((t, hp * LANES), lambda b, p, i: (b * nq + i, p)),
        out_shape=jax.ShapeDtypeStruct((T, DA_HEADS * LANES), BF16),
        compiler_params=_cparams(("parallel", "parallel", "arbitrary")),
        name="diff_attn",
    )(qat, ka, vat, da_lambda, gains)


def _fox_kernel(qt_ref, k_ref, vt_ref, o_ref):
    i = pl.program_id(1)
    t = ATT_TILE
    nh, bs = qt_ref.shape[0], qt_ref.shape[1]
    chains = [(b, h) for b in range(bs) for h in range(nh)]
    qts = [qt_ref[h, b, 0] for b, h in chains]

    def scores(kt):
        off = pl.multiple_of(kt * t, t)
        return tuple(_dot(k_ref[b, pl.ds(off, t), h * FOX_AUG:(h + 1) * FOX_AUG], q)
                     for (b, h), q in zip(chains, qts))

    def values(kt):
        return [vt_ref[h, b, kt] for b, h in chains]

    outs = _attend(scores, values, None, _causal_mask_t(t), 0, i, len(chains))
    for (b, h), o in zip(chains, outs):
        o_ref[b, :, h * LANES:(h + 1) * LANES] = o.T.astype(BF16)


def _fox_attention(qct, kf, vct, *, batch, seq):
    t = ATT_TILE
    nq = seq // t
    nh = FOX_HEADS
    bs = 2 if batch % 2 == 0 else 1
    nbp = batch // bs
    out = pl.pallas_call(
        _fox_kernel,
        grid=(nbp, nq),
        in_specs=[
            pl.BlockSpec((nh, None, bs, 1, FOX_AUG, t), lambda p, i: (0, p, 0, i, 0, 0)),
            pl.BlockSpec((None, bs, seq, nh * FOX_AUG), lambda p, i: (p, 0, 0, 0)),
            pl.BlockSpec((nh, None, bs, nq, VT_ROWS, t), lambda p, i: (0, p, 0, 0, 0, 0)),
        ],
        out_specs=pl.BlockSpec((None, bs, t, nh * LANES), lambda p, i: (p, 0, i, 0)),
        out_shape=jax.ShapeDtypeStruct((nbp, bs, seq, nh * LANES), BF16),
        compiler_params=_cparams(("parallel", "arbitrary")),
        name="fox_attn",
    )(qct.reshape(nh, nbp, bs, nq, FOX_AUG, t), kf.reshape(nbp, bs, seq, nh * FOX_AUG),
      vct.reshape(nh, nbp, bs, nq, VT_ROWS, t))
    return out.reshape(batch * seq, nh * LANES)


def _nsa_kernel(qt_ref, ks_ref, vst_ref, kw_ref, vwt_ref, kc_ref, vct_ref, gt_ref, o_ref, *, n_sel, nb):
    i = pl.program_id(1)
    t = ATT_TILE
    lo = i * t
    nh = qt_ref.shape[0]
    hpg = nh // NSA_GROUPS
    qts = [qt_ref[h, 0] for h in range(nh)]
    tq_row = lo + lax.broadcasted_iota(jnp.int32, (1, t), 1)
    nidx = lax.broadcasted_iota(jnp.int32, (nb, t), 0)
    nidx_f = nidx.astype(F32)
    cmask = nidx * NSA_BLOCK + (NSA_BLOCK - 1) <= tq_row
    cur = tq_row >> BLOCK_SHIFT
    forced = (nidx == 0) | (nidx == cur) | (nidx == cur - 1)
    zpad = jnp.zeros((LANES - nb, t), F32)

    def lane_tile(v):
        return jnp.concatenate([v, zpad], axis=0).astype(BF16)

    o_cmp, sel_b = [], []
    for g in range(NSA_GROUPS):
        kc, vct = kc_ref[g, 0:nb, :], vct_ref[g]
        importance = jnp.zeros((nb, t), F32)
        for h in range(g * hpg, (g + 1) * hpg):
            s = jnp.where(cmask, _dot(kc, qts[h]), NEG_INF)
            e = jnp.where(cmask, jnp.exp2(s - jnp.max(s, axis=0, keepdims=True)), 0.0)
            p = e / jnp.maximum(jnp.sum(e, axis=0, keepdims=True), 1e-30)
            importance = importance + p
            o_cmp.append(_dot(vct, lane_tile(p)))

        score = jnp.where(forced, NSA_FORCED_SCORE, jnp.where(nidx <= cur, importance, -1.0))
        sel = jnp.zeros((nb, t), F32)
        for _ in range(n_sel):
            best = jnp.max(score, axis=0, keepdims=True)
            first = jnp.min(jnp.where(score == best, nidx_f, float(nb)), axis=0, keepdims=True)
            hit = nidx_f == first
            sel = jnp.where(hit, 1.0, sel)
            score = jnp.where(hit, -3e38, score)
        sel_b.append(lane_tile(sel))

    e_key = lax.broadcasted_iota(jnp.int32, (t, LANES), 0) >> BLOCK_SHIFT
    e_blk = lax.broadcasted_iota(jnp.int32, (t, LANES), 1)
    krow = lax.broadcasted_iota(jnp.int32, (t, t), 0)
    qcol = lax.broadcasted_iota(jnp.int32, (t, t), 1)
    causal = krow <= qcol

    def scores_of(k_ref):
        def scores(kt):
            off = pl.multiple_of(kt * t, t)
            ks = [k_ref[pl.ds(off, t), g * LANES:(g + 1) * LANES] for g in range(NSA_GROUPS)]
            return tuple(_dot(ks[h // hpg], qts[h]) for h in range(nh))
        return scores

    def values_of(vt_ref):
        return lambda kt: [vt_ref[h // hpg, kt] for h in range(nh)]

    def selected(kt, extra=None):
        expand = jnp.where(e_blk == e_key + kt * (t // NSA_BLOCK), 1.0, 0.0).astype(BF16)
        per_group = [_dot(expand, sb) > 0.5 for sb in sel_b]
        if extra is not None:
            per_group = [m & extra for m in per_group]
        return [per_group[h // hpg] for h in range(nh)]

    o_slc = _attend(scores_of(ks_ref), values_of(vst_ref), selected, selected(i, causal), 0, i, nh)

    wt = NSA_WINDOW // t
    win_scores, win_values = scores_of(kw_ref), values_of(vwt_ref)
    upper = krow > qcol
    far = jnp.maximum(i - wt, 0)
    penalty = jnp.where(i >= wt, 0.0, NEG_INF)
    s_far, s_diag = win_scores(far), win_scores(i)
    stats = []
    for c in range(nh):
        s = jnp.where(upper, s_far[c] + penalty, s_diag[c])
        m = jnp.max(s, axis=0, keepdims=True)
        stats.append((m, jnp.exp2(s - m).astype(BF16)))
    v_far, v_diag = win_values(far), win_values(i)
    carry = tuple((m, _dot(v_far[c], jnp.where(upper, p, jnp.zeros_like(p)))
                   + _dot(v_diag[c], jnp.where(upper, jnp.zeros_like(p), p)))
                  for c, (m, p) in enumerate(stats))
    carry = lax.fori_loop(jnp.maximum(i - wt + 1, 0), i,
                          lambda kt, cr: _online_t(win_scores(kt), None, cr, win_values(kt)), carry)
    o_win = _normalise(carry)

    for h in range(nh):
        g, r = h // hpg, 3 * (h % hpg)
        gt = jax.nn.sigmoid(gt_ref[(1 + g) * LANES + r:(1 + g) * LANES + r + 3, :])
        o = gt[0:1] * o_cmp[h] + gt[1:2] * o_slc[h] + gt[2:3] * o_win[h]
        o_ref[:, h * LANES:(h + 1) * LANES] = o.T.astype(BF16)


def _nsa_attention(qbt, ks, kw, vst, vwt, kc, vct, small_t, *, batch, seq):
    t = ATT_TILE
    nq = seq // t
    T = batch * seq
    n_sel = min(NSA_TOP_N, seq // NSA_BLOCK)
    G = NSA_GROUPS
    k_spec = pl.BlockSpec((seq, G * LANES), lambda b, i: (b, 0))
    vt_spec = pl.BlockSpec((G, nq, VT_ROWS, t), lambda b, i: (0, b, 0, 0))
    c_spec = pl.BlockSpec((G, None, LANES, LANES), lambda b, i: (0, b, 0, 0))
    return pl.pallas_call(
        functools.partial(_nsa_kernel, n_sel=n_sel, nb=seq // NSA_BLOCK),
        grid=(batch, nq),
        in_specs=[
            pl.BlockSpec((NSA_HEADS, 1, LANES, t), lambda b, i: (0, b * nq + i, 0, 0)),
            k_spec, vt_spec, k_spec, vt_spec, c_spec, c_spec,
            pl.BlockSpec((SMALL_COLS, t), lambda b, i: (0, b * nq + i)),
        ],
        out_specs=pl.BlockSpec((t, NSA_HEADS * LANES), lambda b, i: (b * nq + i, 0)),
        out_shape=jax.ShapeDtypeStruct((T, NSA_HEADS * LANES), BF16),
        compiler_params=_cparams(("parallel", "arbitrary")),
        name="nsa_attn",
    )(qbt, ks, vst, kw, vwt, kc, vct, small_t)


def _merge_kernel(x_ref, oa_ref, ob_ref, oc_ref, g0_ref, g1_ref, g2_ref,
                  wa_ref, wb_ref, wc_ref, wo_ref, o_ref):
    y = (g0_ref[...].astype(F32) * _dot(oa_ref[...], wa_ref[...])
         + g1_ref[...].astype(F32) * _dot(ob_ref[...], wb_ref[...])
         + g2_ref[...].astype(F32) * _dot(oc_ref[...], wc_ref[...]))
    o_ref[...] = x_ref[...] + _dot(y.astype(BF16), wo_ref[...])


def _merge(x2d, oa, ob, oc, p_gm, wa, wb, wc, wo, layer, *, tm=256):
    T, D = x2d.shape

    def rows(w, c=0):
        return pl.BlockSpec((tm, w), lambda i: (i, c))

    def full(w):
        return pl.BlockSpec((None,) + w.shape[1:], lambda i: (layer, 0, 0))

    return pl.pallas_call(
        _merge_kernel,
        grid=(T // tm,),
        in_specs=[rows(D), rows(oa.shape[1]), rows(ob.shape[1]), rows(oc.shape[1]),
                  rows(D, 0), rows(D, 1), rows(D, 2), full(wa), full(wb), full(wc), full(wo)],
        out_specs=rows(D),
        out_shape=jax.ShapeDtypeStruct((T, D), F32),
        compiler_params=_cparams(("parallel",)),
        name="merge",
    )(x2d, oa, ob, oc, p_gm, p_gm, p_gm, wa, wb, wc, wo)


def _rope_tables(positions, width, rot_dim):
    inv_freq = ROPE_THETA ** (-jnp.arange(0, rot_dim, 2, dtype=F32) / rot_dim)
    ang = positions.astype(F32).reshape(-1, 1) * inv_freq
    cos, sin = jnp.cos(ang), jnp.sin(ang)
    T = ang.shape[0]
    rest = width - rot_dim
    c = jnp.concatenate([cos, cos, jnp.ones((T, rest), F32)], axis=1)
    s = jnp.concatenate([-sin, sin, jnp.zeros((T, rest), F32)], axis=1)
    rep = LANES // width
    return [jnp.tile(t, (1, rep)) for t in (c, s)]


def _pad_lanes(v):
    return jnp.pad(v.astype(F32), (0, LANES - v.shape[0]))


def kernel(x, positions, ffn1_norm, ffn1_w_gate, ffn1_w_up, ffn1_w_down, mix_norm, w_in, da_q_norm, da_k_norm, da_lambda, da_out_norm, nsa_q_norm, nsa_k_norm, nsa_cmp_pos, nsa_cmp_w1, nsa_cmp_w2, fox_q_norm, fox_k_norm, fox_f_bias, w_branch_a, w_branch_b, w_branch_c, w_out, ffn2_norm, ffn2_w_gate, ffn2_w_up, ffn2_w_down):
    B, S, D = x.shape
    depth = w_in.shape[0]
    T = B * S
    G, L, d = NSA_GROUPS, NSA_BLOCK, HEAD_DIM
    nb = S // L
    assert S % ATT_TILE == 0 and nb <= LANES and nb % 8 == 0 and NSA_WINDOW % ATT_TILE == 0

    rope_tab = jnp.stack(_rope_tables(positions, DA_QK_DIM, DA_QK_DIM // ROPE_FRACTION)
                         + _rope_tables(positions, HEAD_DIM, HEAD_DIM // ROPE_FRACTION))

    widths = [512, 512, 512, 1024, 1536, 24, 512, 512, 512, 4, 3 * D]
    offs = [0]
    for w in widths:
        offs.append(offs[-1] + w)
    (o_aq, o_ak, o_av, o_bq, o_bkv, o_bg, o_cq, o_ck, o_cv, o_cf, o_gm, o_end) = offs

    w_qkv = jnp.concatenate([w_in[:, :, o_aq:o_bg], w_in[:, :, o_cq:o_cf]], axis=2).astype(BF16)
    w_gm = w_in[:, :, o_gm:o_end].astype(BF16)
    zpad = lambda n: jnp.zeros((depth, D, n), F32)
    w_small = jnp.concatenate([
        w_in[:, :, o_cf:o_gm], zpad(LANES - FOX_HEADS),
        w_in[:, :, o_bg:o_bg + 12], zpad(LANES - 12),
        w_in[:, :, o_bg + 12:o_bg + 24], zpad(LANES - 12)], axis=2)
    w_small_hi = w_small.astype(BF16)
    w_small_lo = (w_small - w_small_hi.astype(F32)).astype(BF16)
    ffn_f32 = [(ffn1_w_gate, ffn1_w_up, ffn1_w_down), (ffn2_w_gate, ffn2_w_up, ffn2_w_down)]
    ffn_g = [ffn1_norm.reshape(depth, 1, D), ffn2_norm.reshape(depth, 1, D)]
    ffn_next = [(k // 2, *ffn_f32[k % 2]) for k in range(1, 2 * depth)] + [None]
    ffn_w = tuple(_to_bf16(w, rows=256, layer=0) for w in ffn_f32[0])
    g_mix = mix_norm.reshape(depth, 1, D)
    w_a, w_b, w_c, w_o = (_to_bf16(w, rows=512) for w in (w_branch_a, w_branch_b, w_branch_c, w_out))
    cmp_pos = nsa_cmp_pos.reshape(depth, 2, 1, L * d)

    xc = x.reshape(T, D)
    for l in range(depth):
        lam_init = 0.8 - 0.6 * math.exp(-0.3 * l)
        gains = jnp.stack([
            jnp.tile(da_q_norm[l], 2), jnp.tile(da_k_norm[l], 2), nsa_q_norm[l],
            nsa_k_norm[l, 0], nsa_k_norm[l, 1], nsa_k_norm[l, 2], fox_q_norm[l], fox_k_norm[l],
            _pad_lanes(fox_f_bias[l]), da_out_norm[l]] + [jnp.zeros((LANES,), F32)] * 6).astype(F32)

        xc, ffn_w = _ffn(xc, ffn_g[0], *ffn_w, l, ffn_next[2 * l])

        p_qkv = _proj(xc, g_mix, w_qkv, l, gate=False, tn=QKV_COLS // 4)
        p_gm = _proj(xc, g_mix, w_gm, l, gate=True, tn=D // 2)
        small, small_t = _proj_small(xc, g_mix, w_small_hi, w_small_lo, l)

        (qat, ka, vat, qbt, ks, kw, vst, vwt, qct, kf, vct, kv_cmp) = _prep(
            p_qkv, small, rope_tab, gains, seq=S)

        kc, vcmp_t = _compress(kv_cmp.reshape(2, G, B * nb, L * d), cmp_pos, nsa_cmp_w1, nsa_cmp_w2,
                               nsa_k_norm[l, 0].reshape(1, d), l, batch=B)

        o_a = _diff_attention(qat, ka, vat, da_lambda[l], gains, batch=B, seq=S, lam_init=lam_init)
        o_b = _nsa_attention(qbt, ks, kw, vst, vwt, kc, vcmp_t, small_t, batch=B, seq=S)
        o_c = _fox_attention(qct, kf, vct, batch=B, seq=S)

        xc = _merge(xc, o_a, o_b, o_c, p_gm, w_a, w_b, w_c, w_o, l)

        xc, ffn_w = _ffn(xc, ffn_g[1], *ffn_w, l, ffn_next[2 * l + 1])
    return xc.reshape(B, S, D)
```

```python
import functools
import math

import jax
import jax.numpy as jnp
from jax import lax
from jax.experimental import pallas as pl
from jax.experimental.pallas import tpu as pltpu

F32 = jnp.float32
BF16 = jnp.bfloat16

HEAD_DIM = 128
LANES = 128
ROPE_THETA = 500000.0
ROPE_FRACTION = 4
EPS = 1e-6
NEG_INF = -1e30
LOG2E = math.log2(math.e)

DA_HEADS = 4
DA_QK_DIM = 64
NSA_HEADS = 8
NSA_GROUPS = 2
NSA_HPG = NSA_HEADS // NSA_GROUPS
NSA_BLOCK = 64
BLOCK_SHIFT = 6
NSA_TOP_N = 8
NSA_WINDOW = 512
NSA_FORCED_SCORE = 1e4
FOX_HEADS = 4

ATT_TILE = 256
FOX_AUG = 2 * LANES
VT_ROWS = LANES + 16

BLK_AQ, BLK_AK, BLK_AV, BLK_BQ, BLK_BKV, BLK_CQ, BLK_CK, BLK_CV = 0, 4, 8, 12, 20, 32, 36, 40
QKV_COLS = 44 * LANES
SMALL_COLS = 3 * LANES

G_DAQ, G_DAK, G_NQ, G_NK0, G_NK1, G_NK2, G_FQ, G_FK, G_FBIAS, G_DAOUT = range(10)

VMEM_BYTES_V7X = 64 * 1024 * 1024
VMEM_LIMIT = VMEM_BYTES_V7X - 8 * 1024 * 1024
VMEM_LIMIT_FFN = VMEM_BYTES_V7X - 3 * 1024 * 1024


def _cparams(sem, limit=VMEM_LIMIT):
    return pltpu.CompilerParams(dimension_semantics=sem, vmem_limit_bytes=limit)


def _dot(a, b):
    return jnp.dot(a, b, preferred_element_type=F32)


def _dot_t(a, b):
    return lax.dot_general(a, b, (((1,), (1,)), ((), ())), preferred_element_type=F32)


def _rms_rows(xf, gain):
    return xf * lax.rsqrt(jnp.mean(xf * xf, axis=-1, keepdims=True) + EPS) * gain


def _ffn_kernel(x_ref, g_ref, wg_ref, wu_ref, wd_ref, *rest):
    n_cast = (len(rest) - 2) // 2
    o_ref, h_scr = rest[n_cast], rest[-1]

    @pl.when(pl.program_id(1) == 0)
    def _():
        xf = x_ref[...]
        h_scr[...] = _rms_rows(xf, g_ref[...]).astype(BF16)
        o_ref[...] = xf

    h = h_scr[...]
    g = _dot(h, wg_ref[...])
    u = _dot(h, wu_ref[...])
    a = (g * jax.nn.sigmoid(g)) * (0.5 * u)
    o_ref[...] += _dot(a.astype(BF16), wd_ref[...])

    for src, dst in zip(rest[:n_cast], rest[n_cast + 1:-1]):
        dst[...] = src[...].astype(BF16)


def _ffn(x2d, gain, wg, wu, wd, layer, cast_next=None, *, tm=1024, tf=512):
    T, D = x2d.shape
    F = wg.shape[2]
    tm = min(tm, T)
    ni, nj = T // tm, F // tf
    in_specs = [
        pl.BlockSpec((tm, D), lambda i, j: (i, 0)),
        pl.BlockSpec((None, 1, D), lambda i, j: (layer, 0, 0)),
        pl.BlockSpec((None, D, tf), lambda i, j: (0, 0, j)),
        pl.BlockSpec((None, D, tf), lambda i, j: (0, 0, j)),
        pl.BlockSpec((None, tf, D), lambda i, j: (0, j, 0)),
    ]
    out_specs = [pl.BlockSpec((tm, D), lambda i, j: (i, 0))]
    out_shape = [jax.ShapeDtypeStruct((T, D), F32)]
    args = [x2d, gain, wg, wu, wd]
    if cast_next is not None:
        nl, *nws = cast_next
        for w in nws:
            _, R, C = w.shape
            blk, imap = (((R // ni, C // nj), lambda i, j: (nl, i, j)) if C == F
                         else ((R // nj, C // ni), lambda i, j: (nl, j, i)))
            in_specs.append(pl.BlockSpec((None,) + blk, imap))
            omap = (lambda i, j: (0, i, j)) if C == F else (lambda i, j: (0, j, i))
            out_specs.append(pl.BlockSpec((None,) + blk, omap))
            out_shape.append(jax.ShapeDtypeStruct((1, R, C), BF16))
            args.append(w)
    outs = pl.pallas_call(
        _ffn_kernel,
        grid=(ni, nj),
        in_specs=in_specs,
        out_specs=tuple(out_specs),
        out_shape=tuple(out_shape),
        scratch_shapes=[pltpu.VMEM((tm, D), BF16)],
        compiler_params=_cparams(("parallel", "arbitrary"), VMEM_LIMIT_FFN),
        name="ffn",
    )(*args)
    return outs[0], tuple(outs[1:])


def _cast_kernel(x_ref, o_ref):
    o_ref[...] = x_ref[...].astype(BF16)


def _to_bf16(w, *, rows, layer=None):
    depth, R, C = w.shape
    first, count = (0, depth) if layer is None else (layer, 1)
    return pl.pallas_call(
        _cast_kernel,
        grid=(count, R // rows),
        in_specs=[pl.BlockSpec((None, rows, C), lambda l, i: (first + l, i, 0))],
        out_specs=pl.BlockSpec((None, rows, C), lambda l, i: (l, i, 0)),
        out_shape=jax.ShapeDtypeStruct((count, R, C), BF16),
        compiler_params=_cparams(("parallel", "parallel")),
        name="cast_bf16",
    )(w)


def _proj_kernel(x_ref, g_ref, w_ref, o_ref, h_scr, *, gate):
    @pl.when(pl.program_id(1) == 0)
    def _():
        h_scr[...] = _rms_rows(x_ref[...], g_ref[...]).astype(BF16)

    r = _dot(h_scr[...], w_ref[...])
    o_ref[...] = (jax.nn.sigmoid(r) if gate else r).astype(o_ref.dtype)


def _proj(x2d, gain, w, layer, *, gate, tm=1024, tn):
    T, D = x2d.shape
    N = w.shape[2]
    tm = min(tm, T)
    return pl.pallas_call(
        functools.partial(_proj_kernel, gate=gate),
        grid=(T // tm, N // tn),
        in_specs=[
            pl.BlockSpec((tm, D), lambda i, j: (i, 0)),
            pl.BlockSpec((None, 1, D), lambda i, j: (layer, 0, 0)),
            pl.BlockSpec((None, D, tn), lambda i, j: (layer, 0, j)),
        ],
        out_specs=pl.BlockSpec((tm, tn), lambda i, j: (i, j)),
        out_shape=jax.ShapeDtypeStruct((T, N), BF16),
        scratch_shapes=[pltpu.VMEM((tm, D), BF16)],
        compiler_params=_cparams(("parallel", "arbitrary")),
        name="proj_gate" if gate else "proj_qkv",
    )(x2d, gain, w)


def _split3(v):
    hi = v.astype(BF16)
    r = v - hi.astype(F32)
    mid = r.astype(BF16)
    lo = (r - mid.astype(F32)).astype(BF16)
    return hi, mid, lo


def _proj_small_kernel(x_ref, g_ref, whi_ref, wlo_ref, o_ref, ot_ref):
    h = _rms_rows(x_ref[...], g_ref[...])
    hi, mid, _ = _split3(h)
    whi = whi_ref[...]
    r = _dot(hi, whi) + (_dot(mid, whi) + _dot(hi, wlo_ref[...]))
    o_ref[...] = r
    for c in range(r.shape[1] // LANES):
        ot_ref[c * LANES:(c + 1) * LANES, :] = r[:, c * LANES:(c + 1) * LANES].T


def _proj_small(x2d, gain, whi, wlo, layer, *, tm=512):
    T, D = x2d.shape
    N = whi.shape[2]
    return pl.pallas_call(
        _proj_small_kernel,
        grid=(T // tm,),
        in_specs=[
            pl.BlockSpec((tm, D), lambda i: (i, 0)),
            pl.BlockSpec((None, 1, D), lambda i: (layer, 0, 0)),
            pl.BlockSpec((None, D, N), lambda i: (layer, 0, 0)),
            pl.BlockSpec((None, D, N), lambda i: (layer, 0, 0)),
        ],
        out_specs=(pl.BlockSpec((tm, N), lambda i: (i, 0)), pl.BlockSpec((N, tm), lambda i: (0, i))),
        out_shape=(jax.ShapeDtypeStruct((T, N), F32), jax.ShapeDtypeStruct((N, T), F32)),
        compiler_params=_cparams(("parallel",)),
        name="proj_small",
    )(x2d, gain, whi, wlo)


def _prep_kernel(p_ref, sm_ref, rope_ref, gains_ref,
                 qat_ref, ka_ref, vat_ref, qbt_ref, ks_ref, kw_ref, vst_ref, vwt_ref,
                 qct_ref, kf_ref, vct_ref, kvc_ref,
                 carry_scr, *, tiles_per_seq):
    i = pl.program_id(0)
    tm = p_ref.shape[0]
    lane = lax.broadcasted_iota(jnp.int32, (tm, LANES), 1)
    src = lax.broadcasted_iota(jnp.int32, (LANES, LANES), 0)
    dst = lax.broadcasted_iota(jnp.int32, (LANES, LANES), 1)

    def const(cond, val=1.0):
        return jnp.where(cond, val, 0.0).astype(BF16)

    def group_mean(width):
        shift = width.bit_length() - 1
        return const((src >> shift) == (dst >> shift), 1.0 / width)

    def swap_halves(width, half):
        d = dst & (width - 1)
        return const(((d < half) & (src == dst + half)) | ((d >= half) & (d < 2 * half) & (src == dst - half)))

    mean128, mean64 = group_mean(HEAD_DIM), group_mean(DA_QK_DIM)
    eye = const(src == dst)

    def dot1(v, mat):
        return _dot(v.astype(BF16), mat)

    def raw(c):
        return p_ref[:, c * LANES:(c + 1) * LANES]

    def chunk(c):
        return raw(c).astype(F32)

    def gain(r):
        return gains_ref[r:r + 1, :]

    def rope_tab(r):
        return rope_ref[r]


    def tr(y):
        return _dot_t(eye, y.astype(BF16)).astype(BF16)

    def put_vt(ref, n, c):
        ref[n, 0, 0:LANES, :] = tr(raw(c))
        ref[n, 0, LANES:VT_ROWS, :] = jnp.ones((VT_ROWS - LANES, tm), BF16)

    half_a = swap_halves(DA_QK_DIM, DA_QK_DIM // ROPE_FRACTION // 2)
    half_b = swap_halves(HEAD_DIM, HEAD_DIM // ROPE_FRACTION // 2)
    scale_a = DA_QK_DIM ** -0.5 * LOG2E
    scale_b = HEAD_DIM ** -0.5 * LOG2E

    def run_group(specs):
        xs = [chunk(c) for c, *_ in specs]
        ms = [None if nm is None else dot1(x * x, nm[1]) for x, (_, nm, *_) in zip(xs, specs)]
        ys = [x if nm is None else x * lax.rsqrt(m + EPS) * gain(nm[0])
              for x, m, (_, nm, *_) in zip(xs, ms, specs)]
        ps = [None if rp is None else dot1(y, rp[1]) for y, (_, _, rp, *_) in zip(ys, specs)]
        rs = [y if rp is None else y * rope_tab(rp[0]) + p * rope_tab(rp[0] + 1)
              for y, p, (_, _, rp, *_) in zip(ys, ps, specs)]
        rs = [r if sc == 1.0 else r * sc for r, (_, _, _, sc, _, _) in zip(rs, specs)]
        outs = [tr(r) if tp else r.astype(BF16) for r, (_, _, _, _, tp, _) in zip(rs, specs)]
        for o, spec in zip(outs, specs):
            spec[-1](o)

    def nat_sink(ref, lo):
        def sink(o):
            ref[:, lo:lo + LANES] = o
        return sink

    def tile_sink(ref, n, rows=slice(None)):
        def sink(o):
            ref[n, 0, rows, :] = o
        return sink

    def grp_sink(n, g):
        def sink(o):
            kvc_ref[n, g] = o
        return sink

    rope_a, rope_b = (0, half_a), (2, half_b)
    run_group([(BLK_AQ + h, (G_DAQ, mean64), rope_a, scale_a, True, tile_sink(qat_ref, h))
               for h in range(DA_HEADS)]
              + [(BLK_AK + h, (G_DAK, mean64), rope_a, 1.0, False, nat_sink(ka_ref, h * LANES))
                 for h in range(DA_HEADS)])
    run_group([(BLK_BQ + h, (G_NQ, mean128), rope_b, scale_b, True, tile_sink(qbt_ref, h))
               for h in range(NSA_HEADS)])
    run_group([(BLK_BKV + 0 + g, None, rope_b, 1.0, False, grp_sink(0, g)) for g in range(NSA_GROUPS)]
              + [(BLK_BKV + 4 + g, (G_NK1, mean128), rope_b, 1.0, False, nat_sink(ks_ref, g * LANES))
                 for g in range(NSA_GROUPS)]
              + [(BLK_BKV + 8 + g, (G_NK2, mean128), rope_b, 1.0, False, nat_sink(kw_ref, g * LANES))
                 for g in range(NSA_GROUPS)])
    run_group([(BLK_CQ + h, (G_FQ, mean128), None, scale_b, True, tile_sink(qct_ref, h, slice(0, LANES)))
               for h in range(FOX_HEADS)]
              + [(BLK_CK + h, (G_FK, mean128), None, 1.0, False, nat_sink(kf_ref, h * FOX_AUG))
                 for h in range(FOX_HEADS)])
    for g in range(NSA_GROUPS):
        kvc_ref[1, g] = raw(BLK_BKV + 2 + g)
        put_vt(vst_ref, g, BLK_BKV + 6 + g)
        put_vt(vwt_ref, g, BLK_BKV + 10 + g)
    for h in range(DA_HEADS):
        put_vt(vat_ref, h, BLK_AV + h)
    for h in range(FOX_HEADS):
        put_vt(vct_ref, h, BLK_CV + h)

    @pl.when(i % tiles_per_seq == 0)
    def _():
        carry_scr[...] = jnp.zeros_like(carry_scr)

    z = sm_ref[...] + gain(G_FBIAS)
    logf = jnp.minimum(z, 0.0) - jnp.log(1.0 + jnp.exp(-jnp.abs(z)))
    r_id = lax.broadcasted_iota(jnp.int32, (tm, tm), 0)
    c_id = lax.broadcasted_iota(jnp.int32, (tm, tm), 1)
    tri = jnp.where(r_id >= c_id, 1.0, 0.0).astype(BF16)
    hi, mid, lo = _split3(logf)
    cum = _dot(tri, hi) + (_dot(tri, mid) + _dot(tri, lo)) + carry_scr[0:1, :]
    carry_scr[0:1, :] = cum[tm - 1:tm, :]

    for h in range(FOX_HEADS):
        c = jnp.broadcast_to(cum[:, h:h + 1], (tm, LANES)) * LOG2E
        c_hi, c_mid, c_lo = (v.astype(F32) for v in _split3(c))
        ones = jnp.where(lane < 6, 1.0, 0.0)
        q_aug = jnp.where(lane == 0, c_hi, jnp.where(lane == 1, c_mid, jnp.where(lane == 2, c_lo, ones)))
        k_aug = jnp.where(lane == 3, -c_hi, jnp.where(lane == 4, -c_mid, jnp.where(lane == 5, -c_lo, ones)))
        qct_ref[h, 0, LANES:FOX_AUG, :] = tr(q_aug)
        kf_ref[:, h * FOX_AUG + LANES:(h + 1) * FOX_AUG] = k_aug.astype(BF16)


def _prep(p_qkv, small, rope_tab, gains, *, seq):
    tm = ATT_TILE
    T = p_qkv.shape[0]
    nt = T // tm
    G = NSA_GROUPS

    def rows(w):
        return pl.BlockSpec((tm, w), lambda i: (i, 0))

    def tiles_t(n, d=LANES):
        return (jax.ShapeDtypeStruct((n, nt, d, tm), BF16), pl.BlockSpec((n, 1, d, tm), lambda i: (0, i, 0, 0)))

    def nat(w):
        return (jax.ShapeDtypeStruct((T, w), BF16), rows(w))

    def grp():
        return (jax.ShapeDtypeStruct((2, G, T, LANES), BF16),
                pl.BlockSpec((2, G, tm, LANES), lambda i: (0, 0, i, 0)))

    outs = [
        tiles_t(DA_HEADS), nat(DA_HEADS * LANES), tiles_t(DA_HEADS, VT_ROWS),
        tiles_t(NSA_HEADS), nat(G * LANES), nat(G * LANES),
        tiles_t(G, VT_ROWS), tiles_t(G, VT_ROWS),
        tiles_t(FOX_HEADS, FOX_AUG), nat(FOX_HEADS * FOX_AUG), tiles_t(FOX_HEADS, VT_ROWS),
        grp(),
    ]
    return pl.pallas_call(
        functools.partial(_prep_kernel, tiles_per_seq=seq // tm),
        grid=(nt,),
        in_specs=[
            rows(QKV_COLS),
            rows(LANES),
            pl.BlockSpec((4, tm, LANES), lambda i: (0, i, 0)),
            pl.BlockSpec(gains.shape, lambda i: (0, 0)),
        ],
        out_specs=tuple(o[1] for o in outs),
        out_shape=tuple(o[0] for o in outs),
        scratch_shapes=[pltpu.VMEM((8, LANES), F32)],
        compiler_params=_cparams(("arbitrary",)),
        name="prep",
    )(p_qkv, small, rope_tab, gains)


def _compress_kernel(x_ref, pos_ref, w1_ref, w2_ref, gain_ref, kc_ref, vct_ref, *, batch):
    def mlp(kv):
        blk = (x_ref[kv].astype(F32) + pos_ref[kv]).astype(BF16)
        hid = jax.nn.gelu(_dot(blk, w1_ref[kv].astype(BF16)))
        return _dot(hid.astype(BF16), w2_ref[kv].astype(BF16))

    kc = _rms_rows(mlp(0), gain_ref[...])
    vc = mlp(1)
    nb = kc.shape[0] // batch
    pad = jnp.zeros((LANES - nb, LANES), F32)
    for b in range(batch):
        kc_ref[b] = jnp.concatenate([kc[b * nb:(b + 1) * nb], pad], axis=0).astype(BF16)
        vct_ref[b] = jnp.concatenate([vc[b * nb:(b + 1) * nb], pad], axis=0).T.astype(BF16)


def _compress(xkv, pos, w1, w2, gain, layer, *, batch):
    _, G, R, K = xkv.shape
    d = w1.shape[-1]
    out = jax.ShapeDtypeStruct((G, batch, LANES, LANES), BF16)
    ospec = pl.BlockSpec((None, batch, LANES, LANES), lambda g: (g, 0, 0, 0))
    return pl.pallas_call(
        functools.partial(_compress_kernel, batch=batch),
        grid=(G,),
        in_specs=[
            pl.BlockSpec((2, None, R, K), lambda g: (0, g, 0, 0)),
            pl.BlockSpec((None, 2, 1, K), lambda g: (layer, 0, 0, 0)),
            pl.BlockSpec((None, 2, K, d), lambda g: (layer, 0, 0, 0)),
            pl.BlockSpec((None, 2, d, d), lambda g: (layer, 0, 0, 0)),
            pl.BlockSpec((1, d), lambda g: (0, 0)),
        ],
        out_specs=(ospec, ospec),
        out_shape=(out, out),
        compiler_params=_cparams(("arbitrary",)),
        name="compress",
    )(xkv, pos, w1, w2, gain)


def _online_t(s_ts, mask, carry, v_ts):
    masks = mask if isinstance(mask, (list, tuple)) else [mask] * len(s_ts)
    stats = []
    for s_t, (m, _), mk in zip(s_ts, carry, masks):
        if mk is not None:
            s_t = jnp.where(mk, s_t, NEG_INF)
        m_new = jnp.maximum(m, jnp.max(s_t, axis=0, keepdims=True))
        stats.append((m_new, jnp.exp2(m - m_new), jnp.exp2(s_t - m_new).astype(BF16)))
    return tuple((m_new, alpha * acc_t + _dot(v_t, p))
                 for (m_new, alpha, p), (_, acc_t), v_t in zip(stats, carry, v_ts))


def _attend(scores, values, loop_mask, last_mask, kt0, kt_last, n):
    t = ATT_TILE
    init = tuple((jnp.full((1, t), NEG_INF, F32), jnp.zeros((VT_ROWS, t), F32)) for _ in range(n))

    def body(kt, carry):
        mask = None if loop_mask is None else loop_mask(kt)
        return _online_t(scores(kt), mask, carry, values(kt))

    carry = lax.fori_loop(kt0, kt_last, body, init)
    return _normalise(_online_t(scores(kt_last), last_mask, carry, values(kt_last)))


def _normalise(carry):
    return [acc[0:LANES] / acc[LANES:LANES + 1] for (_, acc) in carry]


def _causal_mask_t(t):
    return lax.broadcasted_iota(jnp.int32, (t, t), 0) <= lax.broadcasted_iota(jnp.int32, (t, t), 1)


def _diff_kernel(qt_ref, k_ref, vt_ref, lam_ref, gains_ref, o_ref, *, lam_init):
    i = pl.program_id(2)
    t = ATT_TILE
    nh = qt_ref.shape[0]
    drow = lax.broadcasted_iota(jnp.int32, (LANES, t), 0)
    qts = []
    for h in range(nh):
        q = qt_ref[h, 0]
        qts.append(jnp.where(drow < DA_QK_DIM, q, jnp.zeros_like(q)))
        qts.append(jnp.where(drow < DA_QK_DIM, jnp.zeros_like(q), q))

    def scores(kt):
        off = pl.multiple_of(kt * t, t)
        ks = [k_ref[pl.ds(off, t), h * LANES:(h + 1) * LANES] for h in range(nh)]
        return tuple(_dot(ks[c // 2], qts[c]) for c in range(2 * nh))

    def values(kt):
        return [vt_ref[c // 2, kt] for c in range(2 * nh)]

    outs = _attend(scores, values, None, _causal_mask_t(t), 0, i, 2 * nh)

    lp = lam_ref[...]
    lam = (jnp.exp(jnp.sum(lp[0:1] * lp[1:2], axis=-1, keepdims=True))
           - jnp.exp(jnp.sum(lp[2:3] * lp[3:4], axis=-1, keepdims=True)) + lam_init)
    for h in range(nh):
        o = (outs[2 * h] - lam * outs[2 * h + 1]).T
        o = _rms_rows(o, gains_ref[G_DAOUT:G_DAOUT + 1, :]) * (1.0 - lam_init)
        o_ref[:, h * LANES:(h + 1) * LANES] = o.astype(BF16)


def _diff_attention(qat, ka, vat, da_lambda, gains, *, batch, seq, lam_init, heads_per_step=4):
    t = ATT_TILE
    nq = seq // t
    T = batch * seq
    hp = heads_per_step
    return pl.pallas_call(
        functools.partial(_diff_kernel, lam_init=lam_init),
        grid=(batch, DA_HEADS // hp, nq),
        in_specs=[
            pl.BlockSpec((hp, 1, LANES, t), lambda b, p, i: (p, b * nq + i, 0, 0)),
            pl.BlockSpec((seq, hp * LANES), lambda b, p, i: (b, p)),
            pl.BlockSpec((hp, nq, VT_ROWS, t), lambda b, p, i: (p, b, 0, 0)),
            pl.BlockSpec(da_lambda.shape, lambda b, p, i: (0, 0)),
            pl.BlockSpec(gains.shape, lambda b, p, i: (0, 0)),
        ],
        out_specs=pl.BlockSpec((t, hp * LANES), lambda b, p, i: (b * nq + i, p)),
        out_shape=jax.ShapeDtypeStruct((T, DA_HEADS * LANES), BF16),
        compiler_params=_cparams(("parallel", "parallel", "arbitrary")),
        name="diff_attn",
    )(qat, ka, vat, da_lambda, gains)


def _fox_kernel(qt_ref, k_ref, vt_ref, o_ref):
    i = pl.program_id(1)
    t = ATT_TILE
    nh, bs = qt_ref.shape[0], qt_ref.shape[1]
    chains = [(b, h) for b in range(bs) for h in range(nh)]
    qts = [qt_ref[h, b, 0] for b, h in chains]

    def scores(kt):
        off = pl.multiple_of(kt * t, t)
        return tuple(_dot(k_ref[b, pl.ds(off, t), h * FOX_AUG:(h + 1) * FOX_AUG], q)
                     for (b, h), q in zip(chains, qts))

    def values(kt):
        return [vt_ref[h, b, kt] for b, h in chains]

    outs = _attend(scores, values, None, _causal_mask_t(t), 0, i, len(chains))
    for (b, h), o in zip(chains, outs):
        o_ref[b, :, h * LANES:(h + 1) * LANES] = o.T.astype(BF16)


def _fox_attention(qct, kf, vct, *, batch, seq):
    t = ATT_TILE
    nq = seq // t
    nh = FOX_HEADS
    bs = 2 if batch % 2 == 0 else 1
    nbp = batch // bs
    out = pl.pallas_call(
        _fox_kernel,
        grid=(nbp, nq),
        in_specs=[
            pl.BlockSpec((nh, None, bs, 1, FOX_AUG, t), lambda p, i: (0, p, 0, i, 0, 0)),
            pl.BlockSpec((None, bs, seq, nh * FOX_AUG), lambda p, i: (p, 0, 0, 0)),
            pl.BlockSpec((nh, None, bs, nq, VT_ROWS, t), lambda p, i: (0, p, 0, 0, 0, 0)),
        ],
        out_specs=pl.BlockSpec((None, bs, t, nh * LANES), lambda p, i: (p, 0, i, 0)),
        out_shape=jax.ShapeDtypeStruct((nbp, bs, seq, nh * LANES), BF16),
        compiler_params=_cparams(("parallel", "arbitrary")),
        name="fox_attn",
    )(qct.reshape(nh, nbp, bs, nq, FOX_AUG, t), kf.reshape(nbp, bs, seq, nh * FOX_AUG),
      vct.reshape(nh, nbp, bs, nq, VT_ROWS, t))
    return out.reshape(batch * seq, nh * LANES)


def _nsa_kernel(qt_ref, ks_ref, vst_ref, kw_ref, vwt_ref, kc_ref, vct_ref, gt_ref, o_ref, *, n_sel, nb):
    i = pl.program_id(1)
    t = ATT_TILE
    lo = i * t
    nh = qt_ref.shape[0]
    hpg = nh // NSA_GROUPS
    qts = [qt_ref[h, 0] for h in range(nh)]
    tq_row = lo + lax.broadcasted_iota(jnp.int32, (1, t), 1)
    nidx = lax.broadcasted_iota(jnp.int32, (nb, t), 0)
    nidx_f = nidx.astype(F32)
    cmask = nidx * NSA_BLOCK + (NSA_BLOCK - 1) <= tq_row
    cur = tq_row >> BLOCK_SHIFT
    forced = (nidx == 0) | (nidx == cur) | (nidx == cur - 1)
    zpad = jnp.zeros((LANES - nb, t), F32)

    def lane_tile(v):
        return jnp.concatenate([v, zpad], axis=0).astype(BF16)

    o_cmp, sel_b = [], []
    for g in range(NSA_GROUPS):
        kc, vct = kc_ref[g, 0:nb, :], vct_ref[g]
        importance = jnp.zeros((nb, t), F32)
        for h in range(g * hpg, (g + 1) * hpg):
            s = jnp.where(cmask, _dot(kc, qts[h]), NEG_INF)
            e = jnp.where(cmask, jnp.exp2(s - jnp.max(s, axis=0, keepdims=True)), 0.0)
            p = e / jnp.maximum(jnp.sum(e, axis=0, keepdims=True), 1e-30)
            importance = importance + p
            o_cmp.append(_dot(vct, lane_tile(p)))

        score = jnp.where(forced, NSA_FORCED_SCORE, jnp.where(nidx <= cur, importance, -1.0))
        sel = jnp.zeros((nb, t), F32)
        for _ in range(n_sel):
            best = jnp.max(score, axis=0, keepdims=True)
            first = jnp.min(jnp.where(score == best, nidx_f, float(nb)), axis=0, keepdims=True)
            hit = nidx_f == first
            sel = jnp.where(hit, 1.0, sel)
            score = jnp.where(hit, -3e38, score)
        sel_b.append(lane_tile(sel))

    e_key = lax.broadcasted_iota(jnp.int32, (t, LANES), 0) >> BLOCK_SHIFT
    e_blk = lax.broadcasted_iota(jnp.int32, (t, LANES), 1)
    krow = lax.broadcasted_iota(jnp.int32, (t, t), 0)
    qcol = lax.broadcasted_iota(jnp.int32, (t, t), 1)
    causal = krow <= qcol

    def scores_of(k_ref):
        def scores(kt):
            off = pl.multiple_of(kt * t, t)
            ks = [k_ref[pl.ds(off, t), g * LANES:(g + 1) * LANES] for g in range(NSA_GROUPS)]
            return tuple(_dot(ks[h // hpg], qts[h]) for h in range(nh))
        return scores

    def values_of(vt_ref):
        return lambda kt: [vt_ref[h // hpg, kt] for h in range(nh)]

    def selected(kt, extra=None):
        expand = jnp.where(e_blk == e_key + kt * (t // NSA_BLOCK), 1.0, 0.0).astype(BF16)
        per_group = [_dot(expand, sb) > 0.5 for sb in sel_b]
        if extra is not None:
            per_group = [m & extra for m in per_group]
        return [per_group[h // hpg] for h in range(nh)]

    o_slc = _attend(scores_of(ks_ref), values_of(vst_ref), selected, selected(i, causal), 0, i, nh)

    wt = NSA_WINDOW // t
    win_scores, win_values = scores_of(kw_ref), values_of(vwt_ref)
    upper = krow > qcol
    far = jnp.maximum(i - wt, 0)
    penalty = jnp.where(i >= wt, 0.0, NEG_INF)
    s_far, s_diag = win_scores(far), win_scores(i)
    stats = []
    for c in range(nh):
        s = jnp.where(upper, s_far[c] + penalty, s_diag[c])
        m = jnp.max(s, axis=0, keepdims=True)
        stats.append((m, jnp.exp2(s - m).astype(BF16)))
    v_far, v_diag = win_values(far), win_values(i)
    carry = tuple((m, _dot(v_far[c], jnp.where(upper, p, jnp.zeros_like(p)))
                   + _dot(v_diag[c], jnp.where(upper, jnp.zeros_like(p), p)))
                  for c, (m, p) in enumerate(stats))
    carry = lax.fori_loop(jnp.maximum(i - wt + 1, 0), i,
                          lambda kt, cr: _online_t(win_scores(kt), None, cr, win_values(kt)), carry)
    o_win = _normalise(carry)

    for h in range(nh):
        g, r = h // hpg, 3 * (h % hpg)
        gt = jax.nn.sigmoid(gt_ref[(1 + g) * LANES + r:(1 + g) * LANES + r + 3, :])
        o = gt[0:1] * o_cmp[h] + gt[1:2] * o_slc[h] + gt[2:3] * o_win[h]
        o_ref[:, h * LANES:(h + 1) * LANES] = o.T.astype(BF16)


def _nsa_attention(qbt, ks, kw, vst, vwt, kc, vct, small_t, *, batch, seq):
    t = ATT_TILE
    nq = seq // t
    T = batch * seq
    n_sel = min(NSA_TOP_N, seq // NSA_BLOCK)
    G = NSA_GROUPS
    k_spec = pl.BlockSpec((seq, G * LANES), lambda b, i: (b, 0))
    vt_spec = pl.BlockSpec((G, nq, VT_ROWS, t), lambda b, i: (0, b, 0, 0))
    c_spec = pl.BlockSpec((G, None, LANES, LANES), lambda b, i: (0, b, 0, 0))
    return pl.pallas_call(
        functools.partial(_nsa_kernel, n_sel=n_sel, nb=seq // NSA_BLOCK),
        grid=(batch, nq),
        in_specs=[
            pl.BlockSpec((NSA_HEADS, 1, LANES, t), lambda b, i: (0, b * nq + i, 0, 0)),
            k_spec, vt_spec, k_spec, vt_spec, c_spec, c_spec,
            pl.BlockSpec((SMALL_COLS, t), lambda b, i: (0, b * nq + i)),
        ],
        out_specs=pl.BlockSpec((t, NSA_HEADS * LANES), lambda b, i: (b * nq + i, 0)),
        out_shape=jax.ShapeDtypeStruct((T, NSA_HEADS * LANES), BF16),
        compiler_params=_cparams(("parallel", "arbitrary")),
        name="nsa_attn",
    )(qbt, ks, vst, kw, vwt, kc, vct, small_t)


def _merge_kernel(x_ref, oa_ref, ob_ref, oc_ref, g0_ref, g1_ref, g2_ref,
                  wa_ref, wb_ref, wc_ref, wo_ref, o_ref):
    y = (g0_ref[...].astype(F32) * _dot(oa_ref[...], wa_ref[...])
         + g1_ref[...].astype(F32) * _dot(ob_ref[...], wb_ref[...])
         + g2_ref[...].astype(F32) * _dot(oc_ref[...], wc_ref[...]))
    o_ref[...] = x_ref[...] + _dot(y.astype(BF16), wo_ref[...])


def _merge(x2d, oa, ob, oc, p_gm, wa, wb, wc, wo, layer, *, tm=512):
    T, D = x2d.shape

    def rows(w, c=0):
        return pl.BlockSpec((tm, w), lambda i: (i, c))

    def full(w):
        return pl.BlockSpec((None,) + w.shape[1:], lambda i: (layer, 0, 0), pipeline_mode=pl.Buffered(1))

    return pl.pallas_call(
        _merge_kernel,
        grid=(T // tm,),
        in_specs=[rows(D), rows(oa.shape[1]), rows(ob.shape[1]), rows(oc.shape[1]),
                  rows(D, 0), rows(D, 1), rows(D, 2), full(wa), full(wb), full(wc), full(wo)],
        out_specs=rows(D),
        out_shape=jax.ShapeDtypeStruct((T, D), F32),
        compiler_params=_cparams(("parallel",), VMEM_LIMIT_FFN),
        name="merge",
    )(x2d, oa, ob, oc, p_gm, p_gm, p_gm, wa, wb, wc, wo)


def _rope_tables(positions, width, rot_dim):
    inv_freq = ROPE_THETA ** (-jnp.arange(0, rot_dim, 2, dtype=F32) / rot_dim)
    ang = positions.astype(F32).reshape(-1, 1) * inv_freq
    cos, sin = jnp.cos(ang), jnp.sin(ang)
    T = ang.shape[0]
    rest = width - rot_dim
    c = jnp.concatenate([cos, cos, jnp.ones((T, rest), F32)], axis=1)
    s = jnp.concatenate([-sin, sin, jnp.zeros((T, rest), F32)], axis=1)
    rep = LANES // width
    return [jnp.tile(t, (1, rep)) for t in (c, s)]


def _pad_lanes(v):
    return jnp.pad(v.astype(F32), (0, LANES - v.shape[0]))


def kernel(x, positions, ffn1_norm, ffn1_w_gate, ffn1_w_up, ffn1_w_down, mix_norm, w_in, da_q_norm, da_k_norm, da_lambda, da_out_norm, nsa_q_norm, nsa_k_norm, nsa_cmp_pos, nsa_cmp_w1, nsa_cmp_w2, fox_q_norm, fox_k_norm, fox_f_bias, w_branch_a, w_branch_b, w_branch_c, w_out, ffn2_norm, ffn2_w_gate, ffn2_w_up, ffn2_w_down):
    B, S, D = x.shape
    depth = w_in.shape[0]
    T = B * S
    G, L, d = NSA_GROUPS, NSA_BLOCK, HEAD_DIM
    nb = S // L
    assert S % ATT_TILE == 0 and nb <= LANES and nb % 8 == 0 and NSA_WINDOW % ATT_TILE == 0

    rope_tab = jnp.stack(_rope_tables(positions, DA_QK_DIM, DA_QK_DIM // ROPE_FRACTION)
                         + _rope_tables(positions, HEAD_DIM, HEAD_DIM // ROPE_FRACTION))

    widths = [512, 512, 512, 1024, 1536, 24, 512, 512, 512, 4, 3 * D]
    offs = [0]
    for w in widths:
        offs.append(offs[-1] + w)
    (o_aq, o_ak, o_av, o_bq, o_bkv, o_bg, o_cq, o_ck, o_cv, o_cf, o_gm, o_end) = offs

    w_qkv = jnp.concatenate([w_in[:, :, o_aq:o_bg], w_in[:, :, o_cq:o_cf]], axis=2).astype(BF16)
    w_gm = w_in[:, :, o_gm:o_end].astype(BF16)
    zpad = lambda n: jnp.zeros((depth, D, n), F32)
    w_small = jnp.concatenate([
        w_in[:, :, o_cf:o_gm], zpad(LANES - FOX_HEADS),
        w_in[:, :, o_bg:o_bg + 12], zpad(LANES - 12),
        w_in[:, :, o_bg + 12:o_bg + 24], zpad(LANES - 12)], axis=2)
    w_small_hi = w_small.astype(BF16)
    w_small_lo = (w_small - w_small_hi.astype(F32)).astype(BF16)
    ffn_f32 = [(ffn1_w_gate, ffn1_w_up, ffn1_w_down), (ffn2_w_gate, ffn2_w_up, ffn2_w_down)]
    ffn_g = [ffn1_norm.reshape(depth, 1, D), ffn2_norm.reshape(depth, 1, D)]
    ffn_next = [(k // 2, *ffn_f32[k % 2]) for k in range(1, 2 * depth)] + [None]
    ffn_w = tuple(_to_bf16(w, rows=256, layer=0) for w in ffn_f32[0])
    g_mix = mix_norm.reshape(depth, 1, D)
    w_a, w_b, w_c, w_o = (_to_bf16(w, rows=512) for w in (w_branch_a, w_branch_b, w_branch_c, w_out))
    cmp_pos = nsa_cmp_pos.reshape(depth, 2, 1, L * d)

    xc = x.reshape(T, D)
    for l in range(depth):
        lam_init = 0.8 - 0.6 * math.exp(-0.3 * l)
        gains = jnp.stack([
            jnp.tile(da_q_norm[l], 2), jnp.tile(da_k_norm[l], 2), nsa_q_norm[l],
            nsa_k_norm[l, 0], nsa_k_norm[l, 1], nsa_k_norm[l, 2], fox_q_norm[l], fox_k_norm[l],
            _pad_lanes(fox_f_bias[l]), da_out_norm[l]] + [jnp.zeros((LANES,), F32)] * 6).astype(F32)

        xc, ffn_w = _ffn(xc, ffn_g[0], *ffn_w, l, ffn_next[2 * l])

        p_qkv = _proj(xc, g_mix, w_qkv, l, gate=False, tn=QKV_COLS // 4)
        p_gm = _proj(xc, g_mix, w_gm, l, gate=True, tn=D // 2)
        small, small_t = _proj_small(xc, g_mix, w_small_hi, w_small_lo, l)

        (qat, ka, vat, qbt, ks, kw, vst, vwt, qct, kf, vct, kv_cmp) = _prep(
            p_qkv, small, rope_tab, gains, seq=S)

        kc, vcmp_t = _compress(kv_cmp.reshape(2, G, B * nb, L * d), cmp_pos, nsa_cmp_w1, nsa_cmp_w2,
                               nsa_k_norm[l, 0].reshape(1, d), l, batch=B)

        o_a = _diff_attention(qat, ka, vat, da_lambda[l], gains, batch=B, seq=S, lam_init=lam_init)
        o_b = _nsa_attention(qbt, ks, kw, vst, vwt, kc, vcmp_t, small_t, batch=B, seq=S)
        o_c = _fox_attention(qct, kf, vct, batch=B, seq=S)

        xc = _merge(xc, o_a, o_b, o_c, p_gm, w_a, w_b, w_c, w_o, l)

        xc, ffn_w = _ffn(xc, ffn_g[1], *ffn_w, l, ffn_next[2 * l + 1])
    return xc.reshape(B, S, D)
```

```python
import functools
import math

import jax
import jax.numpy as jnp
from jax import lax
from jax.experimental import pallas as pl
from jax.experimental.pallas import tpu as pltpu

F32 = jnp.float32
BF16 = jnp.bfloat16

HEAD_DIM = 128
LANES = 128
ROPE_THETA = 500000.0
ROPE_FRACTION = 4
EPS = 1e-6
NEG_INF = -1e30
LOG2E = math.log2(math.e)

DA_HEADS = 4
DA_QK_DIM = 64
NSA_HEADS = 8
NSA_GROUPS = 2
NSA_HPG = NSA_HEADS // NSA_GROUPS
NSA_BLOCK = 64
BLOCK_SHIFT = 6
NSA_TOP_N = 8
NSA_WINDOW = 512
NSA_FORCED_SCORE = 1e4
FOX_HEADS = 4

ATT_TILE = 256
FOX_AUG = 2 * LANES
VT_ROWS = LANES + 16

BLK_AQ, BLK_AK, BLK_AV, BLK_BQ, BLK_BKV, BLK_CQ, BLK_CK, BLK_CV = 0, 4, 8, 12, 20, 32, 36, 40
QKV_COLS = 44 * LANES
SMALL_COLS = 3 * LANES

G_DAQ, G_DAK, G_NQ, G_NK0, G_NK1, G_NK2, G_FQ, G_FK, G_FBIAS, G_DAOUT = range(10)

VMEM_BYTES_V7X = 64 * 1024 * 1024
VMEM_LIMIT = VMEM_BYTES_V7X - 8 * 1024 * 1024
VMEM_LIMIT_FFN = VMEM_BYTES_V7X - 3 * 1024 * 1024


def _cparams(sem, limit=VMEM_LIMIT):
    return pltpu.CompilerParams(dimension_semantics=sem, vmem_limit_bytes=limit)


def _dot(a, b):
    return jnp.dot(a, b, preferred_element_type=F32)


def _dot_t(a, b):
    return lax.dot_general(a, b, (((1,), (1,)), ((), ())), preferred_element_type=F32)


def _rms_rows(xf, gain):
    return xf * lax.rsqrt(jnp.mean(xf * xf, axis=-1, keepdims=True) + EPS) * gain


def _ffn_kernel(x_ref, g_ref, wg_ref, wu_ref, wd_ref, *rest):
    n_cast = (len(rest) - 2) // 2
    o_ref, h_scr = rest[n_cast], rest[-1]

    @pl.when(pl.program_id(1) == 0)
    def _():
        xf = x_ref[...]
        h_scr[...] = _rms_rows(xf, g_ref[...]).astype(BF16)
        o_ref[...] = xf

    h = h_scr[...]
    g = _dot(h, wg_ref[...])
    u = _dot(h, wu_ref[...])
    a = (g * jax.nn.sigmoid(g)) * (0.5 * u)
    o_ref[...] += _dot(a.astype(BF16), wd_ref[...])

    for src, dst in zip(rest[:n_cast], rest[n_cast + 1:-1]):
        dst[...] = src[...].astype(BF16)


def _ffn(x2d, gain, wg, wu, wd, layer, cast_next=None, *, tm=1024, tf=512):
    T, D = x2d.shape
    F = wg.shape[2]
    tm = min(tm, T)
    ni, nj = T // tm, F // tf
    in_specs = [
        pl.BlockSpec((tm, D), lambda i, j: (i, 0)),
        pl.BlockSpec((None, 1, D), lambda i, j: (layer, 0, 0)),
        pl.BlockSpec((None, D, tf), lambda i, j: (0, 0, j)),
        pl.BlockSpec((None, D, tf), lambda i, j: (0, 0, j)),
        pl.BlockSpec((None, tf, D), lambda i, j: (0, j, 0)),
    ]
    out_specs = [pl.BlockSpec((tm, D), lambda i, j: (i, 0))]
    out_shape = [jax.ShapeDtypeStruct((T, D), F32)]
    args = [x2d, gain, wg, wu, wd]
    if cast_next is not None:
        nl, *nws = cast_next
        for w in nws:
            _, R, C = w.shape
            blk, imap = (((R // ni, C // nj), lambda i, j: (nl, i, j)) if C == F
                         else ((R // nj, C // ni), lambda i, j: (nl, j, i)))
            in_specs.append(pl.BlockSpec((None,) + blk, imap))
            omap = (lambda i, j: (0, i, j)) if C == F else (lambda i, j: (0, j, i))
            out_specs.append(pl.BlockSpec((None,) + blk, omap))
            out_shape.append(jax.ShapeDtypeStruct((1, R, C), BF16))
            args.append(w)
    outs = pl.pallas_call(
        _ffn_kernel,
        grid=(ni, nj),
        in_specs=in_specs,
        out_specs=tuple(out_specs),
        out_shape=tuple(out_shape),
        scratch_shapes=[pltpu.VMEM((tm, D), BF16)],
        compiler_params=_cparams(("parallel", "arbitrary"), VMEM_LIMIT_FFN),
        name="ffn",
    )(*args)
    return outs[0], tuple(outs[1:])


def _cast_kernel(x_ref, o_ref):
    o_ref[...] = x_ref[...].astype(BF16)


def _to_bf16(w, *, rows, layer=None):
    depth, R, C = w.shape
    first, count = (0, depth) if layer is None else (layer, 1)
    return pl.pallas_call(
        _cast_kernel,
        grid=(count, R // rows),
        in_specs=[pl.BlockSpec((None, rows, C), lambda l, i: (first + l, i, 0))],
        out_specs=pl.BlockSpec((None, rows, C), lambda l, i: (l, i, 0)),
        out_shape=jax.ShapeDtypeStruct((count, R, C), BF16),
        compiler_params=_cparams(("parallel", "parallel")),
        name="cast_bf16",
    )(w)


def _proj_kernel(x_ref, g_ref, w_ref, o_ref, h_scr, *, gate):
    @pl.when(pl.program_id(1) == 0)
    def _():
        h_scr[...] = _rms_rows(x_ref[...], g_ref[...]).astype(BF16)

    r = _dot(h_scr[...], w_ref[...])
    o_ref[...] = (jax.nn.sigmoid(r) if gate else r).astype(o_ref.dtype)


def _proj(x2d, gain, w, layer, *, gate, tm=1024, tn):
    T, D = x2d.shape
    N = w.shape[2]
    tm = min(tm, T)
    return pl.pallas_call(
        functools.partial(_proj_kernel, gate=gate),
        grid=(T // tm, N // tn),
        in_specs=[
            pl.BlockSpec((tm, D), lambda i, j: (i, 0)),
            pl.BlockSpec((None, 1, D), lambda i, j: (layer, 0, 0)),
            pl.BlockSpec((None, D, tn), lambda i, j: (layer, 0, j)),
        ],
        out_specs=pl.BlockSpec((tm, tn), lambda i, j: (i, j)),
        out_shape=jax.ShapeDtypeStruct((T, N), BF16),
        scratch_shapes=[pltpu.VMEM((tm, D), BF16)],
        compiler_params=_cparams(("parallel", "arbitrary")),
        name="proj_gate" if gate else "proj_qkv",
    )(x2d, gain, w)


def _split3(v):
    hi = v.astype(BF16)
    r = v - hi.astype(F32)
    mid = r.astype(BF16)
    lo = (r - mid.astype(F32)).astype(BF16)
    return hi, mid, lo


def _proj_small_kernel(x_ref, g_ref, whi_ref, wlo_ref, o_ref, ot_ref):
    h = _rms_rows(x_ref[...], g_ref[...])
    hi, mid, _ = _split3(h)
    whi = whi_ref[...]
    r = _dot(hi, whi) + (_dot(mid, whi) + _dot(hi, wlo_ref[...]))
    o_ref[...] = r
    for c in range(r.shape[1] // LANES):
        ot_ref[c * LANES:(c + 1) * LANES, :] = r[:, c * LANES:(c + 1) * LANES].T


def _proj_small(x2d, gain, whi, wlo, layer, *, tm=512):
    T, D = x2d.shape
    N = whi.shape[2]
    return pl.pallas_call(
        _proj_small_kernel,
        grid=(T // tm,),
        in_specs=[
            pl.BlockSpec((tm, D), lambda i: (i, 0)),
            pl.BlockSpec((None, 1, D), lambda i: (layer, 0, 0)),
            pl.BlockSpec((None, D, N), lambda i: (layer, 0, 0)),
            pl.BlockSpec((None, D, N), lambda i: (layer, 0, 0)),
        ],
        out_specs=(pl.BlockSpec((tm, N), lambda i: (i, 0)), pl.BlockSpec((N, tm), lambda i: (0, i))),
        out_shape=(jax.ShapeDtypeStruct((T, N), F32), jax.ShapeDtypeStruct((N, T), F32)),
        compiler_params=_cparams(("parallel",)),
        name="proj_small",
    )(x2d, gain, whi, wlo)


def _prep_kernel(p_ref, sm_ref, rope_ref, gains_ref,
                 qat_ref, ka_ref, vat_ref, qbt_ref, ks_ref, kw_ref, vst_ref, vwt_ref,
                 qct_ref, kf_ref, vct_ref, kvc_ref,
                 carry_scr, *, tiles_per_seq):
    i = pl.program_id(0)
    tm = p_ref.shape[0]
    lane = lax.broadcasted_iota(jnp.int32, (tm, LANES), 1)
    src = lax.broadcasted_iota(jnp.int32, (LANES, LANES), 0)
    dst = lax.broadcasted_iota(jnp.int32, (LANES, LANES), 1)

    def const(cond, val=1.0):
        return jnp.where(cond, val, 0.0).astype(BF16)

    def group_mean(width):
        shift = width.bit_length() - 1
        return const((src >> shift) == (dst >> shift), 1.0 / width)

    def swap_halves(width, half):
        d = dst & (width - 1)
        return const(((d < half) & (src == dst + half)) | ((d >= half) & (d < 2 * half) & (src == dst - half)))

    mean128, mean64 = group_mean(HEAD_DIM), group_mean(DA_QK_DIM)
    eye = const(src == dst)

    def dot1(v, mat):
        return _dot(v.astype(BF16), mat)

    def raw(c):
        return p_ref[:, c * LANES:(c + 1) * LANES]

    def chunk(c):
        return raw(c).astype(F32)

    def gain(r):
        return gains_ref[r:r + 1, :]

    def rope_tab(r):
        return rope_ref[r]


    def tr(y):
        return _dot_t(eye, y.astype(BF16)).astype(BF16)

    def put_vt(ref, n, c):
        ref[n, 0, 0:LANES, :] = tr(raw(c))
        ref[n, 0, LANES:VT_ROWS, :] = jnp.ones((VT_ROWS - LANES, tm), BF16)

    half_a = swap_halves(DA_QK_DIM, DA_QK_DIM // ROPE_FRACTION // 2)
    half_b = swap_halves(HEAD_DIM, HEAD_DIM // ROPE_FRACTION // 2)
    scale_a = DA_QK_DIM ** -0.5 * LOG2E
    scale_b = HEAD_DIM ** -0.5 * LOG2E

    def run_group(specs):
        xs = [chunk(c) for c, *_ in specs]
        ms = [None if nm is None else dot1(x * x, nm[1]) for x, (_, nm, *_) in zip(xs, specs)]
        ys = [x if nm is None else x * lax.rsqrt(m + EPS) * gain(nm[0])
              for x, m, (_, nm, *_) in zip(xs, ms, specs)]
        ps = [None if rp is None else dot1(y, rp[1]) for y, (_, _, rp, *_) in zip(ys, specs)]
        rs = [y if rp is None else y * rope_tab(rp[0]) + p * rope_tab(rp[0] + 1)
              for y, p, (_, _, rp, *_) in zip(ys, ps, specs)]
        rs = [r if sc == 1.0 else r * sc for r, (_, _, _, sc, _, _) in zip(rs, specs)]
        outs = [tr(r) if tp else r.astype(BF16) for r, (_, _, _, _, tp, _) in zip(rs, specs)]
        for o, spec in zip(outs, specs):
            spec[-1](o)

    def nat_sink(ref, lo):
        def sink(o):
            ref[:, lo:lo + LANES] = o
        return sink

    def tile_sink(ref, n, rows=slice(None)):
        def sink(o):
            ref[n, 0, rows, :] = o
        return sink

    def grp_sink(n, g):
        def sink(o):
            kvc_ref[n, g] = o
        return sink

    rope_a, rope_b = (0, half_a), (2, half_b)
    run_group([(BLK_AQ + h, (G_DAQ, mean64), rope_a, scale_a, True, tile_sink(qat_ref, h))
               for h in range(DA_HEADS)]
              + [(BLK_AK + h, (G_DAK, mean64), rope_a, 1.0, False, nat_sink(ka_ref, h * LANES))
                 for h in range(DA_HEADS)])
    run_group([(BLK_BQ + h, (G_NQ, mean128), rope_b, scale_b, True, tile_sink(qbt_ref, h))
               for h in range(NSA_HEADS)])
    run_group([(BLK_BKV + 0 + g, None, rope_b, 1.0, False, grp_sink(0, g)) for g in range(NSA_GROUPS)]
              + [(BLK_BKV + 4 + g, (G_NK1, mean128), rope_b, 1.0, False, nat_sink(ks_ref, g * LANES))
                 for g in range(NSA_GROUPS)]
              + [(BLK_BKV + 8 + g, (G_NK2, mean128), rope_b, 1.0, False, nat_sink(kw_ref, g * LANES))
                 for g in range(NSA_GROUPS)])
    run_group([(BLK_CQ + h, (G_FQ, mean128), None, scale_b, True, tile_sink(qct_ref, h, slice(0, LANES)))
               for h in range(FOX_HEADS)]
              + [(BLK_CK + h, (G_FK, mean128), None, 1.0, False, nat_sink(kf_ref, h * FOX_AUG))
                 for h in range(FOX_HEADS)])
    for g in range(NSA_GROUPS):
        kvc_ref[1, g] = raw(BLK_BKV + 2 + g)
        put_vt(vst_ref, g, BLK_BKV + 6 + g)
        put_vt(vwt_ref, g, BLK_BKV + 10 + g)
    for h in range(DA_HEADS):
        put_vt(vat_ref, h, BLK_AV + h)
    for h in range(FOX_HEADS):
        put_vt(vct_ref, h, BLK_CV + h)

    @pl.when(i % tiles_per_seq == 0)
    def _():
        carry_scr[...] = jnp.zeros_like(carry_scr)

    z = sm_ref[...] + gain(G_FBIAS)
    logf = jnp.minimum(z, 0.0) - jnp.log(1.0 + jnp.exp(-jnp.abs(z)))
    r_id = lax.broadcasted_iota(jnp.int32, (tm, tm), 0)
    c_id = lax.broadcasted_iota(jnp.int32, (tm, tm), 1)
    tri = jnp.where(r_id >= c_id, 1.0, 0.0).astype(BF16)
    hi, mid, lo = _split3(logf)
    cum = _dot(tri, hi) + (_dot(tri, mid) + _dot(tri, lo)) + carry_scr[0:1, :]
    carry_scr[0:1, :] = cum[tm - 1:tm, :]

    for h in range(FOX_HEADS):
        c = jnp.broadcast_to(cum[:, h:h + 1], (tm, LANES)) * LOG2E
        c_hi, c_mid, c_lo = (v.astype(F32) for v in _split3(c))
        ones = jnp.where(lane < 6, 1.0, 0.0)
        q_aug = jnp.where(lane == 0, c_hi, jnp.where(lane == 1, c_mid, jnp.where(lane == 2, c_lo, ones)))
        k_aug = jnp.where(lane == 3, -c_hi, jnp.where(lane == 4, -c_mid, jnp.where(lane == 5, -c_lo, ones)))
        qct_ref[h, 0, LANES:FOX_AUG, :] = tr(q_aug)
        kf_ref[:, h * FOX_AUG + LANES:(h + 1) * FOX_AUG] = k_aug.astype(BF16)


def _prep(p_qkv, small, rope_tab, gains, *, seq):
    tm = ATT_TILE
    T = p_qkv.shape[0]
    nt = T // tm
    G = NSA_GROUPS

    def rows(w):
        return pl.BlockSpec((tm, w), lambda i: (i, 0))

    def tiles_t(n, d=LANES):
        return (jax.ShapeDtypeStruct((n, nt, d, tm), BF16), pl.BlockSpec((n, 1, d, tm), lambda i: (0, i, 0, 0)))

    def nat(w):
        return (jax.ShapeDtypeStruct((T, w), BF16), rows(w))

    def grp():
        return (jax.ShapeDtypeStruct((2, G, T, LANES), BF16),
                pl.BlockSpec((2, G, tm, LANES), lambda i: (0, 0, i, 0)))

    outs = [
        tiles_t(DA_HEADS), nat(DA_HEADS * LANES), tiles_t(DA_HEADS, VT_ROWS),
        tiles_t(NSA_HEADS), nat(G * LANES), nat(G * LANES),
        tiles_t(G, VT_ROWS), tiles_t(G, VT_ROWS),
        tiles_t(FOX_HEADS, FOX_AUG), nat(FOX_HEADS * FOX_AUG), tiles_t(FOX_HEADS, VT_ROWS),
        grp(),
    ]
    return pl.pallas_call(
        functools.partial(_prep_kernel, tiles_per_seq=seq // tm),
        grid=(nt,),
        in_specs=[
            rows(QKV_COLS),
            rows(LANES),
            pl.BlockSpec((4, tm, LANES), lambda i: (0, i, 0)),
            pl.BlockSpec(gains.shape, lambda i: (0, 0)),
        ],
        out_specs=tuple(o[1] for o in outs),
        out_shape=tuple(o[0] for o in outs),
        scratch_shapes=[pltpu.VMEM((8, LANES), F32)],
        compiler_params=_cparams(("arbitrary",)),
        name="prep",
    )(p_qkv, small, rope_tab, gains)


def _compress_kernel(x_ref, pos_ref, w1_ref, w2_ref, gain_ref, kc_ref, vct_ref, *, batch):
    def mlp(kv):
        blk = (x_ref[kv].astype(F32) + pos_ref[kv]).astype(BF16)
        hid = jax.nn.gelu(_dot(blk, w1_ref[kv].astype(BF16)))
        return _dot(hid.astype(BF16), w2_ref[kv].astype(BF16))

    kc = _rms_rows(mlp(0), gain_ref[...])
    vc = mlp(1)
    nb = kc.shape[0] // batch
    pad = jnp.zeros((LANES - nb, LANES), F32)
    for b in range(batch):
        kc_ref[b] = jnp.concatenate([kc[b * nb:(b + 1) * nb], pad], axis=0).astype(BF16)
        vct_ref[b] = jnp.concatenate([vc[b * nb:(b + 1) * nb], pad], axis=0).T.astype(BF16)


def _compress(xkv, pos, w1, w2, gain, layer, *, batch):
    _, G, R, K = xkv.shape
    d = w1.shape[-1]
    out = jax.ShapeDtypeStruct((G, batch, LANES, LANES), BF16)
    ospec = pl.BlockSpec((None, batch, LANES, LANES), lambda g: (g, 0, 0, 0))
    return pl.pallas_call(
        functools.partial(_compress_kernel, batch=batch),
        grid=(G,),
        in_specs=[
            pl.BlockSpec((2, None, R, K), lambda g: (0, g, 0, 0)),
            pl.BlockSpec((None, 2, 1, K), lambda g: (layer, 0, 0, 0)),
            pl.BlockSpec((None, 2, K, d), lambda g: (layer, 0, 0, 0)),
            pl.BlockSpec((None, 2, d, d), lambda g: (layer, 0, 0, 0)),
            pl.BlockSpec((1, d), lambda g: (0, 0)),
        ],
        out_specs=(ospec, ospec),
        out_shape=(out, out),
        compiler_params=_cparams(("arbitrary",)),
        name="compress",
    )(xkv, pos, w1, w2, gain)


def _online_t(s_ts, mask, carry, v_ts):
    masks = mask if isinstance(mask, (list, tuple)) else [mask] * len(s_ts)
    stats = []
    for s_t, (m, _), mk in zip(s_ts, carry, masks):
        if mk is not None:
            s_t = jnp.where(mk, s_t, NEG_INF)
        m_new = jnp.maximum(m, jnp.max(s_t, axis=0, keepdims=True))
        stats.append((m_new, jnp.exp2(m - m_new), jnp.exp2(s_t - m_new).astype(BF16)))
    return tuple((m_new, alpha * acc_t + _dot(v_t, p))
                 for (m_new, alpha, p), (_, acc_t), v_t in zip(stats, carry, v_ts))


def _attend(scores, values, loop_mask, last_mask, kt0, kt_last, n):
    t = ATT_TILE
    init = tuple((jnp.full((1, t), NEG_INF, F32), jnp.zeros((VT_ROWS, t), F32)) for _ in range(n))

    def body(kt, carry):
        mask = None if loop_mask is None else loop_mask(kt)
        return _online_t(scores(kt), mask, carry, values(kt))

    carry = lax.fori_loop(kt0, kt_last, body, init)
    return _normalise(_online_t(scores(kt_last), last_mask, carry, values(kt_last)))


def _normalise(carry):
    return [acc[0:LANES] / acc[LANES:LANES + 1] for (_, acc) in carry]


def _causal_mask_t(t):
    return lax.broadcasted_iota(jnp.int32, (t, t), 0) <= lax.broadcasted_iota(jnp.int32, (t, t), 1)


def _diff_kernel(qt_ref, k_ref, vt_ref, lam_ref, gains_ref, o_ref, *, lam_init):
    i = pl.program_id(2)
    t = ATT_TILE
    nh = qt_ref.shape[0]
    drow = lax.broadcasted_iota(jnp.int32, (LANES, t), 0)
    qts = []
    for h in range(nh):
        q = qt_ref[h, 0]
        qts.append(jnp.where(drow < DA_QK_DIM, q, jnp.zeros_like(q)))
        qts.append(jnp.where(drow < DA_QK_DIM, jnp.zeros_like(q), q))

    def scores(kt):
        off = pl.multiple_of(kt * t, t)
        ks = [k_ref[pl.ds(off, t), h * LANES:(h + 1) * LANES] for h in range(nh)]
        return tuple(_dot(ks[c // 2], qts[c]) for c in range(2 * nh))

    def values(kt):
        return [vt_ref[c // 2, kt] for c in range(2 * nh)]

    outs = _attend(scores, values, None, _causal_mask_t(t), 0, i, 2 * nh)

    lp = lam_ref[...]
    lam = (jnp.exp(jnp.sum(lp[0:1] * lp[1:2], axis=-1, keepdims=True))
           - jnp.exp(jnp.sum(lp[2:3] * lp[3:4], axis=-1, keepdims=True)) + lam_init)
    for h in range(nh):
        o = (outs[2 * h] - lam * outs[2 * h + 1]).T
        o = _rms_rows(o, gains_ref[G_DAOUT:G_DAOUT + 1, :]) * (1.0 - lam_init)
        o_ref[:, h * LANES:(h + 1) * LANES] = o.astype(BF16)


def _diff_attention(qat, ka, vat, da_lambda, gains, *, batch, seq, lam_init, heads_per_step=4):
    t = ATT_TILE
    nq = seq // t
    T = batch * seq
    hp = heads_per_step
    return pl.pallas_call(
        functools.partial(_diff_kernel, lam_init=lam_init),
        grid=(batch, DA_HEADS // hp, nq),
        in_specs=[
            pl.BlockSpec((hp, 1, LANES, t), lambda b, p, i: (p, b * nq + i, 0, 0)),
            pl.BlockSpec((seq, hp * LANES), lambda b, p, i: (b, p)),
            pl.BlockSpec((hp, nq, VT_ROWS, t), lambda b, p, i: (p, b, 0, 0)),
            pl.BlockSpec(da_lambda.shape, lambda b, p, i: (0, 0)),
            pl.BlockSpec(gains.shape, lambda b, p, i: (0, 0)),
        ],
        out_specs=pl.BlockSpec((t, hp * LANES), lambda b, p, i: (b * nq + i, p)),
        out_shape=jax.ShapeDtypeStruct((T, DA_HEADS * LANES), BF16),
        compiler_params=_cparams(("parallel", "parallel", "arbitrary")),
        name="diff_attn",
    )(qat, ka, vat, da_lambda, gains)


def _fox_kernel(qt_ref, k_ref, vt_ref, o_ref):
    i = pl.program_id(1)
    t = ATT_TILE
    nh, bs = qt_ref.shape[0], qt_ref.shape[1]
    chains = [(b, h) for b in range(bs) for h in range(nh)]
    qts = [qt_ref[h, b, 0] for b, h in chains]

    def scores(kt):
        off = pl.multiple_of(kt * t, t)
        return tuple(_dot(k_ref[b, pl.ds(off, t), h * FOX_AUG:(h + 1) * FOX_AUG], q)
                     for (b, h), q in zip(chains, qts))

    def values(kt):
        return [vt_ref[h, b, kt] for b, h in chains]

    outs = _attend(scores, values, None, _causal_mask_t(t), 0, i, len(chains))
    for (b, h), o in zip(chains, outs):
        o_ref[b, :, h * LANES:(h + 1) * LANES] = o.T.astype(BF16)


def _fox_attention(qct, kf, vct, *, batch, seq):
    t = ATT_TILE
    nq = seq // t
    nh = FOX_HEADS
    bs = 2 if batch % 2 == 0 else 1
    nbp = batch // bs
    out = pl.pallas_call(
        _fox_kernel,
        grid=(nbp, nq),
        in_specs=[
            pl.BlockSpec((nh, None, bs, 1, FOX_AUG, t), lambda p, i: (0, p, 0, i, 0, 0)),
            pl.BlockSpec((None, bs, seq, nh * FOX_AUG), lambda p, i: (p, 0, 0, 0)),
            pl.BlockSpec((nh, None, bs, nq, VT_ROWS, t), lambda p, i: (0, p, 0, 0, 0, 0)),
        ],
        out_specs=pl.BlockSpec((None, bs, t, nh * LANES), lambda p, i: (p, 0, i, 0)),
        out_shape=jax.ShapeDtypeStruct((nbp, bs, seq, nh * LANES), BF16),
        compiler_params=_cparams(("parallel", "arbitrary")),
        name="fox_attn",
    )(qct.reshape(nh, nbp, bs, nq, FOX_AUG, t), kf.reshape(nbp, bs, seq, nh * FOX_AUG),
      vct.reshape(nh, nbp, bs, nq, VT_ROWS, t))
    return out.reshape(batch * seq, nh * LANES)


def _nsa_kernel(qt_ref, ks_ref, vst_ref, kw_ref, vwt_ref, kc_ref, vct_ref, gt_ref, o_ref, *, n_sel, nb):
    i = pl.program_id(1)
    t = ATT_TILE
    lo = i * t
    nh = qt_ref.shape[0]
    hpg = nh // NSA_GROUPS
    qts = [qt_ref[h, 0] for h in range(nh)]
    tq_row = lo + lax.broadcasted_iota(jnp.int32, (1, t), 1)
    nidx = lax.broadcasted_iota(jnp.int32, (nb, t), 0)
    nidx_f = nidx.astype(F32)
    cmask = nidx * NSA_BLOCK + (NSA_BLOCK - 1) <= tq_row
    cur = tq_row >> BLOCK_SHIFT
    forced = (nidx == 0) | (nidx == cur) | (nidx == cur - 1)
    zpad = jnp.zeros((LANES - nb, t), F32)

    def lane_tile(v):
        return jnp.concatenate([v, zpad], axis=0).astype(BF16)

    o_cmp, sel_b = [], []
    for g in range(NSA_GROUPS):
        kc, vct = kc_ref[g, 0:nb, :], vct_ref[g]
        importance = jnp.zeros((nb, t), F32)
        for h in range(g * hpg, (g + 1) * hpg):
            s = jnp.where(cmask, _dot(kc, qts[h]), NEG_INF)
            e = jnp.where(cmask, jnp.exp2(s - jnp.max(s, axis=0, keepdims=True)), 0.0)
            p = e / jnp.maximum(jnp.sum(e, axis=0, keepdims=True), 1e-30)
            importance = importance + p
            o_cmp.append(_dot(vct, lane_tile(p)))

        score = jnp.where(forced, NSA_FORCED_SCORE, jnp.where(nidx <= cur, importance, -1.0))
        sel = jnp.zeros((nb, t), F32)
        for _ in range(n_sel):
            best = jnp.max(score, axis=0, keepdims=True)
            first = jnp.min(jnp.where(score == best, nidx_f, float(nb)), axis=0, keepdims=True)
            hit = nidx_f == first
            sel = jnp.where(hit, 1.0, sel)
            score = jnp.where(hit, -3e38, score)
        sel_b.append(lane_tile(sel))

    e_key = lax.broadcasted_iota(jnp.int32, (t, LANES), 0) >> BLOCK_SHIFT
    e_blk = lax.broadcasted_iota(jnp.int32, (t, LANES), 1)
    krow = lax.broadcasted_iota(jnp.int32, (t, t), 0)
    qcol = lax.broadcasted_iota(jnp.int32, (t, t), 1)
    causal = krow <= qcol

    def scores_of(k_ref):
        def scores(kt):
            off = pl.multiple_of(kt * t, t)
            ks = [k_ref[pl.ds(off, t), g * LANES:(g + 1) * LANES] for g in range(NSA_GROUPS)]
            return tuple(_dot(ks[h // hpg], qts[h]) for h in range(nh))
        return scores

    def values_of(vt_ref):
        return lambda kt: [vt_ref[h // hpg, kt] for h in range(nh)]

    def selected(kt, extra=None):
        expand = jnp.where(e_blk == e_key + kt * (t // NSA_BLOCK), 1.0, 0.0).astype(BF16)
        per_group = [_dot(expand, sb) > 0.5 for sb in sel_b]
        if extra is not None:
            per_group = [m & extra for m in per_group]
        return [per_group[h // hpg] for h in range(nh)]

    o_slc = _attend(scores_of(ks_ref), values_of(vst_ref), selected, selected(i, causal), 0, i, nh)

    wt = NSA_WINDOW // t
    win_scores, win_values = scores_of(kw_ref), values_of(vwt_ref)
    upper = krow > qcol
    far = jnp.maximum(i - wt, 0)
    penalty = jnp.where(i >= wt, 0.0, NEG_INF)
    s_far, s_diag = win_scores(far), win_scores(i)
    stats = []
    for c in range(nh):
        s = jnp.where(upper, s_far[c] + penalty, s_diag[c])
        m = jnp.max(s, axis=0, keepdims=True)
        stats.append((m, jnp.exp2(s - m).astype(BF16)))
    v_far, v_diag = win_values(far), win_values(i)
    carry = tuple((m, _dot(v_far[c], jnp.where(upper, p, jnp.zeros_like(p)))
                   + _dot(v_diag[c], jnp.where(upper, jnp.zeros_like(p), p)))
                  for c, (m, p) in enumerate(stats))
    carry = lax.fori_loop(jnp.maximum(i - wt + 1, 0), i,
                          lambda kt, cr: _online_t(win_scores(kt), None, cr, win_values(kt)), carry)
    o_win = _normalise(carry)

    for h in range(nh):
        g, r = h // hpg, 3 * (h % hpg)
        gt = jax.nn.sigmoid(gt_ref[(1 + g) * LANES + r:(1 + g) * LANES + r + 3, :])
        o = gt[0:1] * o_cmp[h] + gt[1:2] * o_slc[h] + gt[2:3] * o_win[h]
        o_ref[:, h * LANES:(h + 1) * LANES] = o.T.astype(BF16)


def _nsa_attention(qbt, ks, kw, vst, vwt, kc, vct, small_t, *, batch, seq):
    t = ATT_TILE
    nq = seq // t
    T = batch * seq
    n_sel = min(NSA_TOP_N, seq // NSA_BLOCK)
    G = NSA_GROUPS
    k_spec = pl.BlockSpec((seq, G * LANES), lambda b, i: (b, 0))
    vt_spec = pl.BlockSpec((G, nq, VT_ROWS, t), lambda b, i: (0, b, 0, 0))
    c_spec = pl.BlockSpec((G, None, LANES, LANES), lambda b, i: (0, b, 0, 0))
    return pl.pallas_call(
        functools.partial(_nsa_kernel, n_sel=n_sel, nb=seq // NSA_BLOCK),
        grid=(batch, nq),
        in_specs=[
            pl.BlockSpec((NSA_HEADS, 1, LANES, t), lambda b, i: (0, b * nq + i, 0, 0)),
            k_spec, vt_spec, k_spec, vt_spec, c_spec, c_spec,
            pl.BlockSpec((SMALL_COLS, t), lambda b, i: (0, b * nq + i)),
        ],
        out_specs=pl.BlockSpec((t, NSA_HEADS * LANES), lambda b, i: (b * nq + i, 0)),
        out_shape=jax.ShapeDtypeStruct((T, NSA_HEADS * LANES), BF16),
        compiler_params=_cparams(("parallel", "arbitrary")),
        name="nsa_attn",
    )(qbt, ks, vst, kw, vwt, kc, vct, small_t)


def _merge_kernel(x_ref, oa_ref, ob_ref, oc_ref, gm_ref, wa_ref, wb_ref, wc_ref, wo_ref, o_ref, *, gm_off):
    D = x_ref.shape[1]
    g0, g1, g2 = (gm_ref[:, gm_off + n * D:gm_off + (n + 1) * D].astype(F32) for n in range(3))
    y = (g0 * _dot(oa_ref[...], wa_ref[...]) + g1 * _dot(ob_ref[...], wb_ref[...])
         + g2 * _dot(oc_ref[...], wc_ref[...]))
    o_ref[...] = x_ref[...] + _dot(y.astype(BF16), wo_ref[...])


def _merge(x2d, oa, ob, oc, p_gm, gm_off, wa, wb, wc, wo, layer, *, tm=512):
    T, D = x2d.shape

    def rows(w, c=0):
        return pl.BlockSpec((tm, w), lambda i: (i, c))

    def full(w):
        return pl.BlockSpec((None,) + w.shape[1:], lambda i: (layer, 0, 0), pipeline_mode=pl.Buffered(1))

    return pl.pallas_call(
        functools.partial(_merge_kernel, gm_off=gm_off),
        grid=(T // tm,),
        in_specs=[rows(D), rows(oa.shape[1]), rows(ob.shape[1]), rows(oc.shape[1]),
                  rows(p_gm.shape[1]), full(wa), full(wb), full(wc), full(wo)],
        out_specs=rows(D),
        out_shape=jax.ShapeDtypeStruct((T, D), F32),
        compiler_params=_cparams(("parallel",), VMEM_LIMIT_FFN),
        name="merge",
    )(x2d, oa, ob, oc, p_gm, wa, wb, wc, wo)


def _rope_tables(positions, width, rot_dim):
    inv_freq = ROPE_THETA ** (-jnp.arange(0, rot_dim, 2, dtype=F32) / rot_dim)
    ang = positions.astype(F32).reshape(-1, 1) * inv_freq
    cos, sin = jnp.cos(ang), jnp.sin(ang)
    T = ang.shape[0]
    rest = width - rot_dim
    c = jnp.concatenate([cos, cos, jnp.ones((T, rest), F32)], axis=1)
    s = jnp.concatenate([-sin, sin, jnp.zeros((T, rest), F32)], axis=1)
    rep = LANES // width
    return [jnp.tile(t, (1, rep)) for t in (c, s)]


def _pad_lanes(v):
    return jnp.pad(v.astype(F32), (0, LANES - v.shape[0]))


def kernel(x, positions, ffn1_norm, ffn1_w_gate, ffn1_w_up, ffn1_w_down, mix_norm, w_in, da_q_norm, da_k_norm, da_lambda, da_out_norm, nsa_q_norm, nsa_k_norm, nsa_cmp_pos, nsa_cmp_w1, nsa_cmp_w2, fox_q_norm, fox_k_norm, fox_f_bias, w_branch_a, w_branch_b, w_branch_c, w_out, ffn2_norm, ffn2_w_gate, ffn2_w_up, ffn2_w_down):
    B, S, D = x.shape
    depth = w_in.shape[0]
    T = B * S
    G, L, d = NSA_GROUPS, NSA_BLOCK, HEAD_DIM
    nb = S // L
    assert S % ATT_TILE == 0 and nb <= LANES and nb % 8 == 0 and NSA_WINDOW % ATT_TILE == 0

    rope_tab = jnp.stack(_rope_tables(positions, DA_QK_DIM, DA_QK_DIM // ROPE_FRACTION)
                         + _rope_tables(positions, HEAD_DIM, HEAD_DIM // ROPE_FRACTION))

    widths = [512, 512, 512, 1024, 1536, 24, 512, 512, 512, 4, 3 * D]
    offs = [0]
    for w in widths:
        offs.append(offs[-1] + w)
    (o_aq, o_ak, o_av, o_bq, o_bkv, o_bg, o_cq, o_ck, o_cv, o_cf, o_gm, o_end) = offs

    w_qkv = jnp.concatenate([w_in[:, :, o_aq:o_bg], w_in[:, :, o_cq:o_cf]], axis=2).astype(BF16)
    gm_lo = o_gm // LANES * LANES
    gm_off = o_gm - gm_lo
    gm_tile = 7 * LANES
    gm_cols = -(-(o_end - gm_lo) // gm_tile) * gm_tile
    w_gm = jnp.pad(w_in[:, :, gm_lo:o_end], ((0, 0), (0, 0), (0, gm_cols - (o_end - gm_lo)))).astype(BF16)
    zpad = lambda n: jnp.zeros((depth, D, n), F32)
    w_small = jnp.concatenate([
        w_in[:, :, o_cf:o_gm], zpad(LANES - FOX_HEADS),
        w_in[:, :, o_bg:o_bg + 12], zpad(LANES - 12),
        w_in[:, :, o_bg + 12:o_bg + 24], zpad(LANES - 12)], axis=2)
    w_small_hi = w_small.astype(BF16)
    w_small_lo = (w_small - w_small_hi.astype(F32)).astype(BF16)
    ffn_f32 = [(ffn1_w_gate, ffn1_w_up, ffn1_w_down), (ffn2_w_gate, ffn2_w_up, ffn2_w_down)]
    ffn_g = [ffn1_norm.reshape(depth, 1, D), ffn2_norm.reshape(depth, 1, D)]
    ffn_next = [(k // 2, *ffn_f32[k % 2]) for k in range(1, 2 * depth)] + [None]
    ffn_w = tuple(_to_bf16(w, rows=256, layer=0) for w in ffn_f32[0])
    g_mix = mix_norm.reshape(depth, 1, D)
    w_a, w_b, w_c, w_o = (_to_bf16(w, rows=512) for w in (w_branch_a, w_branch_b, w_branch_c, w_out))
    cmp_pos = nsa_cmp_pos.reshape(depth, 2, 1, L * d)

    xc = x.reshape(T, D)
    for l in range(depth):
        lam_init = 0.8 - 0.6 * math.exp(-0.3 * l)
        gains = jnp.stack([
            jnp.tile(da_q_norm[l], 2), jnp.tile(da_k_norm[l], 2), nsa_q_norm[l],
            nsa_k_norm[l, 0], nsa_k_norm[l, 1], nsa_k_norm[l, 2], fox_q_norm[l], fox_k_norm[l],
            _pad_lanes(fox_f_bias[l]), da_out_norm[l]] + [jnp.zeros((LANES,), F32)] * 6).astype(F32)

        xc, ffn_w = _ffn(xc, ffn_g[0], *ffn_w, l, ffn_next[2 * l])

        p_qkv = _proj(xc, g_mix, w_qkv, l, gate=False, tn=QKV_COLS // 4)
        p_gm = _proj(xc, g_mix, w_gm, l, gate=True, tn=gm_tile)
        small, small_t = _proj_small(xc, g_mix, w_small_hi, w_small_lo, l)

        (qat, ka, vat, qbt, ks, kw, vst, vwt, qct, kf, vct, kv_cmp) = _prep(
            p_qkv, small, rope_tab, gains, seq=S)

        kc, vcmp_t = _compress(kv_cmp.reshape(2, G, B * nb, L * d), cmp_pos, nsa_cmp_w1, nsa_cmp_w2,
                               nsa_k_norm[l, 0].reshape(1, d), l, batch=B)

        o_a = _diff_attention(qat, ka, vat, da_lambda[l], gains, batch=B, seq=S, lam_init=lam_init)
        o_b = _nsa_attention(qbt, ks, kw, vst, vwt, kc, vcmp_t, small_t, batch=B, seq=S)
        o_c = _fox_attention(qct, kf, vct, batch=B, seq=S)

        xc = _merge(xc, o_a, o_b, o_c, p_gm, gm_off, w_a, w_b, w_c, w_o, l)

        xc, ffn_w = _ffn(xc, ffn_g[1], *ffn_w, l, ffn_next[2 * l + 1])
    return xc.reshape(B, S, D)
```

```python
import functools
import math

import jax
import jax.numpy as jnp
from jax import lax
from jax.experimental import pallas as pl
from jax.experimental.pallas import tpu as pltpu

F32 = jnp.float32
BF16 = jnp.bfloat16

HEAD_DIM = 128
LANES = 128
ROPE_THETA = 500000.0
ROPE_FRACTION = 4
EPS = 1e-6
NEG_INF = -1e30
LOG2E = math.log2(math.e)

DA_HEADS = 4
DA_QK_DIM = 64
NSA_HEADS = 8
NSA_GROUPS = 2
NSA_HPG = NSA_HEADS // NSA_GROUPS
NSA_BLOCK = 64
BLOCK_SHIFT = 6
NSA_TOP_N = 8
NSA_WINDOW = 512
NSA_FORCED_SCORE = 1e4
FOX_HEADS = 4

ATT_TILE = 256
FOX_AUG = 2 * LANES
VT_ROWS = LANES + 16

BLK_AQ, BLK_AK, BLK_AV, BLK_BQ, BLK_BKV, BLK_CQ, BLK_CK, BLK_CV = 0, 4, 8, 12, 20, 32, 36, 40
QKV_COLS = 44 * LANES
SMALL_COLS = 3 * LANES

G_DAQ, G_DAK, G_NQ, G_NK0, G_NK1, G_NK2, G_FQ, G_FK, G_FBIAS, G_DAOUT = range(10)

VMEM_BYTES_V7X = 64 * 1024 * 1024
VMEM_LIMIT = VMEM_BYTES_V7X - 8 * 1024 * 1024
VMEM_LIMIT_FFN = VMEM_BYTES_V7X - 3 * 1024 * 1024


def _cparams(sem, limit=VMEM_LIMIT):
    return pltpu.CompilerParams(dimension_semantics=sem, vmem_limit_bytes=limit)


def _dot(a, b):
    return jnp.dot(a, b, preferred_element_type=F32)


def _dot_t(a, b):
    return lax.dot_general(a, b, (((1,), (1,)), ((), ())), preferred_element_type=F32)


def _rms_rows(xf, gain):
    return xf * lax.rsqrt(jnp.mean(xf * xf, axis=-1, keepdims=True) + EPS) * gain


def _ffn_kernel(x_ref, g_ref, wg_ref, wu_ref, wd_ref, *rest):
    n_cast = (len(rest) - 2) // 2
    o_ref, h_scr = rest[n_cast], rest[-1]

    @pl.when(pl.program_id(1) == 0)
    def _():
        xf = x_ref[...]
        h_scr[...] = _rms_rows(xf, g_ref[...]).astype(BF16)
        o_ref[...] = xf

    h = h_scr[...]
    g = _dot(h, wg_ref[...])
    u = _dot(h, wu_ref[...])
    a = (g * jax.nn.sigmoid(g)) * (0.5 * u)
    o_ref[...] += _dot(a.astype(BF16), wd_ref[...])

    for src, dst in zip(rest[:n_cast], rest[n_cast + 1:-1]):
        dst[...] = src[...].astype(BF16)


def _ffn(x2d, gain, wg, wu, wd, layer, cast_next=None, *, tm=1024, tf=512):
    T, D = x2d.shape
    F = wg.shape[2]
    tm = min(tm, T)
    ni, nj = T // tm, F // tf
    in_specs = [
        pl.BlockSpec((tm, D), lambda i, j: (i, 0)),
        pl.BlockSpec((None, 1, D), lambda i, j: (layer, 0, 0)),
        pl.BlockSpec((None, D, tf), lambda i, j: (0, 0, j)),
        pl.BlockSpec((None, D, tf), lambda i, j: (0, 0, j)),
        pl.BlockSpec((None, tf, D), lambda i, j: (0, j, 0)),
    ]
    out_specs = [pl.BlockSpec((tm, D), lambda i, j: (i, 0))]
    out_shape = [jax.ShapeDtypeStruct((T, D), F32)]
    args = [x2d, gain, wg, wu, wd]
    if cast_next is not None:
        nl, *nws = cast_next
        for w in nws:
            _, R, C = w.shape
            blk, imap = (((R // ni, C // nj), lambda i, j: (nl, i, j)) if C == F
                         else ((R // nj, C // ni), lambda i, j: (nl, j, i)))
            in_specs.append(pl.BlockSpec((None,) + blk, imap))
            omap = (lambda i, j: (0, i, j)) if C == F else (lambda i, j: (0, j, i))
            out_specs.append(pl.BlockSpec((None,) + blk, omap))
            out_shape.append(jax.ShapeDtypeStruct((1, R, C), BF16))
            args.append(w)
    outs = pl.pallas_call(
        _ffn_kernel,
        grid=(ni, nj),
        in_specs=in_specs,
        out_specs=tuple(out_specs),
        out_shape=tuple(out_shape),
        scratch_shapes=[pltpu.VMEM((tm, D), BF16)],
        compiler_params=_cparams(("parallel", "arbitrary"), VMEM_LIMIT_FFN),
        name="ffn",
    )(*args)
    return outs[0], tuple(outs[1:])


def _cast_kernel(x_ref, o_ref):
    o_ref[...] = x_ref[...].astype(BF16)


def _to_bf16(w, *, rows, layer=None):
    depth, R, C = w.shape
    first, count = (0, depth) if layer is None else (layer, 1)
    return pl.pallas_call(
        _cast_kernel,
        grid=(count, R // rows),
        in_specs=[pl.BlockSpec((None, rows, C), lambda l, i: (first + l, i, 0))],
        out_specs=pl.BlockSpec((None, rows, C), lambda l, i: (l, i, 0)),
        out_shape=jax.ShapeDtypeStruct((count, R, C), BF16),
        compiler_params=_cparams(("parallel", "parallel")),
        name="cast_bf16",
    )(w)


def _proj_kernel(x_ref, g_ref, w_ref, o_ref, h_scr, *, gate):
    @pl.when(pl.program_id(1) == 0)
    def _():
        h_scr[...] = _rms_rows(x_ref[...], g_ref[...]).astype(BF16)

    r = _dot(h_scr[...], w_ref[...])
    o_ref[...] = (jax.nn.sigmoid(r) if gate else r).astype(o_ref.dtype)


def _proj(x2d, gain, w, layer, *, gate, tm=1024, tn):
    T, D = x2d.shape
    N = w.shape[2]
    tm = min(tm, T)
    return pl.pallas_call(
        functools.partial(_proj_kernel, gate=gate),
        grid=(T // tm, pl.cdiv(N, tn)),
        in_specs=[
            pl.BlockSpec((tm, D), lambda i, j: (i, 0)),
            pl.BlockSpec((None, 1, D), lambda i, j: (layer, 0, 0)),
            pl.BlockSpec((None, D, tn), lambda i, j: (layer, 0, j)),
        ],
        out_specs=pl.BlockSpec((tm, tn), lambda i, j: (i, j)),
        out_shape=jax.ShapeDtypeStruct((T, N), BF16),
        scratch_shapes=[pltpu.VMEM((tm, D), BF16)],
        compiler_params=_cparams(("parallel", "arbitrary")),
        name="proj_gate" if gate else "proj_qkv",
    )(x2d, gain, w)


def _split3(v):
    hi = v.astype(BF16)
    r = v - hi.astype(F32)
    mid = r.astype(BF16)
    lo = (r - mid.astype(F32)).astype(BF16)
    return hi, mid, lo


def _proj_small_kernel(x_ref, g_ref, whi_ref, wlo_ref, o_ref, ot_ref):
    h = _rms_rows(x_ref[...], g_ref[...])
    hi, mid, _ = _split3(h)
    whi = whi_ref[...]
    r = _dot(hi, whi) + (_dot(mid, whi) + _dot(hi, wlo_ref[...]))
    o_ref[...] = r
    for c in range(r.shape[1] // LANES):
        ot_ref[c * LANES:(c + 1) * LANES, :] = r[:, c * LANES:(c + 1) * LANES].T


def _proj_small(x2d, gain, whi, wlo, layer, *, tm=512):
    T, D = x2d.shape
    N = whi.shape[2]
    return pl.pallas_call(
        _proj_small_kernel,
        grid=(T // tm,),
        in_specs=[
            pl.BlockSpec((tm, D), lambda i: (i, 0)),
            pl.BlockSpec((None, 1, D), lambda i: (layer, 0, 0)),
            pl.BlockSpec((None, D, N), lambda i: (layer, 0, 0)),
            pl.BlockSpec((None, D, N), lambda i: (layer, 0, 0)),
        ],
        out_specs=(pl.BlockSpec((tm, N), lambda i: (i, 0)), pl.BlockSpec((N, tm), lambda i: (0, i))),
        out_shape=(jax.ShapeDtypeStruct((T, N), F32), jax.ShapeDtypeStruct((N, T), F32)),
        compiler_params=_cparams(("parallel",)),
        name="proj_small",
    )(x2d, gain, whi, wlo)


def _prep_kernel(p_ref, sm_ref, rope_ref, gains_ref,
                 qat_ref, ka_ref, vat_ref, qbt_ref, ks_ref, kw_ref, vst_ref, vwt_ref,
                 qct_ref, kf_ref, vct_ref, kvc_ref,
                 carry_scr, *, tiles_per_seq):
    i = pl.program_id(0)
    tm = p_ref.shape[0]
    lane = lax.broadcasted_iota(jnp.int32, (tm, LANES), 1)
    src = lax.broadcasted_iota(jnp.int32, (LANES, LANES), 0)
    dst = lax.broadcasted_iota(jnp.int32, (LANES, LANES), 1)

    def const(cond, val=1.0):
        return jnp.where(cond, val, 0.0).astype(BF16)

    def group_mean(width):
        shift = width.bit_length() - 1
        return const((src >> shift) == (dst >> shift), 1.0 / width)

    def swap_halves(width, half):
        d = dst & (width - 1)
        return const(((d < half) & (src == dst + half)) | ((d >= half) & (d < 2 * half) & (src == dst - half)))

    mean128, mean64 = group_mean(HEAD_DIM), group_mean(DA_QK_DIM)
    eye = const(src == dst)

    def dot1(v, mat):
        return _dot(v.astype(BF16), mat)

    def raw(c):
        return p_ref[:, c * LANES:(c + 1) * LANES]

    def chunk(c):
        return raw(c).astype(F32)

    def gain(r):
        return gains_ref[r:r + 1, :]

    def rope_tab(r):
        return rope_ref[r]


    def tr(y):
        return _dot_t(eye, y.astype(BF16)).astype(BF16)

    def put_vt(ref, n, c):
        ref[n, 0, 0:LANES, :] = tr(raw(c))
        ref[n, 0, LANES:VT_ROWS, :] = jnp.ones((VT_ROWS - LANES, tm), BF16)

    half_a = swap_halves(DA_QK_DIM, DA_QK_DIM // ROPE_FRACTION // 2)
    half_b = swap_halves(HEAD_DIM, HEAD_DIM // ROPE_FRACTION // 2)
    scale_a = DA_QK_DIM ** -0.5 * LOG2E
    scale_b = HEAD_DIM ** -0.5 * LOG2E

    def run_group(specs):
        xs = [chunk(c) for c, *_ in specs]
        ms = [None if nm is None else dot1(x * x, nm[1]) for x, (_, nm, *_) in zip(xs, specs)]
        ys = [x if nm is None else x * lax.rsqrt(m + EPS) * gain(nm[0])
              for x, m, (_, nm, *_) in zip(xs, ms, specs)]
        ps = [None if rp is None else dot1(y, rp[1]) for y, (_, _, rp, *_) in zip(ys, specs)]
        rs = [y if rp is None else y * rope_tab(rp[0]) + p * rope_tab(rp[0] + 1)
              for y, p, (_, _, rp, *_) in zip(ys, ps, specs)]
        rs = [r if sc == 1.0 else r * sc for r, (_, _, _, sc, _, _) in zip(rs, specs)]
        outs = [tr(r) if tp else r.astype(BF16) for r, (_, _, _, _, tp, _) in zip(rs, specs)]
        for o, spec in zip(outs, specs):
            spec[-1](o)

    def nat_sink(ref, lo):
        def sink(o):
            ref[:, lo:lo + LANES] = o
        return sink

    def tile_sink(ref, n, rows=slice(None)):
        def sink(o):
            ref[n, 0, rows, :] = o
        return sink

    def grp_sink(n, g):
        def sink(o):
            kvc_ref[n, g] = o
        return sink

    rope_a, rope_b = (0, half_a), (2, half_b)
    run_group([(BLK_AQ + h, (G_DAQ, mean64), rope_a, scale_a, True, tile_sink(qat_ref, h))
               for h in range(DA_HEADS)]
              + [(BLK_AK + h, (G_DAK, mean64), rope_a, 1.0, False, nat_sink(ka_ref, h * LANES))
                 for h in range(DA_HEADS)])
    run_group([(BLK_BQ + h, (G_NQ, mean128), rope_b, scale_b, True, tile_sink(qbt_ref, h))
               for h in range(NSA_HEADS)])
    run_group([(BLK_BKV + 0 + g, None, rope_b, 1.0, False, grp_sink(0, g)) for g in range(NSA_GROUPS)]
              + [(BLK_BKV + 4 + g, (G_NK1, mean128), rope_b, 1.0, False, nat_sink(ks_ref, g * LANES))
                 for g in range(NSA_GROUPS)]
              + [(BLK_BKV + 8 + g, (G_NK2, mean128), rope_b, 1.0, False, nat_sink(kw_ref, g * LANES))
                 for g in range(NSA_GROUPS)])
    run_group([(BLK_CQ + h, (G_FQ, mean128), None, scale_b, True, tile_sink(qct_ref, h, slice(0, LANES)))
               for h in range(FOX_HEADS)]
              + [(BLK_CK + h, (G_FK, mean128), None, 1.0, False, nat_sink(kf_ref, h * FOX_AUG))
                 for h in range(FOX_HEADS)])
    for g in range(NSA_GROUPS):
        kvc_ref[1, g] = raw(BLK_BKV + 2 + g)
        put_vt(vst_ref, g, BLK_BKV + 6 + g)
        put_vt(vwt_ref, g, BLK_BKV + 10 + g)
    for h in range(DA_HEADS):
        put_vt(vat_ref, h, BLK_AV + h)
    for h in range(FOX_HEADS):
        put_vt(vct_ref, h, BLK_CV + h)

    @pl.when(i % tiles_per_seq == 0)
    def _():
        carry_scr[...] = jnp.zeros_like(carry_scr)

    z = sm_ref[...] + gain(G_FBIAS)
    logf = jnp.minimum(z, 0.0) - jnp.log(1.0 + jnp.exp(-jnp.abs(z)))
    r_id = lax.broadcasted_iota(jnp.int32, (tm, tm), 0)
    c_id = lax.broadcasted_iota(jnp.int32, (tm, tm), 1)
    tri = jnp.where(r_id >= c_id, 1.0, 0.0).astype(BF16)
    hi, mid, lo = _split3(logf)
    cum = _dot(tri, hi) + (_dot(tri, mid) + _dot(tri, lo)) + carry_scr[0:1, :]
    carry_scr[0:1, :] = cum[tm - 1:tm, :]

    for h in range(FOX_HEADS):
        c = jnp.broadcast_to(cum[:, h:h + 1], (tm, LANES)) * LOG2E
        c_hi, c_mid, c_lo = (v.astype(F32) for v in _split3(c))
        ones = jnp.where(lane < 6, 1.0, 0.0)
        q_aug = jnp.where(lane == 0, c_hi, jnp.where(lane == 1, c_mid, jnp.where(lane == 2, c_lo, ones)))
        k_aug = jnp.where(lane == 3, -c_hi, jnp.where(lane == 4, -c_mid, jnp.where(lane == 5, -c_lo, ones)))
        qct_ref[h, 0, LANES:FOX_AUG, :] = tr(q_aug)
        kf_ref[:, h * FOX_AUG + LANES:(h + 1) * FOX_AUG] = k_aug.astype(BF16)


def _prep(p_qkv, small, rope_tab, gains, *, seq):
    tm = ATT_TILE
    T = p_qkv.shape[0]
    nt = T // tm
    G = NSA_GROUPS

    def rows(w):
        return pl.BlockSpec((tm, w), lambda i: (i, 0))

    def tiles_t(n, d=LANES):
        return (jax.ShapeDtypeStruct((n, nt, d, tm), BF16), pl.BlockSpec((n, 1, d, tm), lambda i: (0, i, 0, 0)))

    def nat(w):
        return (jax.ShapeDtypeStruct((T, w), BF16), rows(w))

    def grp():
        return (jax.ShapeDtypeStruct((2, G, T, LANES), BF16),
                pl.BlockSpec((2, G, tm, LANES), lambda i: (0, 0, i, 0)))

    outs = [
        tiles_t(DA_HEADS), nat(DA_HEADS * LANES), tiles_t(DA_HEADS, VT_ROWS),
        tiles_t(NSA_HEADS), nat(G * LANES), nat(G * LANES),
        tiles_t(G, VT_ROWS), tiles_t(G, VT_ROWS),
        tiles_t(FOX_HEADS, FOX_AUG), nat(FOX_HEADS * FOX_AUG), tiles_t(FOX_HEADS, VT_ROWS),
        grp(),
    ]
    return pl.pallas_call(
        functools.partial(_prep_kernel, tiles_per_seq=seq // tm),
        grid=(nt,),
        in_specs=[
            rows(QKV_COLS),
            rows(LANES),
            pl.BlockSpec((4, tm, LANES), lambda i: (0, i, 0)),
            pl.BlockSpec(gains.shape, lambda i: (0, 0)),
        ],
        out_specs=tuple(o[1] for o in outs),
        out_shape=tuple(o[0] for o in outs),
        scratch_shapes=[pltpu.VMEM((8, LANES), F32)],
        compiler_params=_cparams(("arbitrary",)),
        name="prep",
    )(p_qkv, small, rope_tab, gains)


def _compress_kernel(x_ref, pos_ref, w1_ref, w2_ref, gain_ref, kc_ref, vct_ref, *, batch):
    def mlp(kv):
        blk = (x_ref[kv].astype(F32) + pos_ref[kv]).astype(BF16)
        hid = jax.nn.gelu(_dot(blk, w1_ref[kv].astype(BF16)))
        return _dot(hid.astype(BF16), w2_ref[kv].astype(BF16))

    kc = _rms_rows(mlp(0), gain_ref[...])
    vc = mlp(1)
    nb = kc.shape[0] // batch
    pad = jnp.zeros((LANES - nb, LANES), F32)
    for b in range(batch):
        kc_ref[b] = jnp.concatenate([kc[b * nb:(b + 1) * nb], pad], axis=0).astype(BF16)
        vct_ref[b] = jnp.concatenate([vc[b * nb:(b + 1) * nb], pad], axis=0).T.astype(BF16)


def _compress(xkv, pos, w1, w2, gain, layer, *, batch):
    _, G, R, K = xkv.shape
    d = w1.shape[-1]
    out = jax.ShapeDtypeStruct((G, batch, LANES, LANES), BF16)
    ospec = pl.BlockSpec((None, batch, LANES, LANES), lambda g: (g, 0, 0, 0))
    return pl.pallas_call(
        functools.partial(_compress_kernel, batch=batch),
        grid=(G,),
        in_specs=[
            pl.BlockSpec((2, None, R, K), lambda g: (0, g, 0, 0)),
            pl.BlockSpec((None, 2, 1, K), lambda g: (layer, 0, 0, 0)),
            pl.BlockSpec((None, 2, K, d), lambda g: (layer, 0, 0, 0)),
            pl.BlockSpec((None, 2, d, d), lambda g: (layer, 0, 0, 0)),
            pl.BlockSpec((1, d), lambda g: (0, 0)),
        ],
        out_specs=(ospec, ospec),
        out_shape=(out, out),
        compiler_params=_cparams(("arbitrary",)),
        name="compress",
    )(xkv, pos, w1, w2, gain)


def _online_t(s_ts, mask, carry, v_ts):
    masks = mask if isinstance(mask, (list, tuple)) else [mask] * len(s_ts)
    stats = []
    for s_t, (m, _), mk in zip(s_ts, carry, masks):
        if mk is not None:
            s_t = jnp.where(mk, s_t, NEG_INF)
        m_new = jnp.maximum(m, jnp.max(s_t, axis=0, keepdims=True))
        stats.append((m_new, jnp.exp2(m - m_new), jnp.exp2(s_t - m_new).astype(BF16)))
    return tuple((m_new, alpha * acc_t + _dot(v_t, p))
                 for (m_new, alpha, p), (_, acc_t), v_t in zip(stats, carry, v_ts))


def _attend(scores, values, loop_mask, last_mask, kt0, kt_last, n):
    t = ATT_TILE
    init = tuple((jnp.full((1, t), NEG_INF, F32), jnp.zeros((VT_ROWS, t), F32)) for _ in range(n))

    def body(kt, carry):
        mask = None if loop_mask is None else loop_mask(kt)
        return _online_t(scores(kt), mask, carry, values(kt))

    carry = lax.fori_loop(kt0, kt_last, body, init)
    return _normalise(_online_t(scores(kt_last), last_mask, carry, values(kt_last)))


def _normalise(carry):
    return [acc[0:LANES] / acc[LANES:LANES + 1] for (_, acc) in carry]


def _causal_mask_t(t):
    return lax.broadcasted_iota(jnp.int32, (t, t), 0) <= lax.broadcasted_iota(jnp.int32, (t, t), 1)


def _diff_kernel(qt_ref, k_ref, vt_ref, lam_ref, gains_ref, o_ref, *, lam_init):
    i = pl.program_id(2)
    t = ATT_TILE
    nh = qt_ref.shape[0]
    drow = lax.broadcasted_iota(jnp.int32, (LANES, t), 0)
    qts = []
    for h in range(nh):
        q = qt_ref[h, 0]
        qts.append(jnp.where(drow < DA_QK_DIM, q, jnp.zeros_like(q)))
        qts.append(jnp.where(drow < DA_QK_DIM, jnp.zeros_like(q), q))

    def scores(kt):
        off = pl.multiple_of(kt * t, t)
        ks = [k_ref[pl.ds(off, t), h * LANES:(h + 1) * LANES] for h in range(nh)]
        return tuple(_dot(ks[c // 2], qts[c]) for c in range(2 * nh))

    def values(kt):
        return [vt_ref[c // 2, kt] for c in range(2 * nh)]

    outs = _attend(scores, values, None, _causal_mask_t(t), 0, i, 2 * nh)

    lp = lam_ref[...]
    lam = (jnp.exp(jnp.sum(lp[0:1] * lp[1:2], axis=-1, keepdims=True))
           - jnp.exp(jnp.sum(lp[2:3] * lp[3:4], axis=-1, keepdims=True)) + lam_init)
    for h in range(nh):
        o = (outs[2 * h] - lam * outs[2 * h + 1]).T
        o = _rms_rows(o, gains_ref[G_DAOUT:G_DAOUT + 1, :]) * (1.0 - lam_init)
        o_ref[:, h * LANES:(h + 1) * LANES] = o.astype(BF16)


def _diff_attention(qat, ka, vat, da_lambda, gains, *, batch, seq, lam_init, heads_per_step=4):
    t = ATT_TILE
    nq = seq // t
    T = batch * seq
    hp = heads_per_step
    return pl.pallas_call(
        functools.partial(_diff_kernel, lam_init=lam_init),
        grid=(batch, DA_HEADS // hp, nq),
        in_specs=[
            pl.BlockSpec((hp, 1, LANES, t), lambda b, p, i: (p, b * nq + i, 0, 0)),
            pl.BlockSpec((seq, hp * LANES), lambda b, p, i: (b, p)),
            pl.BlockSpec((hp, nq, VT_ROWS, t), lambda b, p, i: (p, b, 0, 0)),
            pl.BlockSpec(da_lambda.shape, lambda b, p, i: (0, 0)),
            pl.BlockSpec(gains.shape, lambda b, p, i: (0, 0)),
        ],
        out_specs=pl.BlockSpec((t, hp * LANES), lambda b, p, i: (b * nq + i, p)),
        out_shape=jax.ShapeDtypeStruct((T, DA_HEADS * LANES), BF16),
        compiler_params=_cparams(("parallel", "parallel", "arbitrary")),
        name="diff_attn",
    )(qat, ka, vat, da_lambda, gains)


def _fox_kernel(qt_ref, k_ref, vt_ref, o_ref):
    i = pl.program_id(1)
    t = ATT_TILE
    nh, bs = qt_ref.shape[0], qt_ref.shape[1]
    chains = [(b, h) for b in range(bs) for h in range(nh)]
    qts = [qt_ref[h, b, 0] for b, h in chains]

    def scores(kt):
        off = pl.multiple_of(kt * t, t)
        return tuple(_dot(k_ref[b, pl.ds(off, t), h * FOX_AUG:(h + 1) * FOX_AUG], q)
                     for (b, h), q in zip(chains, qts))

    def values(kt):
        return [vt_ref[h, b, kt] for b, h in chains]

    outs = _attend(scores, values, None, _causal_mask_t(t), 0, i, len(chains))
    for (b, h), o in zip(chains, outs):
        o_ref[b, :, h * LANES:(h + 1) * LANES] = o.T.astype(BF16)


def _fox_attention(qct, kf, vct, *, batch, seq):
    t = ATT_TILE
    nq = seq // t
    nh = FOX_HEADS
    bs = 2 if batch % 2 == 0 else 1
    nbp = batch // bs
    out = pl.pallas_call(
        _fox_kernel,
        grid=(nbp, nq),
        in_specs=[
            pl.BlockSpec((nh, None, bs, 1, FOX_AUG, t), lambda p, i: (0, p, 0, i, 0, 0)),
            pl.BlockSpec((None, bs, seq, nh * FOX_AUG), lambda p, i: (p, 0, 0, 0)),
            pl.BlockSpec((nh, None, bs, nq, VT_ROWS, t), lambda p, i: (0, p, 0, 0, 0, 0)),
        ],
        out_specs=pl.BlockSpec((None, bs, t, nh * LANES), lambda p, i: (p, 0, i, 0)),
        out_shape=jax.ShapeDtypeStruct((nbp, bs, seq, nh * LANES), BF16),
        compiler_params=_cparams(("parallel", "arbitrary")),
        name="fox_attn",
    )(qct.reshape(nh, nbp, bs, nq, FOX_AUG, t), kf.reshape(nbp, bs, seq, nh * FOX_AUG),
      vct.reshape(nh, nbp, bs, nq, VT_ROWS, t))
    return out.reshape(batch * seq, nh * LANES)


def _nsa_kernel(qt_ref, ks_ref, vst_ref, kw_ref, vwt_ref, kc_ref, vct_ref, gt_ref, o_ref, *, n_sel, nb):
    i = pl.program_id(1)
    t = ATT_TILE
    lo = i * t
    nh = qt_ref.shape[0]
    hpg = nh // NSA_GROUPS
    qts = [qt_ref[h, 0] for h in range(nh)]
    tq_row = lo + lax.broadcasted_iota(jnp.int32, (1, t), 1)
    nidx = lax.broadcasted_iota(jnp.int32, (nb, t), 0)
    nidx_f = nidx.astype(F32)
    cmask = nidx * NSA_BLOCK + (NSA_BLOCK - 1) <= tq_row
    cur = tq_row >> BLOCK_SHIFT
    forced = (nidx == 0) | (nidx == cur) | (nidx == cur - 1)
    zpad = jnp.zeros((LANES - nb, t), F32)

    def lane_tile(v):
        return jnp.concatenate([v, zpad], axis=0).astype(BF16)

    o_cmp, sel_b = [], []
    for g in range(NSA_GROUPS):
        kc, vct = kc_ref[g, 0:nb, :], vct_ref[g]
        importance = jnp.zeros((nb, t), F32)
        for h in range(g * hpg, (g + 1) * hpg):
            s = jnp.where(cmask, _dot(kc, qts[h]), NEG_INF)
            e = jnp.where(cmask, jnp.exp2(s - jnp.max(s, axis=0, keepdims=True)), 0.0)
            p = e / jnp.maximum(jnp.sum(e, axis=0, keepdims=True), 1e-30)
            importance = importance + p
            o_cmp.append(_dot(vct, lane_tile(p)))

        score = jnp.where(forced, NSA_FORCED_SCORE, jnp.where(nidx <= cur, importance, -1.0))
        sel = jnp.zeros((nb, t), F32)
        for _ in range(n_sel):
            best = jnp.max(score, axis=0, keepdims=True)
            first = jnp.min(jnp.where(score == best, nidx_f, float(nb)), axis=0, keepdims=True)
            hit = nidx_f == first
            sel = jnp.where(hit, 1.0, sel)
            score = jnp.where(hit, -3e38, score)
        sel_b.append(lane_tile(sel))

    e_key = lax.broadcasted_iota(jnp.int32, (t, LANES), 0) >> BLOCK_SHIFT
    e_blk = lax.broadcasted_iota(jnp.int32, (t, LANES), 1)
    krow = lax.broadcasted_iota(jnp.int32, (t, t), 0)
    qcol = lax.broadcasted_iota(jnp.int32, (t, t), 1)
    causal = krow <= qcol

    def scores_of(k_ref):
        def scores(kt):
            off = pl.multiple_of(kt * t, t)
            ks = [k_ref[pl.ds(off, t), g * LANES:(g + 1) * LANES] for g in range(NSA_GROUPS)]
            return tuple(_dot(ks[h // hpg], qts[h]) for h in range(nh))
        return scores

    def values_of(vt_ref):
        return lambda kt: [vt_ref[h // hpg, kt] for h in range(nh)]

    def selected(kt, extra=None):
        expand = jnp.where(e_blk == e_key + kt * (t // NSA_BLOCK), 1.0, 0.0).astype(BF16)
        per_group = [_dot(expand, sb) > 0.5 for sb in sel_b]
        if extra is not None:
            per_group = [m & extra for m in per_group]
        return [per_group[h // hpg] for h in range(nh)]

    o_slc = _attend(scores_of(ks_ref), values_of(vst_ref), selected, selected(i, causal), 0, i, nh)

    wt = NSA_WINDOW // t
    win_scores, win_values = scores_of(kw_ref), values_of(vwt_ref)
    upper = krow > qcol
    far = jnp.maximum(i - wt, 0)
    penalty = jnp.where(i >= wt, 0.0, NEG_INF)
    s_far, s_diag = win_scores(far), win_scores(i)
    stats = []
    for c in range(nh):
        s = jnp.where(upper, s_far[c] + penalty, s_diag[c])
        m = jnp.max(s, axis=0, keepdims=True)
        stats.append((m, jnp.exp2(s - m).astype(BF16)))
    v_far, v_diag = win_values(far), win_values(i)
    carry = tuple((m, _dot(v_far[c], jnp.where(upper, p, jnp.zeros_like(p)))
                   + _dot(v_diag[c], jnp.where(upper, jnp.zeros_like(p), p)))
                  for c, (m, p) in enumerate(stats))
    carry = lax.fori_loop(jnp.maximum(i - wt + 1, 0), i,
                          lambda kt, cr: _online_t(win_scores(kt), None, cr, win_values(kt)), carry)
    o_win = _normalise(carry)

    for h in range(nh):
        g, r = h // hpg, 3 * (h % hpg)
        gt = jax.nn.sigmoid(gt_ref[(1 + g) * LANES + r:(1 + g) * LANES + r + 3, :])
        o = gt[0:1] * o_cmp[h] + gt[1:2] * o_slc[h] + gt[2:3] * o_win[h]
        o_ref[:, h * LANES:(h + 1) * LANES] = o.T.astype(BF16)


def _nsa_attention(qbt, ks, kw, vst, vwt, kc, vct, small_t, *, batch, seq):
    t = ATT_TILE
    nq = seq // t
    T = batch * seq
    n_sel = min(NSA_TOP_N, seq // NSA_BLOCK)
    G = NSA_GROUPS
    k_spec = pl.BlockSpec((seq, G * LANES), lambda b, i: (b, 0))
    vt_spec = pl.BlockSpec((G, nq, VT_ROWS, t), lambda b, i: (0, b, 0, 0))
    c_spec = pl.BlockSpec((G, None, LANES, LANES), lambda b, i: (0, b, 0, 0))
    return pl.pallas_call(
        functools.partial(_nsa_kernel, n_sel=n_sel, nb=seq // NSA_BLOCK),
        grid=(batch, nq),
        in_specs=[
            pl.BlockSpec((NSA_HEADS, 1, LANES, t), lambda b, i: (0, b * nq + i, 0, 0)),
            k_spec, vt_spec, k_spec, vt_spec, c_spec, c_spec,
            pl.BlockSpec((SMALL_COLS, t), lambda b, i: (0, b * nq + i)),
        ],
        out_specs=pl.BlockSpec((t, NSA_HEADS * LANES), lambda b, i: (b * nq + i, 0)),
        out_shape=jax.ShapeDtypeStruct((T, NSA_HEADS * LANES), BF16),
        compiler_params=_cparams(("parallel", "arbitrary")),
        name="nsa_attn",
    )(qbt, ks, vst, kw, vwt, kc, vct, small_t)


def _merge_kernel(x_ref, oa_ref, ob_ref, oc_ref, gm_ref, wa_ref, wb_ref, wc_ref, wo_ref, o_ref, *, gm_off):
    D = x_ref.shape[1]
    g0, g1, g2 = (gm_ref[:, gm_off + n * D:gm_off + (n + 1) * D].astype(F32) for n in range(3))
    y = (g0 * _dot(oa_ref[...], wa_ref[...]) + g1 * _dot(ob_ref[...], wb_ref[...])
         + g2 * _dot(oc_ref[...], wc_ref[...]))
    o_ref[...] = x_ref[...] + _dot(y.astype(BF16), wo_ref[...])


def _merge(x2d, oa, ob, oc, p_gm, gm_off, wa, wb, wc, wo, layer, *, tm=512):
    T, D = x2d.shape

    def rows(w, c=0):
        return pl.BlockSpec((tm, w), lambda i: (i, c))

    def full(w):
        return pl.BlockSpec((None,) + w.shape[1:], lambda i: (layer, 0, 0), pipeline_mode=pl.Buffered(1))

    return pl.pallas_call(
        functools.partial(_merge_kernel, gm_off=gm_off),
        grid=(T // tm,),
        in_specs=[rows(D), rows(oa.shape[1]), rows(ob.shape[1]), rows(oc.shape[1]),
                  rows(p_gm.shape[1]), full(wa), full(wb), full(wc), full(wo)],
        out_specs=rows(D),
        out_shape=jax.ShapeDtypeStruct((T, D), F32),
        compiler_params=_cparams(("parallel",), VMEM_LIMIT_FFN),
        name="merge",
    )(x2d, oa, ob, oc, p_gm, wa, wb, wc, wo)


def _rope_tables(positions, width, rot_dim):
    inv_freq = ROPE_THETA ** (-jnp.arange(0, rot_dim, 2, dtype=F32) / rot_dim)
    ang = positions.astype(F32).reshape(-1, 1) * inv_freq
    cos, sin = jnp.cos(ang), jnp.sin(ang)
    T = ang.shape[0]
    rest = width - rot_dim
    c = jnp.concatenate([cos, cos, jnp.ones((T, rest), F32)], axis=1)
    s = jnp.concatenate([-sin, sin, jnp.zeros((T, rest), F32)], axis=1)
    rep = LANES // width
    return [jnp.tile(t, (1, rep)) for t in (c, s)]


def _pad_lanes(v):
    return jnp.pad(v.astype(F32), (0, LANES - v.shape[0]))


def kernel(x, positions, ffn1_norm, ffn1_w_gate, ffn1_w_up, ffn1_w_down, mix_norm, w_in, da_q_norm, da_k_norm, da_lambda, da_out_norm, nsa_q_norm, nsa_k_norm, nsa_cmp_pos, nsa_cmp_w1, nsa_cmp_w2, fox_q_norm, fox_k_norm, fox_f_bias, w_branch_a, w_branch_b, w_branch_c, w_out, ffn2_norm, ffn2_w_gate, ffn2_w_up, ffn2_w_down):
    B, S, D = x.shape
    depth = w_in.shape[0]
    T = B * S
    G, L, d = NSA_GROUPS, NSA_BLOCK, HEAD_DIM
    nb = S // L
    assert S % ATT_TILE == 0 and nb <= LANES and nb % 8 == 0 and NSA_WINDOW % ATT_TILE == 0

    rope_tab = jnp.stack(_rope_tables(positions, DA_QK_DIM, DA_QK_DIM // ROPE_FRACTION)
                         + _rope_tables(positions, HEAD_DIM, HEAD_DIM // ROPE_FRACTION))

    widths = [512, 512, 512, 1024, 1536, 24, 512, 512, 512, 4, 3 * D]
    offs = [0]
    for w in widths:
        offs.append(offs[-1] + w)
    (o_aq, o_ak, o_av, o_bq, o_bkv, o_bg, o_cq, o_ck, o_cv, o_cf, o_gm, o_end) = offs

    w_qkv = jnp.concatenate([w_in[:, :, o_aq:o_bg], w_in[:, :, o_cq:o_cf]], axis=2).astype(BF16)
    gm_lo = o_gm // LANES * LANES
    gm_off = o_gm - gm_lo
    gm_tile = 10 * LANES
    gm_cols = -(-(o_end - gm_lo) // LANES) * LANES
    w_gm = jnp.pad(w_in[:, :, gm_lo:o_end], ((0, 0), (0, 0), (0, gm_cols - (o_end - gm_lo)))).astype(BF16)
    zpad = lambda n: jnp.zeros((depth, D, n), F32)
    w_small = jnp.concatenate([
        w_in[:, :, o_cf:o_gm], zpad(LANES - FOX_HEADS),
        w_in[:, :, o_bg:o_bg + 12], zpad(LANES - 12),
        w_in[:, :, o_bg + 12:o_bg + 24], zpad(LANES - 12)], axis=2)
    w_small_hi = w_small.astype(BF16)
    w_small_lo = (w_small - w_small_hi.astype(F32)).astype(BF16)
    ffn_f32 = [(ffn1_w_gate, ffn1_w_up, ffn1_w_down), (ffn2_w_gate, ffn2_w_up, ffn2_w_down)]
    ffn_g = [ffn1_norm.reshape(depth, 1, D), ffn2_norm.reshape(depth, 1, D)]
    ffn_next = [(k // 2, *ffn_f32[k % 2]) for k in range(1, 2 * depth)] + [None]
    ffn_w = tuple(_to_bf16(w, rows=256, layer=0) for w in ffn_f32[0])
    g_mix = mix_norm.reshape(depth, 1, D)
    w_a, w_b, w_c, w_o = (_to_bf16(w, rows=512) for w in (w_branch_a, w_branch_b, w_branch_c, w_out))
    cmp_pos = nsa_cmp_pos.reshape(depth, 2, 1, L * d)

    xc = x.reshape(T, D)
    for l in range(depth):
        lam_init = 0.8 - 0.6 * math.exp(-0.3 * l)
        gains = jnp.stack([
            jnp.tile(da_q_norm[l], 2), jnp.tile(da_k_norm[l], 2), nsa_q_norm[l],
            nsa_k_norm[l, 0], nsa_k_norm[l, 1], nsa_k_norm[l, 2], fox_q_norm[l], fox_k_norm[l],
            _pad_lanes(fox_f_bias[l]), da_out_norm[l]] + [jnp.zeros((LANES,), F32)] * 6).astype(F32)

        xc, ffn_w = _ffn(xc, ffn_g[0], *ffn_w, l, ffn_next[2 * l])

        p_qkv = _proj(xc, g_mix, w_qkv, l, gate=False, tn=QKV_COLS // 4)
        p_gm = _proj(xc, g_mix, w_gm, l, gate=True, tn=gm_tile)
        small, small_t = _proj_small(xc, g_mix, w_small_hi, w_small_lo, l)

        (qat, ka, vat, qbt, ks, kw, vst, vwt, qct, kf, vct, kv_cmp) = _prep(
            p_qkv, small, rope_tab, gains, seq=S)

        kc, vcmp_t = _compress(kv_cmp.reshape(2, G, B * nb, L * d), cmp_pos, nsa_cmp_w1, nsa_cmp_w2,
                               nsa_k_norm[l, 0].reshape(1, d), l, batch=B)

        o_a = _diff_attention(qat, ka, vat, da_lambda[l], gains, batch=B, seq=S, lam_init=lam_init)
        o_b = _nsa_attention(qbt, ks, kw, vst, vwt, kc, vcmp_t, small_t, batch=B, seq=S)
        o_c = _fox_attention(qct, kf, vct, batch=B, seq=S)

        xc = _merge(xc, o_a, o_b, o_c, p_gm, gm_off, w_a, w_b, w_c, w_o, l)

        xc, ffn_w = _ffn(xc, ffn_g[1], *ffn_w, l, ffn_next[2 * l + 1])
    return xc.reshape(B, S, D)
```

```python
import functools
import math

import jax
import jax.numpy as jnp
from jax import lax
from jax.experimental import pallas as pl
from jax.experimental.pallas import tpu as pltpu

F32 = jnp.float32
BF16 = jnp.bfloat16

HEAD_DIM = 128
LANES = 128
MXU_COLS_V7X = 256
ROPE_THETA = 500000.0
ROPE_FRACTION = 4
EPS = 1e-6
NEG_INF = -1e30
LOG2E = math.log2(math.e)

DA_HEADS = 4
DA_QK_DIM = 64
NSA_HEADS = 8
NSA_GROUPS = 2
NSA_HPG = NSA_HEADS // NSA_GROUPS
NSA_BLOCK = 64
BLOCK_SHIFT = 6
NSA_TOP_N = 8
NSA_WINDOW = 512
NSA_FORCED_SCORE = 1e4
FOX_HEADS = 4

ATT_TILE = 256
FOX_AUG = 2 * LANES
VT_ROWS = LANES + 16

BLK_AQ, BLK_AK, BLK_AV, BLK_BQ, BLK_BKV, BLK_CQ, BLK_CK, BLK_CV = 0, 4, 8, 12, 20, 32, 36, 40
QKV_COLS = 44 * LANES
SMALL_COLS = 3 * LANES

G_DAQ, G_DAK, G_NQ, G_NK0, G_NK1, G_NK2, G_FQ, G_FK, G_FBIAS, G_DAOUT = range(10)

VMEM_BYTES_V7X = 64 * 1024 * 1024
VMEM_LIMIT = VMEM_BYTES_V7X - 8 * 1024 * 1024
VMEM_LIMIT_FFN = VMEM_BYTES_V7X - 3 * 1024 * 1024


def _cparams(sem, limit=VMEM_LIMIT):
    return pltpu.CompilerParams(dimension_semantics=sem, vmem_limit_bytes=limit)


def _dot(a, b):
    return jnp.dot(a, b, preferred_element_type=F32)


def _dot_t(a, b):
    return lax.dot_general(a, b, (((1,), (1,)), ((), ())), preferred_element_type=F32)


def _rms_rows(xf, gain):
    return xf * lax.rsqrt(jnp.mean(xf * xf, axis=-1, keepdims=True) + EPS) * gain


def _ffn_kernel(x_ref, g_ref, wg_ref, wu_ref, wd_ref, *rest):
    n_cast = (len(rest) - 2) // 2
    o_ref, h_scr = rest[n_cast], rest[-1]

    @pl.when(pl.program_id(1) == 0)
    def _():
        xf = x_ref[...]
        h_scr[...] = _rms_rows(xf, g_ref[...]).astype(BF16)
        o_ref[...] = xf

    h = h_scr[...]
    g = _dot(h, wg_ref[...])
    u = _dot(h, wu_ref[...])
    a = (g * jax.nn.sigmoid(g)) * (0.5 * u)
    o_ref[...] += _dot(a.astype(BF16), wd_ref[...])

    for src, dst in zip(rest[:n_cast], rest[n_cast + 1:-1]):
        dst[...] = src[...].astype(BF16)


def _ffn(x2d, gain, wg, wu, wd, layer, cast_next=None, *, tm=1024, tf=512):
    T, D = x2d.shape
    F = wg.shape[2]
    tm = min(tm, T)
    ni, nj = T // tm, F // tf
    in_specs = [
        pl.BlockSpec((tm, D), lambda i, j: (i, 0)),
        pl.BlockSpec((None, 1, D), lambda i, j: (layer, 0, 0)),
        pl.BlockSpec((None, D, tf), lambda i, j: (0, 0, j)),
        pl.BlockSpec((None, D, tf), lambda i, j: (0, 0, j)),
        pl.BlockSpec((None, tf, D), lambda i, j: (0, j, 0)),
    ]
    out_specs = [pl.BlockSpec((tm, D), lambda i, j: (i, 0))]
    out_shape = [jax.ShapeDtypeStruct((T, D), F32)]
    args = [x2d, gain, wg, wu, wd]
    if cast_next is not None:
        nl, *nws = cast_next
        for w in nws:
            _, R, C = w.shape
            blk, imap = (((R // ni, C // nj), lambda i, j: (nl, i, j)) if C == F
                         else ((R // nj, C // ni), lambda i, j: (nl, j, i)))
            in_specs.append(pl.BlockSpec((None,) + blk, imap))
            omap = (lambda i, j: (0, i, j)) if C == F else (lambda i, j: (0, j, i))
            out_specs.append(pl.BlockSpec((None,) + blk, omap))
            out_shape.append(jax.ShapeDtypeStruct((1, R, C), BF16))
            args.append(w)
    outs = pl.pallas_call(
        _ffn_kernel,
        grid=(ni, nj),
        in_specs=in_specs,
        out_specs=tuple(out_specs),
        out_shape=tuple(out_shape),
        scratch_shapes=[pltpu.VMEM((tm, D), BF16)],
        compiler_params=_cparams(("parallel", "arbitrary"), VMEM_LIMIT_FFN),
        name="ffn",
    )(*args)
    return outs[0], tuple(outs[1:])


def _cast_kernel(x_ref, o_ref):
    o_ref[...] = x_ref[...].astype(BF16)


def _to_bf16(w, *, rows, layer=None):
    depth, R, C = w.shape
    first, count = (0, depth) if layer is None else (layer, 1)
    return pl.pallas_call(
        _cast_kernel,
        grid=(count, R // rows),
        in_specs=[pl.BlockSpec((None, rows, C), lambda l, i: (first + l, i, 0))],
        out_specs=pl.BlockSpec((None, rows, C), lambda l, i: (l, i, 0)),
        out_shape=jax.ShapeDtypeStruct((count, R, C), BF16),
        compiler_params=_cparams(("parallel", "parallel")),
        name="cast_bf16",
    )(w)


def _proj_kernel(x_ref, g_ref, w_ref, o_ref, h_scr, *, gate):
    @pl.when(pl.program_id(1) == 0)
    def _():
        h_scr[...] = _rms_rows(x_ref[...], g_ref[...]).astype(BF16)

    r = _dot(h_scr[...], w_ref[...])
    o_ref[...] = (jax.nn.sigmoid(r) if gate else r).astype(o_ref.dtype)


def _proj(x2d, gain, w, layer, *, gate, tm=1024, tn):
    T, D = x2d.shape
    N = w.shape[2]
    tm = min(tm, T)
    return pl.pallas_call(
        functools.partial(_proj_kernel, gate=gate),
        grid=(T // tm, pl.cdiv(N, tn)),
        in_specs=[
            pl.BlockSpec((tm, D), lambda i, j: (i, 0)),
            pl.BlockSpec((None, 1, D), lambda i, j: (layer, 0, 0)),
            pl.BlockSpec((None, D, tn), lambda i, j: (layer, 0, j)),
        ],
        out_specs=pl.BlockSpec((tm, tn), lambda i, j: (i, j)),
        out_shape=jax.ShapeDtypeStruct((T, N), BF16),
        scratch_shapes=[pltpu.VMEM((tm, D), BF16)],
        compiler_params=_cparams(("parallel", "arbitrary")),
        name="proj_gate" if gate else "proj_qkv",
    )(x2d, gain, w)


def _split3(v):
    hi = v.astype(BF16)
    r = v - hi.astype(F32)
    mid = r.astype(BF16)
    lo = (r - mid.astype(F32)).astype(BF16)
    return hi, mid, lo


def _proj_small_kernel(x_ref, g_ref, whi_ref, wlo_ref, o_ref, ot_ref):
    h = _rms_rows(x_ref[...], g_ref[...])
    hi, mid, _ = _split3(h)
    whi = whi_ref[...]
    r = _dot(hi, whi) + (_dot(mid, whi) + _dot(hi, wlo_ref[...]))
    o_ref[...] = r
    for c in range(r.shape[1] // LANES):
        ot_ref[c * LANES:(c + 1) * LANES, :] = r[:, c * LANES:(c + 1) * LANES].T


def _proj_small(x2d, gain, whi, wlo, layer, *, tm=512):
    T, D = x2d.shape
    N = whi.shape[2]
    return pl.pallas_call(
        _proj_small_kernel,
        grid=(T // tm,),
        in_specs=[
            pl.BlockSpec((tm, D), lambda i: (i, 0)),
            pl.BlockSpec((None, 1, D), lambda i: (layer, 0, 0)),
            pl.BlockSpec((None, D, N), lambda i: (layer, 0, 0)),
            pl.BlockSpec((None, D, N), lambda i: (layer, 0, 0)),
        ],
        out_specs=(pl.BlockSpec((tm, N), lambda i: (i, 0)), pl.BlockSpec((N, tm), lambda i: (0, i))),
        out_shape=(jax.ShapeDtypeStruct((T, N), F32), jax.ShapeDtypeStruct((N, T), F32)),
        compiler_params=_cparams(("parallel",)),
        name="proj_small",
    )(x2d, gain, whi, wlo)


def _prep_kernel(p_ref, sm_ref, rope_ref, gains_ref,
                 qat_ref, ka_ref, vat_ref, qbt_ref, ks_ref, kw_ref, vst_ref, vwt_ref,
                 qct_ref, kf_ref, vct_ref, kvc_ref,
                 carry_scr, *, tiles_per_seq):
    i = pl.program_id(0)
    tm = p_ref.shape[0]
    lane = lax.broadcasted_iota(jnp.int32, (tm, LANES), 1)
    src = lax.broadcasted_iota(jnp.int32, (LANES, LANES), 0)
    dst = lax.broadcasted_iota(jnp.int32, (LANES, LANES), 1)

    def const(cond, val=1.0):
        return jnp.where(cond, val, 0.0).astype(BF16)

    def group_mean(width):
        shift = width.bit_length() - 1
        return const((src >> shift) == (dst >> shift), 1.0 / width)

    def swap_halves(width, half):
        d = dst & (width - 1)
        return const(((d < half) & (src == dst + half)) | ((d >= half) & (d < 2 * half) & (src == dst - half)))

    mean128, mean64 = group_mean(HEAD_DIM), group_mean(DA_QK_DIM)
    eye = const(src == dst)

    def dot1(v, mat):
        return _dot(v.astype(BF16), mat)

    def raw(c):
        return p_ref[:, c * LANES:(c + 1) * LANES]

    def chunk(c):
        return raw(c).astype(F32)

    def gain(r):
        return gains_ref[r:r + 1, :]

    def rope_tab(r):
        return rope_ref[r]


    def tr(y):
        return _dot_t(eye, y.astype(BF16)).astype(BF16)

    def put_vt(ref, n, c):
        ref[n, 0, 0:LANES, :] = tr(raw(c))
        ref[n, 0, LANES:VT_ROWS, :] = jnp.ones((VT_ROWS - LANES, tm), BF16)

    half_a = swap_halves(DA_QK_DIM, DA_QK_DIM // ROPE_FRACTION // 2)
    half_b = swap_halves(HEAD_DIM, HEAD_DIM // ROPE_FRACTION // 2)
    scale_a = DA_QK_DIM ** -0.5 * LOG2E
    scale_b = HEAD_DIM ** -0.5 * LOG2E

    def run_group(specs):
        xs = [chunk(c) for c, *_ in specs]
        ms = [None if nm is None else dot1(x * x, nm[1]) for x, (_, nm, *_) in zip(xs, specs)]
        ys = [x if nm is None else x * lax.rsqrt(m + EPS) * gain(nm[0])
              for x, m, (_, nm, *_) in zip(xs, ms, specs)]
        ps = [None if rp is None else dot1(y, rp[1]) for y, (_, _, rp, *_) in zip(ys, specs)]
        rs = [y if rp is None else y * rope_tab(rp[0]) + p * rope_tab(rp[0] + 1)
              for y, p, (_, _, rp, *_) in zip(ys, ps, specs)]
        rs = [r if sc == 1.0 else r * sc for r, (_, _, _, sc, _, _) in zip(rs, specs)]
        outs = [tr(r) if tp else r.astype(BF16) for r, (_, _, _, _, tp, _) in zip(rs, specs)]
        for o, spec in zip(outs, specs):
            spec[-1](o)

    def nat_sink(ref, lo):
        def sink(o):
            ref[:, lo:lo + LANES] = o
        return sink

    def tile_sink(ref, n, rows=slice(None)):
        def sink(o):
            ref[n, 0, rows, :] = o
        return sink

    def grp_sink(n, g):
        def sink(o):
            kvc_ref[n, g] = o
        return sink

    rope_a, rope_b = (0, half_a), (2, half_b)
    run_group([(BLK_AQ + h, (G_DAQ, mean64), rope_a, scale_a, True, tile_sink(qat_ref, h))
               for h in range(DA_HEADS)]
              + [(BLK_AK + h, (G_DAK, mean64), rope_a, 1.0, False, nat_sink(ka_ref, h * LANES))
                 for h in range(DA_HEADS)])
    run_group([(BLK_BQ + h, (G_NQ, mean128), rope_b, scale_b, True, tile_sink(qbt_ref, h))
               for h in range(NSA_HEADS)])
    run_group([(BLK_BKV + 0 + g, None, rope_b, 1.0, False, grp_sink(0, g)) for g in range(NSA_GROUPS)]
              + [(BLK_BKV + 4 + g, (G_NK1, mean128), rope_b, 1.0, False, nat_sink(ks_ref, g * LANES))
                 for g in range(NSA_GROUPS)]
              + [(BLK_BKV + 8 + g, (G_NK2, mean128), rope_b, 1.0, False, nat_sink(kw_ref, g * LANES))
                 for g in range(NSA_GROUPS)])
    run_group([(BLK_CQ + h, (G_FQ, mean128), None, scale_b, True, tile_sink(qct_ref, h, slice(0, LANES)))
               for h in range(FOX_HEADS)]
              + [(BLK_CK + h, (G_FK, mean128), None, 1.0, False, nat_sink(kf_ref, h * FOX_AUG))
                 for h in range(FOX_HEADS)])
    for g in range(NSA_GROUPS):
        kvc_ref[1, g] = raw(BLK_BKV + 2 + g)
        put_vt(vst_ref, g, BLK_BKV + 6 + g)
        put_vt(vwt_ref, g, BLK_BKV + 10 + g)
    for h in range(DA_HEADS):
        put_vt(vat_ref, h, BLK_AV + h)
    for h in range(FOX_HEADS):
        put_vt(vct_ref, h, BLK_CV + h)

    @pl.when(i % tiles_per_seq == 0)
    def _():
        carry_scr[...] = jnp.zeros_like(carry_scr)

    z = sm_ref[...] + gain(G_FBIAS)
    logf = jnp.minimum(z, 0.0) - jnp.log(1.0 + jnp.exp(-jnp.abs(z)))
    r_id = lax.broadcasted_iota(jnp.int32, (tm, tm), 0)
    c_id = lax.broadcasted_iota(jnp.int32, (tm, tm), 1)
    tri = jnp.where(r_id >= c_id, 1.0, 0.0).astype(BF16)
    hi, mid, lo = _split3(logf)
    cum = _dot(tri, hi) + (_dot(tri, mid) + _dot(tri, lo)) + carry_scr[0:1, :]
    carry_scr[0:1, :] = cum[tm - 1:tm, :]

    for h in range(FOX_HEADS):
        c = jnp.broadcast_to(cum[:, h:h + 1], (tm, LANES)) * LOG2E
        c_hi, c_mid, c_lo = (v.astype(F32) for v in _split3(c))
        ones = jnp.where(lane < 6, 1.0, 0.0)
        q_aug = jnp.where(lane == 0, c_hi, jnp.where(lane == 1, c_mid, jnp.where(lane == 2, c_lo, ones)))
        k_aug = jnp.where(lane == 3, -c_hi, jnp.where(lane == 4, -c_mid, jnp.where(lane == 5, -c_lo, ones)))
        qct_ref[h, 0, LANES:FOX_AUG, :] = tr(q_aug)
        kf_ref[:, h * FOX_AUG + LANES:(h + 1) * FOX_AUG] = k_aug.astype(BF16)


def _prep(p_qkv, small, rope_tab, gains, *, seq):
    tm = ATT_TILE
    T = p_qkv.shape[0]
    nt = T // tm
    G = NSA_GROUPS

    def rows(w):
        return pl.BlockSpec((tm, w), lambda i: (i, 0))

    def tiles_t(n, d=LANES):
        return (jax.ShapeDtypeStruct((n, nt, d, tm), BF16), pl.BlockSpec((n, 1, d, tm), lambda i: (0, i, 0, 0)))

    def nat(w):
        return (jax.ShapeDtypeStruct((T, w), BF16), rows(w))

    def grp():
        return (jax.ShapeDtypeStruct((2, G, T, LANES), BF16),
                pl.BlockSpec((2, G, tm, LANES), lambda i: (0, 0, i, 0)))

    outs = [
        tiles_t(DA_HEADS), nat(DA_HEADS * LANES), tiles_t(DA_HEADS, VT_ROWS),
        tiles_t(NSA_HEADS), nat(G * LANES), nat(G * LANES),
        tiles_t(G, VT_ROWS), tiles_t(G, VT_ROWS),
        tiles_t(FOX_HEADS, FOX_AUG), nat(FOX_HEADS * FOX_AUG), tiles_t(FOX_HEADS, VT_ROWS),
        grp(),
    ]
    return pl.pallas_call(
        functools.partial(_prep_kernel, tiles_per_seq=seq // tm),
        grid=(nt,),
        in_specs=[
            rows(QKV_COLS),
            rows(LANES),
            pl.BlockSpec((4, tm, LANES), lambda i: (0, i, 0)),
            pl.BlockSpec(gains.shape, lambda i: (0, 0)),
        ],
        out_specs=tuple(o[1] for o in outs),
        out_shape=tuple(o[0] for o in outs),
        scratch_shapes=[pltpu.VMEM((8, LANES), F32)],
        compiler_params=_cparams(("arbitrary",)),
        name="prep",
    )(p_qkv, small, rope_tab, gains)


def _compress_kernel(x_ref, pos_ref, w1_ref, w2_ref, gain_ref, kc_ref, vct_ref, *, batch):
    def mlp(kv):
        blk = (x_ref[kv].astype(F32) + pos_ref[kv]).astype(BF16)
        hid = jax.nn.gelu(_dot(blk, w1_ref[kv].astype(BF16)))
        return _dot(hid.astype(BF16), w2_ref[kv].astype(BF16))

    kc = _rms_rows(mlp(0), gain_ref[...])
    vc = mlp(1)
    nb = kc.shape[0] // batch
    pad = jnp.zeros((LANES - nb, LANES), F32)
    for b in range(batch):
        kc_ref[b] = jnp.concatenate([kc[b * nb:(b + 1) * nb], pad], axis=0).astype(BF16)
        vct_ref[b] = jnp.concatenate([vc[b * nb:(b + 1) * nb], pad], axis=0).T.astype(BF16)


def _compress(xkv, pos, w1, w2, gain, layer, *, batch):
    _, G, R, K = xkv.shape
    d = w1.shape[-1]
    out = jax.ShapeDtypeStruct((G, batch, LANES, LANES), BF16)
    ospec = pl.BlockSpec((None, batch, LANES, LANES), lambda g: (g, 0, 0, 0))
    return pl.pallas_call(
        functools.partial(_compress_kernel, batch=batch),
        grid=(G,),
        in_specs=[
            pl.BlockSpec((2, None, R, K), lambda g: (0, g, 0, 0)),
            pl.BlockSpec((None, 2, 1, K), lambda g: (layer, 0, 0, 0)),
            pl.BlockSpec((None, 2, K, d), lambda g: (layer, 0, 0, 0)),
            pl.BlockSpec((None, 2, d, d), lambda g: (layer, 0, 0, 0)),
            pl.BlockSpec((1, d), lambda g: (0, 0)),
        ],
        out_specs=(ospec, ospec),
        out_shape=(out, out),
        compiler_params=_cparams(("arbitrary",)),
        name="compress",
    )(xkv, pos, w1, w2, gain)


def _online_t(s_ts, mask, carry, v_ts):
    masks = mask if isinstance(mask, (list, tuple)) else [mask] * len(s_ts)
    stats = []
    for s_t, (m, _), mk in zip(s_ts, carry, masks):
        if mk is not None:
            s_t = jnp.where(mk, s_t, NEG_INF)
        m_new = jnp.maximum(m, jnp.max(s_t, axis=0, keepdims=True))
        stats.append((m_new, jnp.exp2(m - m_new), jnp.exp2(s_t - m_new).astype(BF16)))
    return tuple((m_new, alpha * acc_t + _dot(v_t, p))
                 for (m_new, alpha, p), (_, acc_t), v_t in zip(stats, carry, v_ts))


def _attend(scores, values, loop_mask, last_mask, kt0, kt_last, n):
    t = ATT_TILE
    init = tuple((jnp.full((1, t), NEG_INF, F32), jnp.zeros((VT_ROWS, t), F32)) for _ in range(n))

    def body(kt, carry):
        mask = None if loop_mask is None else loop_mask(kt)
        return _online_t(scores(kt), mask, carry, values(kt))

    carry = lax.fori_loop(kt0, kt_last, body, init)
    return _normalise(_online_t(scores(kt_last), last_mask, carry, values(kt_last)))


def _normalise(carry):
    return [acc[0:LANES] / acc[LANES:LANES + 1] for (_, acc) in carry]


def _causal_mask_t(t):
    return lax.broadcasted_iota(jnp.int32, (t, t), 0) <= lax.broadcasted_iota(jnp.int32, (t, t), 1)


def _diff_kernel(qt_ref, k_ref, vt_ref, lam_ref, gains_ref, o_ref, *, lam_init):
    i = pl.program_id(2)
    t = ATT_TILE
    nh = qt_ref.shape[0]
    drow = lax.broadcasted_iota(jnp.int32, (LANES, t), 0)
    qts = []
    for h in range(nh):
        q = qt_ref[h, 0]
        qts.append(jnp.where(drow < DA_QK_DIM, q, jnp.zeros_like(q)))
        qts.append(jnp.where(drow < DA_QK_DIM, jnp.zeros_like(q), q))

    def scores(kt):
        off = pl.multiple_of(kt * t, t)
        ks = [k_ref[pl.ds(off, t), h * LANES:(h + 1) * LANES] for h in range(nh)]
        return tuple(_dot(ks[c // 2], qts[c]) for c in range(2 * nh))

    def values(kt):
        return [vt_ref[c // 2, kt] for c in range(2 * nh)]

    outs = _attend(scores, values, None, _causal_mask_t(t), 0, i, 2 * nh)

    lp = lam_ref[...]
    lam = (jnp.exp(jnp.sum(lp[0:1] * lp[1:2], axis=-1, keepdims=True))
           - jnp.exp(jnp.sum(lp[2:3] * lp[3:4], axis=-1, keepdims=True)) + lam_init)
    for h in range(nh):
        o = (outs[2 * h] - lam * outs[2 * h + 1]).T
        o = _rms_rows(o, gains_ref[G_DAOUT:G_DAOUT + 1, :]) * (1.0 - lam_init)
        o_ref[:, h * LANES:(h + 1) * LANES] = o.astype(BF16)


def _diff_attention(qat, ka, vat, da_lambda, gains, *, batch, seq, lam_init, heads_per_step=4):
    t = ATT_TILE
    nq = seq // t
    T = batch * seq
    hp = heads_per_step
    return pl.pallas_call(
        functools.partial(_diff_kernel, lam_init=lam_init),
        grid=(batch, DA_HEADS // hp, nq),
        in_specs=[
            pl.BlockSpec((hp, 1, LANES, t), lambda b, p, i: (p, b * nq + i, 0, 0)),
            pl.BlockSpec((seq, hp * LANES), lambda b, p, i: (b, p)),
            pl.BlockSpec((hp, nq, VT_ROWS, t), lambda b, p, i: (p, b, 0, 0)),
            pl.BlockSpec(da_lambda.shape, lambda b, p, i: (0, 0)),
            pl.BlockSpec(gains.shape, lambda b, p, i: (0, 0)),
        ],
        out_specs=pl.BlockSpec((t, hp * LANES), lambda b, p, i: (b * nq + i, p)),
        out_shape=jax.ShapeDtypeStruct((T, DA_HEADS * LANES), BF16),
        compiler_params=_cparams(("parallel", "parallel", "arbitrary")),
        name="diff_attn",
    )(qat, ka, vat, da_lambda, gains)


def _fox_kernel(qt_ref, k_ref, vt_ref, o_ref):
    i = pl.program_id(1)
    t = ATT_TILE
    nh, bs = qt_ref.shape[0], qt_ref.shape[1]
    chains = [(b, h) for b in range(bs) for h in range(nh)]
    qts = [qt_ref[h, b, 0] for b, h in chains]

    def scores(kt):
        off = pl.multiple_of(kt * t, t)
        return tuple(_dot(k_ref[b, pl.ds(off, t), h * FOX_AUG:(h + 1) * FOX_AUG], q)
                     for (b, h), q in zip(chains, qts))

    def values(kt):
        return [vt_ref[h, b, kt] for b, h in chains]

    outs = _attend(scores, values, None, _causal_mask_t(t), 0, i, len(chains))
    for (b, h), o in zip(chains, outs):
        o_ref[b, :, h * LANES:(h + 1) * LANES] = o.T.astype(BF16)


def _fox_attention(qct, kf, vct, *, batch, seq):
    t = ATT_TILE
    nq = seq // t
    nh = FOX_HEADS
    bs = 2 if batch % 2 == 0 else 1
    nbp = batch // bs
    out = pl.pallas_call(
        _fox_kernel,
        grid=(nbp, nq),
        in_specs=[
            pl.BlockSpec((nh, None, bs, 1, FOX_AUG, t), lambda p, i: (0, p, 0, i, 0, 0)),
            pl.BlockSpec((None, bs, seq, nh * FOX_AUG), lambda p, i: (p, 0, 0, 0)),
            pl.BlockSpec((nh, None, bs, nq, VT_ROWS, t), lambda p, i: (0, p, 0, 0, 0, 0)),
        ],
        out_specs=pl.BlockSpec((None, bs, t, nh * LANES), lambda p, i: (p, 0, i, 0)),
        out_shape=jax.ShapeDtypeStruct((nbp, bs, seq, nh * LANES), BF16),
        compiler_params=_cparams(("parallel", "arbitrary")),
        name="fox_attn",
    )(qct.reshape(nh, nbp, bs, nq, FOX_AUG, t), kf.reshape(nbp, bs, seq, nh * FOX_AUG),
      vct.reshape(nh, nbp, bs, nq, VT_ROWS, t))
    return out.reshape(batch * seq, nh * LANES)


def _nsa_kernel(qt_ref, ks_ref, vst_ref, kw_ref, vwt_ref, kc_ref, vct_ref, gt_ref, o_ref, *, n_sel, nb):
    i = pl.program_id(1)
    t = ATT_TILE
    lo = i * t
    nh = qt_ref.shape[0]
    hpg = nh // NSA_GROUPS
    qts = [qt_ref[h, 0] for h in range(nh)]
    tq_row = lo + lax.broadcasted_iota(jnp.int32, (1, t), 1)
    nidx = lax.broadcasted_iota(jnp.int32, (nb, t), 0)
    nidx_f = nidx.astype(F32)
    cmask = nidx * NSA_BLOCK + (NSA_BLOCK - 1) <= tq_row
    cur = tq_row >> BLOCK_SHIFT
    forced = (nidx == 0) | (nidx == cur) | (nidx == cur - 1)
    zpad = jnp.zeros((LANES - nb, t), F32)

    def lane_tile(v):
        return jnp.concatenate([v, zpad], axis=0).astype(BF16)

    o_cmp, sel_b = [], []
    for g in range(NSA_GROUPS):
        kc, vct = kc_ref[g, 0:nb, :], vct_ref[g]
        importance = jnp.zeros((nb, t), F32)
        for h in range(g * hpg, (g + 1) * hpg):
            s = jnp.where(cmask, _dot(kc, qts[h]), NEG_INF)
            e = jnp.where(cmask, jnp.exp2(s - jnp.max(s, axis=0, keepdims=True)), 0.0)
            p = e / jnp.maximum(jnp.sum(e, axis=0, keepdims=True), 1e-30)
            importance = importance + p
            o_cmp.append(_dot(vct, lane_tile(p)))

        score = jnp.where(forced, NSA_FORCED_SCORE, jnp.where(nidx <= cur, importance, -1.0))
        sel = jnp.zeros((nb, t), F32)
        for _ in range(n_sel):
            best = jnp.max(score, axis=0, keepdims=True)
            first = jnp.min(jnp.where(score == best, nidx_f, float(nb)), axis=0, keepdims=True)
            hit = nidx_f == first
            sel = jnp.where(hit, 1.0, sel)
            score = jnp.where(hit, -3e38, score)
        sel_b.append(lane_tile(sel))

    e_key = lax.broadcasted_iota(jnp.int32, (t, LANES), 0) >> BLOCK_SHIFT
    e_blk = lax.broadcasted_iota(jnp.int32, (t, LANES), 1)
    krow = lax.broadcasted_iota(jnp.int32, (t, t), 0)
    qcol = lax.broadcasted_iota(jnp.int32, (t, t), 1)
    causal = krow <= qcol

    def scores_of(k_ref):
        def scores(kt):
            off = pl.multiple_of(kt * t, t)
            ks = [k_ref[pl.ds(off, t), g * LANES:(g + 1) * LANES] for g in range(NSA_GROUPS)]
            return tuple(_dot(ks[h // hpg], qts[h]) for h in range(nh))
        return scores

    def values_of(vt_ref):
        return lambda kt: [vt_ref[h // hpg, kt] for h in range(nh)]

    def selected(kt, extra=None):
        expand = jnp.where(e_blk == e_key + kt * (t // NSA_BLOCK), 1.0, 0.0).astype(BF16)
        per_group = [_dot(expand, sb) > 0.5 for sb in sel_b]
        if extra is not None:
            per_group = [m & extra for m in per_group]
        return [per_group[h // hpg] for h in range(nh)]

    o_slc = _attend(scores_of(ks_ref), values_of(vst_ref), selected, selected(i, causal), 0, i, nh)

    wt = NSA_WINDOW // t
    win_scores, win_values = scores_of(kw_ref), values_of(vwt_ref)
    upper = krow > qcol
    far = jnp.maximum(i - wt, 0)
    penalty = jnp.where(i >= wt, 0.0, NEG_INF)
    s_far, s_diag = win_scores(far), win_scores(i)
    stats = []
    for c in range(nh):
        s = jnp.where(upper, s_far[c] + penalty, s_diag[c])
        m = jnp.max(s, axis=0, keepdims=True)
        stats.append((m, jnp.exp2(s - m).astype(BF16)))
    v_far, v_diag = win_values(far), win_values(i)
    carry = tuple((m, _dot(v_far[c], jnp.where(upper, p, jnp.zeros_like(p)))
                   + _dot(v_diag[c], jnp.where(upper, jnp.zeros_like(p), p)))
                  for c, (m, p) in enumerate(stats))
    carry = lax.fori_loop(jnp.maximum(i - wt + 1, 0), i,
                          lambda kt, cr: _online_t(win_scores(kt), None, cr, win_values(kt)), carry)
    o_win = _normalise(carry)

    for h in range(nh):
        g, r = h // hpg, 3 * (h % hpg)
        gt = jax.nn.sigmoid(gt_ref[(1 + g) * LANES + r:(1 + g) * LANES + r + 3, :])
        o = gt[0:1] * o_cmp[h] + gt[1:2] * o_slc[h] + gt[2:3] * o_win[h]
        o_ref[:, h * LANES:(h + 1) * LANES] = o.T.astype(BF16)


def _nsa_attention(qbt, ks, kw, vst, vwt, kc, vct, small_t, *, batch, seq):
    t = ATT_TILE
    nq = seq // t
    T = batch * seq
    n_sel = min(NSA_TOP_N, seq // NSA_BLOCK)
    G = NSA_GROUPS
    k_spec = pl.BlockSpec((seq, G * LANES), lambda b, i: (b, 0))
    vt_spec = pl.BlockSpec((G, nq, VT_ROWS, t), lambda b, i: (0, b, 0, 0))
    c_spec = pl.BlockSpec((G, None, LANES, LANES), lambda b, i: (0, b, 0, 0))
    return pl.pallas_call(
        functools.partial(_nsa_kernel, n_sel=n_sel, nb=seq // NSA_BLOCK),
        grid=(batch, nq),
        in_specs=[
            pl.BlockSpec((NSA_HEADS, 1, LANES, t), lambda b, i: (0, b * nq + i, 0, 0)),
            k_spec, vt_spec, k_spec, vt_spec, c_spec, c_spec,
            pl.BlockSpec((SMALL_COLS, t), lambda b, i: (0, b * nq + i)),
        ],
        out_specs=pl.BlockSpec((t, NSA_HEADS * LANES), lambda b, i: (b * nq + i, 0)),
        out_shape=jax.ShapeDtypeStruct((T, NSA_HEADS * LANES), BF16),
        compiler_params=_cparams(("parallel", "arbitrary")),
        name="nsa_attn",
    )(qbt, ks, vst, kw, vwt, kc, vct, small_t)


def _merge_kernel(x_ref, oa_ref, ob_ref, oc_ref, gm_ref, wa_ref, wb_ref, wc_ref, wo_ref, o_ref, *, gm_off):
    D = x_ref.shape[1]
    g0, g1, g2 = (gm_ref[:, gm_off + n * D:gm_off + (n + 1) * D].astype(F32) for n in range(3))
    y = (g0 * _dot(oa_ref[...], wa_ref[...]) + g1 * _dot(ob_ref[...], wb_ref[...])
         + g2 * _dot(oc_ref[...], wc_ref[...]))
    o_ref[...] = x_ref[...] + _dot(y.astype(BF16), wo_ref[...])


def _merge(x2d, oa, ob, oc, p_gm, gm_off, wa, wb, wc, wo, layer, *, tm=512):
    T, D = x2d.shape

    def rows(w, c=0):
        return pl.BlockSpec((tm, w), lambda i: (i, c))

    def full(w):
        return pl.BlockSpec((None,) + w.shape[1:], lambda i: (layer, 0, 0), pipeline_mode=pl.Buffered(1))

    return pl.pallas_call(
        functools.partial(_merge_kernel, gm_off=gm_off),
        grid=(T // tm,),
        in_specs=[rows(D), rows(oa.shape[1]), rows(ob.shape[1]), rows(oc.shape[1]),
                  rows(p_gm.shape[1]), full(wa), full(wb), full(wc), full(wo)],
        out_specs=rows(D),
        out_shape=jax.ShapeDtypeStruct((T, D), F32),
        compiler_params=_cparams(("parallel",), VMEM_LIMIT_FFN),
        name="merge",
    )(x2d, oa, ob, oc, p_gm, wa, wb, wc, wo)


def _rope_tables(positions, width, rot_dim):
    inv_freq = ROPE_THETA ** (-jnp.arange(0, rot_dim, 2, dtype=F32) / rot_dim)
    ang = positions.astype(F32).reshape(-1, 1) * inv_freq
    cos, sin = jnp.cos(ang), jnp.sin(ang)
    T = ang.shape[0]
    rest = width - rot_dim
    c = jnp.concatenate([cos, cos, jnp.ones((T, rest), F32)], axis=1)
    s = jnp.concatenate([-sin, sin, jnp.zeros((T, rest), F32)], axis=1)
    rep = LANES // width
    return [jnp.tile(t, (1, rep)) for t in (c, s)]


def _pad_lanes(v):
    return jnp.pad(v.astype(F32), (0, LANES - v.shape[0]))


def kernel(x, positions, ffn1_norm, ffn1_w_gate, ffn1_w_up, ffn1_w_down, mix_norm, w_in, da_q_norm, da_k_norm, da_lambda, da_out_norm, nsa_q_norm, nsa_k_norm, nsa_cmp_pos, nsa_cmp_w1, nsa_cmp_w2, fox_q_norm, fox_k_norm, fox_f_bias, w_branch_a, w_branch_b, w_branch_c, w_out, ffn2_norm, ffn2_w_gate, ffn2_w_up, ffn2_w_down):
    B, S, D = x.shape
    depth = w_in.shape[0]
    T = B * S
    G, L, d = NSA_GROUPS, NSA_BLOCK, HEAD_DIM
    nb = S // L
    assert S % ATT_TILE == 0 and nb <= LANES and nb % 8 == 0 and NSA_WINDOW % ATT_TILE == 0

    rope_tab = jnp.stack(_rope_tables(positions, DA_QK_DIM, DA_QK_DIM // ROPE_FRACTION)
                         + _rope_tables(positions, HEAD_DIM, HEAD_DIM // ROPE_FRACTION))

    widths = [512, 512, 512, 1024, 1536, 24, 512, 512, 512, 4, 3 * D]
    offs = [0]
    for w in widths:
        offs.append(offs[-1] + w)
    (o_aq, o_ak, o_av, o_bq, o_bkv, o_bg, o_cq, o_ck, o_cv, o_cf, o_gm, o_end) = offs

    w_qkv = jnp.concatenate([w_in[:, :, o_aq:o_bg], w_in[:, :, o_cq:o_cf]], axis=2).astype(BF16)
    gm_lo = o_gm // LANES * LANES
    gm_off = o_gm - gm_lo
    gm_tile = 5 * MXU_COLS_V7X
    gm_cols = -(-(o_end - gm_lo) // LANES) * LANES
    w_gm = jnp.pad(w_in[:, :, gm_lo:o_end], ((0, 0), (0, 0), (0, gm_cols - (o_end - gm_lo)))).astype(BF16)
    zpad = lambda n: jnp.zeros((depth, D, n), F32)
    w_small = jnp.concatenate([
        w_in[:, :, o_cf:o_gm], zpad(LANES - FOX_HEADS),
        w_in[:, :, o_bg:o_bg + 12], zpad(LANES - 12),
        w_in[:, :, o_bg + 12:o_bg + 24], zpad(LANES - 12)], axis=2)
    w_small_hi = w_small.astype(BF16)
    w_small_lo = (w_small - w_small_hi.astype(F32)).astype(BF16)
    ffn_f32 = [(ffn1_w_gate, ffn1_w_up, ffn1_w_down), (ffn2_w_gate, ffn2_w_up, ffn2_w_down)]
    ffn_g = [ffn1_norm.reshape(depth, 1, D), ffn2_norm.reshape(depth, 1, D)]
    ffn_next = [(k // 2, *ffn_f32[k % 2]) for k in range(1, 2 * depth)] + [None]
    ffn_w = tuple(_to_bf16(w, rows=256, layer=0) for w in ffn_f32[0])
    g_mix = mix_norm.reshape(depth, 1, D)
    w_a, w_b, w_c, w_o = (_to_bf16(w, rows=512) for w in (w_branch_a, w_branch_b, w_branch_c, w_out))
    cmp_pos = nsa_cmp_pos.reshape(depth, 2, 1, L * d)

    xc = x.reshape(T, D)
    for l in range(depth):
        lam_init = 0.8 - 0.6 * math.exp(-0.3 * l)
        gains = jnp.stack([
            jnp.tile(da_q_norm[l], 2), jnp.tile(da_k_norm[l], 2), nsa_q_norm[l],
            nsa_k_norm[l, 0], nsa_k_norm[l, 1], nsa_k_norm[l, 2], fox_q_norm[l], fox_k_norm[l],
            _pad_lanes(fox_f_bias[l]), da_out_norm[l]] + [jnp.zeros((LANES,), F32)] * 6).astype(F32)

        xc, ffn_w = _ffn(xc, ffn_g[0], *ffn_w, l, ffn_next[2 * l])

        p_qkv = _proj(xc, g_mix, w_qkv, l, gate=False, tn=6 * MXU_COLS_V7X)
        p_gm = _proj(xc, g_mix, w_gm, l, gate=True, tn=gm_tile)
        small, small_t = _proj_small(xc, g_mix, w_small_hi, w_small_lo, l)

        (qat, ka, vat, qbt, ks, kw, vst, vwt, qct, kf, vct, kv_cmp) = _prep(
            p_qkv, small, rope_tab, gains, seq=S)

        kc, vcmp_t = _compress(kv_cmp.reshape(2, G, B * nb, L * d), cmp_pos, nsa_cmp_w1, nsa_cmp_w2,
                               nsa_k_norm[l, 0].reshape(1, d), l, batch=B)

        o_a = _diff_attention(qat, ka, vat, da_lambda[l], gains, batch=B, seq=S, lam_init=lam_init)
        o_b = _nsa_attention(qbt, ks, kw, vst, vwt, kc, vcmp_t, small_t, batch=B, seq=S)
        o_c = _fox_attention(qct, kf, vct, batch=B, seq=S)

        xc = _merge(xc, o_a, o_b, o_c, p_gm, gm_off, w_a, w_b, w_c, w_o, l)

        xc, ffn_w = _ffn(xc, ffn_g[1], *ffn_w, l, ffn_next[2 * l + 1])
    return xc.reshape(B, S, D)
```

```python
import functools
import math

import jax
import jax.numpy as jnp
from jax import lax
from jax.experimental import pallas as pl
from jax.experimental.pallas import tpu as pltpu

F32 = jnp.float32
BF16 = jnp.bfloat16

HEAD_DIM = 128
LANES = 128
ROPE_THETA = 500000.0
ROPE_FRACTION = 4
EPS = 1e-6
NEG_INF = -1e30
LOG2E = math.log2(math.e)

DA_HEADS = 4
DA_QK_DIM = 64
NSA_HEADS = 8
NSA_GROUPS = 2
NSA_HPG = NSA_HEADS // NSA_GROUPS
NSA_BLOCK = 64
BLOCK_SHIFT = 6
NSA_TOP_N = 8
NSA_WINDOW = 512
NSA_FORCED_SCORE = 1e4
FOX_HEADS = 4

ATT_TILE = 256
FOX_AUG = 2 * LANES
VT_ROWS = LANES + 16

BLK_AQ, BLK_AK, BLK_AV, BLK_BQ, BLK_BKV, BLK_CQ, BLK_CK, BLK_CV = 0, 4, 8, 12, 20, 32, 36, 40
QKV_COLS = 44 * LANES
SMALL_COLS = 3 * LANES

G_DAQ, G_DAK, G_NQ, G_NK0, G_NK1, G_NK2, G_FQ, G_FK, G_FBIAS, G_DAOUT = range(10)

VMEM_BYTES_V7X = 64 * 1024 * 1024
VMEM_LIMIT = VMEM_BYTES_V7X - 8 * 1024 * 1024
VMEM_LIMIT_FFN = VMEM_BYTES_V7X - 3 * 1024 * 1024


def _cparams(sem, limit=VMEM_LIMIT):
    return pltpu.CompilerParams(dimension_semantics=sem, vmem_limit_bytes=limit)


def _dot(a, b):
    return jnp.dot(a, b, preferred_element_type=F32)


def _dot_t(a, b):
    return lax.dot_general(a, b, (((1,), (1,)), ((), ())), preferred_element_type=F32)


def _rms_rows(xf, gain):
    return xf * lax.rsqrt(jnp.mean(xf * xf, axis=-1, keepdims=True) + EPS) * gain


def _ffn_kernel(x_ref, g_ref, wg_ref, wu_ref, wd_ref, *rest):
    n_cast = (len(rest) - 2) // 2
    o_ref, h_scr = rest[n_cast], rest[-1]

    @pl.when(pl.program_id(1) == 0)
    def _():
        xf = x_ref[...]
        h_scr[...] = _rms_rows(xf, g_ref[...]).astype(BF16)
        o_ref[...] = xf

    h = h_scr[...]
    g = _dot(h, wg_ref[...])
    u = _dot(h, wu_ref[...])
    a = (g * jax.nn.sigmoid(g)) * (0.5 * u)
    o_ref[...] += _dot(a.astype(BF16), wd_ref[...])

    for src, dst in zip(rest[:n_cast], rest[n_cast + 1:-1]):
        dst[...] = src[...].astype(BF16)


def _ffn(x2d, gain, wg, wu, wd, layer, cast_next=None, *, tm=1024, tf=512):
    T, D = x2d.shape
    F = wg.shape[2]
    tm = min(tm, T)
    ni, nj = T // tm, F // tf
    in_specs = [
        pl.BlockSpec((tm, D), lambda i, j: (i, 0)),
        pl.BlockSpec((None, 1, D), lambda i, j: (layer, 0, 0)),
        pl.BlockSpec((None, D, tf), lambda i, j: (0, 0, j)),
        pl.BlockSpec((None, D, tf), lambda i, j: (0, 0, j)),
        pl.BlockSpec((None, tf, D), lambda i, j: (0, j, 0)),
    ]
    out_specs = [pl.BlockSpec((tm, D), lambda i, j: (i, 0))]
    out_shape = [jax.ShapeDtypeStruct((T, D), F32)]
    args = [x2d, gain, wg, wu, wd]
    if cast_next is not None:
        nl, *nws = cast_next
        for w in nws:
            _, R, C = w.shape
            blk, imap = (((R // ni, C // nj), lambda i, j: (nl, i, j)) if C == F
                         else ((R // nj, C // ni), lambda i, j: (nl, j, i)))
            in_specs.append(pl.BlockSpec((None,) + blk, imap))
            omap = (lambda i, j: (0, i, j)) if C == F else (lambda i, j: (0, j, i))
            out_specs.append(pl.BlockSpec((None,) + blk, omap))
            out_shape.append(jax.ShapeDtypeStruct((1, R, C), BF16))
            args.append(w)
    outs = pl.pallas_call(
        _ffn_kernel,
        grid=(ni, nj),
        in_specs=in_specs,
        out_specs=tuple(out_specs),
        out_shape=tuple(out_shape),
        scratch_shapes=[pltpu.VMEM((tm, D), BF16)],
        compiler_params=_cparams(("parallel", "arbitrary"), VMEM_LIMIT_FFN),
        name="ffn",
    )(*args)
    return outs[0], tuple(outs[1:])


def _cast_kernel(x_ref, o_ref):
    o_ref[...] = x_ref[...].astype(BF16)


def _to_bf16(w, *, rows, layer=None):
    depth, R, C = w.shape
    first, count = (0, depth) if layer is None else (layer, 1)
    return pl.pallas_call(
        _cast_kernel,
        grid=(count, R // rows),
        in_specs=[pl.BlockSpec((None, rows, C), lambda l, i: (first + l, i, 0))],
        out_specs=pl.BlockSpec((None, rows, C), lambda l, i: (l, i, 0)),
        out_shape=jax.ShapeDtypeStruct((count, R, C), BF16),
        compiler_params=_cparams(("parallel", "parallel")),
        name="cast_bf16",
    )(w)


def _proj_kernel(x_ref, g_ref, w_ref, o_ref, h_scr, *, gate):
    @pl.when(pl.program_id(1) == 0)
    def _():
        h_scr[...] = _rms_rows(x_ref[...], g_ref[...]).astype(BF16)

    r = _dot(h_scr[...], w_ref[...])
    o_ref[...] = (jax.nn.sigmoid(r) if gate else r).astype(o_ref.dtype)


def _proj(x2d, gain, w, layer, *, gate, tm=1024, tn):
    T, D = x2d.shape
    N = w.shape[2]
    tm = min(tm, T)
    return pl.pallas_call(
        functools.partial(_proj_kernel, gate=gate),
        grid=(T // tm, pl.cdiv(N, tn)),
        in_specs=[
            pl.BlockSpec((tm, D), lambda i, j: (i, 0)),
            pl.BlockSpec((None, 1, D), lambda i, j: (layer, 0, 0)),
            pl.BlockSpec((None, D, tn), lambda i, j: (layer, 0, j)),
        ],
        out_specs=pl.BlockSpec((tm, tn), lambda i, j: (i, j)),
        out_shape=jax.ShapeDtypeStruct((T, N), BF16),
        scratch_shapes=[pltpu.VMEM((tm, D), BF16)],
        compiler_params=_cparams(("parallel", "arbitrary")),
        name="proj_gate" if gate else "proj_qkv",
    )(x2d, gain, w)


def _split3(v):
    hi = v.astype(BF16)
    r = v - hi.astype(F32)
    mid = r.astype(BF16)
    lo = (r - mid.astype(F32)).astype(BF16)
    return hi, mid, lo


def _proj_small_kernel(x_ref, g_ref, whi_ref, wlo_ref, o_ref, ot_ref):
    h = _rms_rows(x_ref[...], g_ref[...])
    hi, mid, _ = _split3(h)
    whi = whi_ref[...]
    r = _dot(hi, whi) + (_dot(mid, whi) + _dot(hi, wlo_ref[...]))
    o_ref[...] = r
    for c in range(r.shape[1] // LANES):
        ot_ref[c * LANES:(c + 1) * LANES, :] = r[:, c * LANES:(c + 1) * LANES].T


def _proj_small(x2d, gain, whi, wlo, layer, *, tm=512):
    T, D = x2d.shape
    N = whi.shape[2]
    return pl.pallas_call(
        _proj_small_kernel,
        grid=(T // tm,),
        in_specs=[
            pl.BlockSpec((tm, D), lambda i: (i, 0)),
            pl.BlockSpec((None, 1, D), lambda i: (layer, 0, 0)),
            pl.BlockSpec((None, D, N), lambda i: (layer, 0, 0)),
            pl.BlockSpec((None, D, N), lambda i: (layer, 0, 0)),
        ],
        out_specs=(pl.BlockSpec((tm, N), lambda i: (i, 0)), pl.BlockSpec((N, tm), lambda i: (0, i))),
        out_shape=(jax.ShapeDtypeStruct((T, N), F32), jax.ShapeDtypeStruct((N, T), F32)),
        compiler_params=_cparams(("parallel",)),
        name="proj_small",
    )(x2d, gain, whi, wlo)


def _prep_kernel(p_ref, sm_ref, rope_ref, gains_ref,
                 qat_ref, ka_ref, vat_ref, qbt_ref, ks_ref, kw_ref, vst_ref, vwt_ref,
                 qct_ref, kf_ref, vct_ref, kvc_ref,
                 carry_scr, *, tiles_per_seq):
    i = pl.program_id(0)
    tm = p_ref.shape[0]
    lane = lax.broadcasted_iota(jnp.int32, (tm, LANES), 1)
    src = lax.broadcasted_iota(jnp.int32, (LANES, LANES), 0)
    dst = lax.broadcasted_iota(jnp.int32, (LANES, LANES), 1)

    def const(cond, val=1.0):
        return jnp.where(cond, val, 0.0).astype(BF16)

    def group_mean(width):
        shift = width.bit_length() - 1
        return const((src >> shift) == (dst >> shift), 1.0 / width)

    def swap_halves(width, half):
        d = dst & (width - 1)
        return const(((d < half) & (src == dst + half)) | ((d >= half) & (d < 2 * half) & (src == dst - half)))

    mean128, mean64 = group_mean(HEAD_DIM), group_mean(DA_QK_DIM)
    eye = const(src == dst)

    def dot1(v, mat):
        return _dot(v.astype(BF16), mat)

    def raw(c):
        return p_ref[:, c * LANES:(c + 1) * LANES]

    def chunk(c):
        return raw(c).astype(F32)

    def gain(r):
        return gains_ref[r:r + 1, :]

    def rope_tab(r):
        return rope_ref[r]


    def tr(y):
        return _dot_t(eye, y.astype(BF16)).astype(BF16)

    def put_vt(ref, n, c):
        ref[n, 0, 0:LANES, :] = tr(raw(c))
        ref[n, 0, LANES:VT_ROWS, :] = jnp.ones((VT_ROWS - LANES, tm), BF16)

    half_a = swap_halves(DA_QK_DIM, DA_QK_DIM // ROPE_FRACTION // 2)
    half_b = swap_halves(HEAD_DIM, HEAD_DIM // ROPE_FRACTION // 2)
    scale_a = DA_QK_DIM ** -0.5 * LOG2E
    scale_b = HEAD_DIM ** -0.5 * LOG2E

    def run_group(specs):
        xs = [chunk(c) for c, *_ in specs]
        ms = [None if nm is None else dot1(x * x, nm[1]) for x, (_, nm, *_) in zip(xs, specs)]
        ys = [x if nm is None else x * lax.rsqrt(m + EPS) * gain(nm[0])
              for x, m, (_, nm, *_) in zip(xs, ms, specs)]
        ps = [None if rp is None else dot1(y, rp[1]) for y, (_, _, rp, *_) in zip(ys, specs)]
        rs = [y if rp is None else y * rope_tab(rp[0]) + p * rope_tab(rp[0] + 1)
              for y, p, (_, _, rp, *_) in zip(ys, ps, specs)]
        rs = [r if sc == 1.0 else r * sc for r, (_, _, _, sc, _, _) in zip(rs, specs)]
        outs = [tr(r) if tp else r.astype(BF16) for r, (_, _, _, _, tp, _) in zip(rs, specs)]
        for o, spec in zip(outs, specs):
            spec[-1](o)

    def nat_sink(ref, lo):
        def sink(o):
            ref[:, lo:lo + LANES] = o
        return sink

    def tile_sink(ref, n, rows=slice(None)):
        def sink(o):
            ref[n, 0, rows, :] = o
        return sink

    def grp_sink(n, g):
        def sink(o):
            kvc_ref[n, g] = o
        return sink

    rope_a, rope_b = (0, half_a), (2, half_b)
    run_group([(BLK_AQ + h, (G_DAQ, mean64), rope_a, scale_a, True, tile_sink(qat_ref, h))
               for h in range(DA_HEADS)]
              + [(BLK_AK + h, (G_DAK, mean64), rope_a, 1.0, False, nat_sink(ka_ref, h * LANES))
                 for h in range(DA_HEADS)])
    run_group([(BLK_BQ + h, (G_NQ, mean128), rope_b, scale_b, True, tile_sink(qbt_ref, h))
               for h in range(NSA_HEADS)])
    run_group([(BLK_BKV + 0 + g, None, rope_b, 1.0, False, grp_sink(0, g)) for g in range(NSA_GROUPS)]
              + [(BLK_BKV + 4 + g, (G_NK1, mean128), rope_b, 1.0, False, nat_sink(ks_ref, g * LANES))
                 for g in range(NSA_GROUPS)]
              + [(BLK_BKV + 8 + g, (G_NK2, mean128), rope_b, 1.0, False, nat_sink(kw_ref, g * LANES))
                 for g in range(NSA_GROUPS)])
    run_group([(BLK_CQ + h, (G_FQ, mean128), None, scale_b, True, tile_sink(qct_ref, h, slice(0, LANES)))
               for h in range(FOX_HEADS)]
              + [(BLK_CK + h, (G_FK, mean128), None, 1.0, False, nat_sink(kf_ref, h * FOX_AUG))
                 for h in range(FOX_HEADS)])
    for g in range(NSA_GROUPS):
        kvc_ref[1, g] = raw(BLK_BKV + 2 + g)
        put_vt(vst_ref, g, BLK_BKV + 6 + g)
        put_vt(vwt_ref, g, BLK_BKV + 10 + g)
    for h in range(DA_HEADS):
        put_vt(vat_ref, h, BLK_AV + h)
    for h in range(FOX_HEADS):
        put_vt(vct_ref, h, BLK_CV + h)

    @pl.when(i % tiles_per_seq == 0)
    def _():
        carry_scr[...] = jnp.zeros_like(carry_scr)

    z = sm_ref[...] + gain(G_FBIAS)
    logf = jnp.minimum(z, 0.0) - jnp.log(1.0 + jnp.exp(-jnp.abs(z)))
    r_id = lax.broadcasted_iota(jnp.int32, (tm, tm), 0)
    c_id = lax.broadcasted_iota(jnp.int32, (tm, tm), 1)
    tri = jnp.where(r_id >= c_id, 1.0, 0.0).astype(BF16)
    hi, mid, lo = _split3(logf)
    cum = _dot(tri, hi) + (_dot(tri, mid) + _dot(tri, lo)) + carry_scr[0:1, :]
    carry_scr[0:1, :] = cum[tm - 1:tm, :]

    for h in range(FOX_HEADS):
        c = jnp.broadcast_to(cum[:, h:h + 1], (tm, LANES)) * LOG2E
        c_hi, c_mid, c_lo = (v.astype(F32) for v in _split3(c))
        ones = jnp.where(lane < 6, 1.0, 0.0)
        q_aug = jnp.where(lane == 0, c_hi, jnp.where(lane == 1, c_mid, jnp.where(lane == 2, c_lo, ones)))
        k_aug = jnp.where(lane == 3, -c_hi, jnp.where(lane == 4, -c_mid, jnp.where(lane == 5, -c_lo, ones)))
        qct_ref[h, 0, LANES:FOX_AUG, :] = tr(q_aug)
        kf_ref[:, h * FOX_AUG + LANES:(h + 1) * FOX_AUG] = k_aug.astype(BF16)


def _prep(p_qkv, small, rope_tab, gains, *, seq):
    tm = ATT_TILE
    T = p_qkv.shape[0]
    nt = T // tm
    G = NSA_GROUPS

    def rows(w):
        return pl.BlockSpec((tm, w), lambda i: (i, 0))

    def tiles_t(n, d=LANES):
        return (jax.ShapeDtypeStruct((n, nt, d, tm), BF16), pl.BlockSpec((n, 1, d, tm), lambda i: (0, i, 0, 0)))

    def nat(w):
        return (jax.ShapeDtypeStruct((T, w), BF16), rows(w))

    def grp():
        return (jax.ShapeDtypeStruct((2, G, T, LANES), BF16),
                pl.BlockSpec((2, G, tm, LANES), lambda i: (0, 0, i, 0)))

    outs = [
        tiles_t(DA_HEADS), nat(DA_HEADS * LANES), tiles_t(DA_HEADS, VT_ROWS),
        tiles_t(NSA_HEADS), nat(G * LANES), nat(G * LANES),
        tiles_t(G, VT_ROWS), tiles_t(G, VT_ROWS),
        tiles_t(FOX_HEADS, FOX_AUG), nat(FOX_HEADS * FOX_AUG), tiles_t(FOX_HEADS, VT_ROWS),
        grp(),
    ]
    return pl.pallas_call(
        functools.partial(_prep_kernel, tiles_per_seq=seq // tm),
        grid=(nt,),
        in_specs=[
            rows(QKV_COLS),
            rows(LANES),
            pl.BlockSpec((4, tm, LANES), lambda i: (0, i, 0)),
            pl.BlockSpec(gains.shape, lambda i: (0, 0)),
        ],
        out_specs=tuple(o[1] for o in outs),
        out_shape=tuple(o[0] for o in outs),
        scratch_shapes=[pltpu.VMEM((8, LANES), F32)],
        compiler_params=_cparams(("arbitrary",)),
        name="prep",
    )(p_qkv, small, rope_tab, gains)


def _compress_kernel(x_ref, pos_ref, w1_ref, w2_ref, gain_ref, kc_ref, vct_ref, *, batch):
    def mlp(kv):
        blk = (x_ref[kv].astype(F32) + pos_ref[kv]).astype(BF16)
        hid = jax.nn.gelu(_dot(blk, w1_ref[kv].astype(BF16)))
        return _dot(hid.astype(BF16), w2_ref[kv].astype(BF16))

    kc = _rms_rows(mlp(0), gain_ref[...])
    vc = mlp(1)
    nb = kc.shape[0] // batch
    pad = jnp.zeros((LANES - nb, LANES), F32)
    for b in range(batch):
        kc_ref[b] = jnp.concatenate([kc[b * nb:(b + 1) * nb], pad], axis=0).astype(BF16)
        vct_ref[b] = jnp.concatenate([vc[b * nb:(b + 1) * nb], pad], axis=0).T.astype(BF16)


def _compress(xkv, pos, w1, w2, gain, layer, *, batch):
    _, G, R, K = xkv.shape
    d = w1.shape[-1]
    out = jax.ShapeDtypeStruct((G, batch, LANES, LANES), BF16)
    ospec = pl.BlockSpec((None, batch, LANES, LANES), lambda g: (g, 0, 0, 0))
    return pl.pallas_call(
        functools.partial(_compress_kernel, batch=batch),
        grid=(G,),
        in_specs=[
            pl.BlockSpec((2, None, R, K), lambda g: (0, g, 0, 0)),
            pl.BlockSpec((None, 2, 1, K), lambda g: (layer, 0, 0, 0)),
            pl.BlockSpec((None, 2, K, d), lambda g: (layer, 0, 0, 0)),
            pl.BlockSpec((None, 2, d, d), lambda g: (layer, 0, 0, 0)),
            pl.BlockSpec((1, d), lambda g: (0, 0)),
        ],
        out_specs=(ospec, ospec),
        out_shape=(out, out),
        compiler_params=_cparams(("arbitrary",)),
        name="compress",
    )(xkv, pos, w1, w2, gain)


def _online_t(s_ts, mask, carry, v_ts):
    masks = mask if isinstance(mask, (list, tuple)) else [mask] * len(s_ts)
    stats = []
    for s_t, (m, _), mk in zip(s_ts, carry, masks):
        if mk is not None:
            s_t = jnp.where(mk, s_t, NEG_INF)
        m_new = jnp.maximum(m, jnp.max(s_t, axis=0, keepdims=True))
        stats.append((m_new, jnp.exp2(m - m_new), jnp.exp2(s_t - m_new).astype(BF16)))
    return tuple((m_new, alpha * acc_t + _dot(v_t, p))
                 for (m_new, alpha, p), (_, acc_t), v_t in zip(stats, carry, v_ts))


def _attend(scores, values, loop_mask, last_mask, kt0, kt_last, n):
    t = ATT_TILE
    init = tuple((jnp.full((1, t), NEG_INF, F32), jnp.zeros((VT_ROWS, t), F32)) for _ in range(n))

    def body(kt, carry):
        mask = None if loop_mask is None else loop_mask(kt)
        return _online_t(scores(kt), mask, carry, values(kt))

    carry = lax.fori_loop(kt0, kt_last, body, init)
    return _normalise(_online_t(scores(kt_last), last_mask, carry, values(kt_last)))


def _normalise(carry):
    return [acc[0:LANES] / acc[LANES:LANES + 1] for (_, acc) in carry]


def _causal_mask_t(t):
    return lax.broadcasted_iota(jnp.int32, (t, t), 0) <= lax.broadcasted_iota(jnp.int32, (t, t), 1)


def _diff_kernel(qt_ref, k_ref, vt_ref, lam_ref, gains_ref, o_ref, *, lam_init):
    i = pl.program_id(2)
    t = ATT_TILE
    nh = qt_ref.shape[0]
    drow = lax.broadcasted_iota(jnp.int32, (LANES, t), 0)
    qts = []
    for h in range(nh):
        q = qt_ref[h, 0]
        qts.append(jnp.where(drow < DA_QK_DIM, q, jnp.zeros_like(q)))
        qts.append(jnp.where(drow < DA_QK_DIM, jnp.zeros_like(q), q))

    def scores(kt):
        off = pl.multiple_of(kt * t, t)
        ks = [k_ref[pl.ds(off, t), h * LANES:(h + 1) * LANES] for h in range(nh)]
        return tuple(_dot(ks[c // 2], qts[c]) for c in range(2 * nh))

    def values(kt):
        return [vt_ref[c // 2, kt] for c in range(2 * nh)]

    outs = _attend(scores, values, None, _causal_mask_t(t), 0, i, 2 * nh)

    lp = lam_ref[...]
    lam = (jnp.exp(jnp.sum(lp[0:1] * lp[1:2], axis=-1, keepdims=True))
           - jnp.exp(jnp.sum(lp[2:3] * lp[3:4], axis=-1, keepdims=True)) + lam_init)
    for h in range(nh):
        o = (outs[2 * h] - lam * outs[2 * h + 1]).T
        o = _rms_rows(o, gains_ref[G_DAOUT:G_DAOUT + 1, :]) * (1.0 - lam_init)
        o_ref[:, h * LANES:(h + 1) * LANES] = o.astype(BF16)


def _diff_attention(qat, ka, vat, da_lambda, gains, *, batch, seq, lam_init, heads_per_step=4):
    t = ATT_TILE
    nq = seq // t
    T = batch * seq
    hp = heads_per_step
    return pl.pallas_call(
        functools.partial(_diff_kernel, lam_init=lam_init),
        grid=(batch, DA_HEADS // hp, nq),
        in_specs=[
            pl.BlockSpec((hp, 1, LANES, t), lambda b, p, i: (p, b * nq + i, 0, 0)),
            pl.BlockSpec((seq, hp * LANES), lambda b, p, i: (b, p)),
            pl.BlockSpec((hp, nq, VT_ROWS, t), lambda b, p, i: (p, b, 0, 0)),
            pl.BlockSpec(da_lambda.shape, lambda b, p, i: (0, 0)),
            pl.BlockSpec(gains.shape, lambda b, p, i: (0, 0)),
        ],
        out_specs=pl.BlockSpec((t, hp * LANES), lambda b, p, i: (b * nq + i, p)),
        out_shape=jax.ShapeDtypeStruct((T, DA_HEADS * LANES), BF16),
        compiler_params=_cparams(("parallel", "parallel", "arbitrary")),
        name="diff_attn",
    )(qat, ka, vat, da_lambda, gains)


def _fox_kernel(qt_ref, k_ref, vt_ref, o_ref):
    i = pl.program_id(1)
    t = ATT_TILE
    nh, bs = qt_ref.shape[0], qt_ref.shape[1]
    chains = [(b, h) for b in range(bs) for h in range(nh)]
    qts = [qt_ref[h, b, 0] for b, h in chains]

    def scores(kt):
        off = pl.multiple_of(kt * t, t)
        return tuple(_dot(k_ref[b, pl.ds(off, t), h * FOX_AUG:(h + 1) * FOX_AUG], q)
                     for (b, h), q in zip(chains, qts))

    def values(kt):
        return [vt_ref[h, b, kt] for b, h in chains]

    outs = _attend(scores, values, None, _causal_mask_t(t), 0, i, len(chains))
    for (b, h), o in zip(chains, outs):
        o_ref[b, :, h * LANES:(h + 1) * LANES] = o.T.astype(BF16)


def _fox_attention(qct, kf, vct, *, batch, seq):
    t = ATT_TILE
    nq = seq // t
    nh = FOX_HEADS
    bs = 4 if batch % 4 == 0 else (2 if batch % 2 == 0 else 1)
    nbp = batch // bs
    out = pl.pallas_call(
        _fox_kernel,
        grid=(nbp, nq),
        in_specs=[
            pl.BlockSpec((nh, None, bs, 1, FOX_AUG, t), lambda p, i: (0, p, 0, i, 0, 0)),
            pl.BlockSpec((None, bs, seq, nh * FOX_AUG), lambda p, i: (p, 0, 0, 0)),
            pl.BlockSpec((nh, None, bs, nq, VT_ROWS, t), lambda p, i: (0, p, 0, 0, 0, 0)),
        ],
        out_specs=pl.BlockSpec((None, bs, t, nh * LANES), lambda p, i: (p, 0, i, 0)),
        out_shape=jax.ShapeDtypeStruct((nbp, bs, seq, nh * LANES), BF16),
        compiler_params=_cparams(("parallel", "arbitrary")),
        name="fox_attn",
    )(qct.reshape(nh, nbp, bs, nq, FOX_AUG, t), kf.reshape(nbp, bs, seq, nh * FOX_AUG),
      vct.reshape(nh, nbp, bs, nq, VT_ROWS, t))
    return out.reshape(batch * seq, nh * LANES)


def _nsa_kernel(qt_ref, ks_ref, vst_ref, kw_ref, vwt_ref, kc_ref, vct_ref, gt_ref, o_ref, *, n_sel, nb):
    i = pl.program_id(1)
    t = ATT_TILE
    lo = i * t
    nh = qt_ref.shape[0]
    hpg = nh // NSA_GROUPS
    qts = [qt_ref[h, 0] for h in range(nh)]
    tq_row = lo + lax.broadcasted_iota(jnp.int32, (1, t), 1)
    nidx = lax.broadcasted_iota(jnp.int32, (nb, t), 0)
    nidx_f = nidx.astype(F32)
    cmask = nidx * NSA_BLOCK + (NSA_BLOCK - 1) <= tq_row
    cur = tq_row >> BLOCK_SHIFT
    forced = (nidx == 0) | (nidx == cur) | (nidx == cur - 1)
    zpad = jnp.zeros((LANES - nb, t), F32)

    def lane_tile(v):
        return jnp.concatenate([v, zpad], axis=0).astype(BF16)

    o_cmp, sel_b = [], []
    for g in range(NSA_GROUPS):
        kc, vct = kc_ref[g, 0:nb, :], vct_ref[g]
        importance = jnp.zeros((nb, t), F32)
        for h in range(g * hpg, (g + 1) * hpg):
            s = jnp.where(cmask, _dot(kc, qts[h]), NEG_INF)
            e = jnp.where(cmask, jnp.exp2(s - jnp.max(s, axis=0, keepdims=True)), 0.0)
            p = e / jnp.maximum(jnp.sum(e, axis=0, keepdims=True), 1e-30)
            importance = importance + p
            o_cmp.append(_dot(vct, lane_tile(p)))

        score = jnp.where(forced, NSA_FORCED_SCORE, jnp.where(nidx <= cur, importance, -1.0))
        sel = jnp.zeros((nb, t), F32)
        for _ in range(n_sel):
            best = jnp.max(score, axis=0, keepdims=True)
            first = jnp.min(jnp.where(score == best, nidx_f, float(nb)), axis=0, keepdims=True)
            hit = nidx_f == first
            sel = jnp.where(hit, 1.0, sel)
            score = jnp.where(hit, -3e38, score)
        sel_b.append(lane_tile(sel))

    e_key = lax.broadcasted_iota(jnp.int32, (t, LANES), 0) >> BLOCK_SHIFT
    e_blk = lax.broadcasted_iota(jnp.int32, (t, LANES), 1)
    krow = lax.broadcasted_iota(jnp.int32, (t, t), 0)
    qcol = lax.broadcasted_iota(jnp.int32, (t, t), 1)
    causal = krow <= qcol

    def scores_of(k_ref):
        def scores(kt):
            off = pl.multiple_of(kt * t, t)
            ks = [k_ref[pl.ds(off, t), g * LANES:(g + 1) * LANES] for g in range(NSA_GROUPS)]
            return tuple(_dot(ks[h // hpg], qts[h]) for h in range(nh))
        return scores

    def values_of(vt_ref):
        return lambda kt: [vt_ref[h // hpg, kt] for h in range(nh)]

    def selected(kt, extra=None):
        expand = jnp.where(e_blk == e_key + kt * (t // NSA_BLOCK), 1.0, 0.0).astype(BF16)
        per_group = [_dot(expand, sb) > 0.5 for sb in sel_b]
        if extra is not None:
            per_group = [m & extra for m in per_group]
        return [per_group[h // hpg] for h in range(nh)]

    o_slc = _attend(scores_of(ks_ref), values_of(vst_ref), selected, selected(i, causal), 0, i, nh)

    wt = NSA_WINDOW // t
    win_scores, win_values = scores_of(kw_ref), values_of(vwt_ref)
    upper = krow > qcol
    far = jnp.maximum(i - wt, 0)
    penalty = jnp.where(i >= wt, 0.0, NEG_INF)
    s_far, s_diag = win_scores(far), win_scores(i)
    stats = []
    for c in range(nh):
        s = jnp.where(upper, s_far[c] + penalty, s_diag[c])
        m = jnp.max(s, axis=0, keepdims=True)
        stats.append((m, jnp.exp2(s - m).astype(BF16)))
    v_far, v_diag = win_values(far), win_values(i)
    carry = tuple((m, _dot(v_far[c], jnp.where(upper, p, jnp.zeros_like(p)))
                   + _dot(v_diag[c], jnp.where(upper, jnp.zeros_like(p), p)))
                  for c, (m, p) in enumerate(stats))
    carry = lax.fori_loop(jnp.maximum(i - wt + 1, 0), i,
                          lambda kt, cr: _online_t(win_scores(kt), None, cr, win_values(kt)), carry)
    o_win = _normalise(carry)

    for h in range(nh):
        g, r = h // hpg, 3 * (h % hpg)
        gt = jax.nn.sigmoid(gt_ref[(1 + g) * LANES + r:(1 + g) * LANES + r + 3, :])
        o = gt[0:1] * o_cmp[h] + gt[1:2] * o_slc[h] + gt[2:3] * o_win[h]
        o_ref[:, h * LANES:(h + 1) * LANES] = o.T.astype(BF16)


def _nsa_attention(qbt, ks, kw, vst, vwt, kc, vct, small_t, *, batch, seq):
    t = ATT_TILE
    nq = seq // t
    T = batch * seq
    n_sel = min(NSA_TOP_N, seq // NSA_BLOCK)
    G = NSA_GROUPS
    k_spec = pl.BlockSpec((seq, G * LANES), lambda b, i: (b, 0))
    vt_spec = pl.BlockSpec((G, nq, VT_ROWS, t), lambda b, i: (0, b, 0, 0))
    c_spec = pl.BlockSpec((G, None, LANES, LANES), lambda b, i: (0, b, 0, 0))
    return pl.pallas_call(
        functools.partial(_nsa_kernel, n_sel=n_sel, nb=seq // NSA_BLOCK),
        grid=(batch, nq),
        in_specs=[
            pl.BlockSpec((NSA_HEADS, 1, LANES, t), lambda b, i: (0, b * nq + i, 0, 0)),
            k_spec, vt_spec, k_spec, vt_spec, c_spec, c_spec,
            pl.BlockSpec((SMALL_COLS, t), lambda b, i: (0, b * nq + i)),
        ],
        out_specs=pl.BlockSpec((t, NSA_HEADS * LANES), lambda b, i: (b * nq + i, 0)),
        out_shape=jax.ShapeDtypeStruct((T, NSA_HEADS * LANES), BF16),
        compiler_params=_cparams(("parallel", "arbitrary")),
        name="nsa_attn",
    )(qbt, ks, vst, kw, vwt, kc, vct, small_t)


def _merge_kernel(x_ref, oa_ref, ob_ref, oc_ref, gm_ref, wa_ref, wb_ref, wc_ref, wo_ref, o_ref, *, gm_off):
    D = x_ref.shape[1]
    g0, g1, g2 = (gm_ref[:, gm_off + n * D:gm_off + (n + 1) * D].astype(F32) for n in range(3))
    y = (g0 * _dot(oa_ref[...], wa_ref[...]) + g1 * _dot(ob_ref[...], wb_ref[...])
         + g2 * _dot(oc_ref[...], wc_ref[...]))
    o_ref[...] = x_ref[...] + _dot(y.astype(BF16), wo_ref[...])


def _merge(x2d, oa, ob, oc, p_gm, gm_off, wa, wb, wc, wo, layer, *, tm=512):
    T, D = x2d.shape

    def rows(w, c=0):
        return pl.BlockSpec((tm, w), lambda i: (i, c))

    def full(w):
        return pl.BlockSpec((None,) + w.shape[1:], lambda i: (layer, 0, 0), pipeline_mode=pl.Buffered(1))

    return pl.pallas_call(
        functools.partial(_merge_kernel, gm_off=gm_off),
        grid=(T // tm,),
        in_specs=[rows(D), rows(oa.shape[1]), rows(ob.shape[1]), rows(oc.shape[1]),
                  rows(p_gm.shape[1]), full(wa), full(wb), full(wc), full(wo)],
        out_specs=rows(D),
        out_shape=jax.ShapeDtypeStruct((T, D), F32),
        compiler_params=_cparams(("parallel",), VMEM_LIMIT_FFN),
        name="merge",
    )(x2d, oa, ob, oc, p_gm, wa, wb, wc, wo)


def _rope_tables(positions, width, rot_dim):
    inv_freq = ROPE_THETA ** (-jnp.arange(0, rot_dim, 2, dtype=F32) / rot_dim)
    ang = positions.astype(F32).reshape(-1, 1) * inv_freq
    cos, sin = jnp.cos(ang), jnp.sin(ang)
    T = ang.shape[0]
    rest = width - rot_dim
    c = jnp.concatenate([cos, cos, jnp.ones((T, rest), F32)], axis=1)
    s = jnp.concatenate([-sin, sin, jnp.zeros((T, rest), F32)], axis=1)
    rep = LANES // width
    return [jnp.tile(t, (1, rep)) for t in (c, s)]


def _pad_lanes(v):
    return jnp.pad(v.astype(F32), (0, LANES - v.shape[0]))


def kernel(x, positions, ffn1_norm, ffn1_w_gate, ffn1_w_up, ffn1_w_down, mix_norm, w_in, da_q_norm, da_k_norm, da_lambda, da_out_norm, nsa_q_norm, nsa_k_norm, nsa_cmp_pos, nsa_cmp_w1, nsa_cmp_w2, fox_q_norm, fox_k_norm, fox_f_bias, w_branch_a, w_branch_b, w_branch_c, w_out, ffn2_norm, ffn2_w_gate, ffn2_w_up, ffn2_w_down):
    B, S, D = x.shape
    depth = w_in.shape[0]
    T = B * S
    G, L, d = NSA_GROUPS, NSA_BLOCK, HEAD_DIM
    nb = S // L
    assert S % ATT_TILE == 0 and nb <= LANES and nb % 8 == 0 and NSA_WINDOW % ATT_TILE == 0

    rope_tab = jnp.stack(_rope_tables(positions, DA_QK_DIM, DA_QK_DIM // ROPE_FRACTION)
                         + _rope_tables(positions, HEAD_DIM, HEAD_DIM // ROPE_FRACTION))

    widths = [512, 512, 512, 1024, 1536, 24, 512, 512, 512, 4, 3 * D]
    offs = [0]
    for w in widths:
        offs.append(offs[-1] + w)
    (o_aq, o_ak, o_av, o_bq, o_bkv, o_bg, o_cq, o_ck, o_cv, o_cf, o_gm, o_end) = offs

    w_qkv = jnp.concatenate([w_in[:, :, o_aq:o_bg], w_in[:, :, o_cq:o_cf]], axis=2).astype(BF16)
    gm_lo = o_gm // LANES * LANES
    gm_off = o_gm - gm_lo
    gm_tile = 10 * LANES
    gm_cols = -(-(o_end - gm_lo) // LANES) * LANES
    w_gm = jnp.pad(w_in[:, :, gm_lo:o_end], ((0, 0), (0, 0), (0, gm_cols - (o_end - gm_lo)))).astype(BF16)
    zpad = lambda n: jnp.zeros((depth, D, n), F32)
    w_small = jnp.concatenate([
        w_in[:, :, o_cf:o_gm], zpad(LANES - FOX_HEADS),
        w_in[:, :, o_bg:o_bg + 12], zpad(LANES - 12),
        w_in[:, :, o_bg + 12:o_bg + 24], zpad(LANES - 12)], axis=2)
    w_small_hi = w_small.astype(BF16)
    w_small_lo = (w_small - w_small_hi.astype(F32)).astype(BF16)
    ffn_f32 = [(ffn1_w_gate, ffn1_w_up, ffn1_w_down), (ffn2_w_gate, ffn2_w_up, ffn2_w_down)]
    ffn_g = [ffn1_norm.reshape(depth, 1, D), ffn2_norm.reshape(depth, 1, D)]
    ffn_next = [(k // 2, *ffn_f32[k % 2]) for k in range(1, 2 * depth)] + [None]
    ffn_w = tuple(_to_bf16(w, rows=256, layer=0) for w in ffn_f32[0])
    g_mix = mix_norm.reshape(depth, 1, D)
    w_a, w_b, w_c, w_o = (_to_bf16(w, rows=512) for w in (w_branch_a, w_branch_b, w_branch_c, w_out))
    cmp_pos = nsa_cmp_pos.reshape(depth, 2, 1, L * d)

    xc = x.reshape(T, D)
    for l in range(depth):
        lam_init = 0.8 - 0.6 * math.exp(-0.3 * l)
        gains = jnp.stack([
            jnp.tile(da_q_norm[l], 2), jnp.tile(da_k_norm[l], 2), nsa_q_norm[l],
            nsa_k_norm[l, 0], nsa_k_norm[l, 1], nsa_k_norm[l, 2], fox_q_norm[l], fox_k_norm[l],
            _pad_lanes(fox_f_bias[l]), da_out_norm[l]] + [jnp.zeros((LANES,), F32)] * 6).astype(F32)

        xc, ffn_w = _ffn(xc, ffn_g[0], *ffn_w, l, ffn_next[2 * l])

        p_qkv = _proj(xc, g_mix, w_qkv, l, gate=False, tn=QKV_COLS // 4)
        p_gm = _proj(xc, g_mix, w_gm, l, gate=True, tn=gm_tile)
        small, small_t = _proj_small(xc, g_mix, w_small_hi, w_small_lo, l)

        (qat, ka, vat, qbt, ks, kw, vst, vwt, qct, kf, vct, kv_cmp) = _prep(
            p_qkv, small, rope_tab, gains, seq=S)

        kc, vcmp_t = _compress(kv_cmp.reshape(2, G, B * nb, L * d), cmp_pos, nsa_cmp_w1, nsa_cmp_w2,
                               nsa_k_norm[l, 0].reshape(1, d), l, batch=B)

        o_a = _diff_attention(qat, ka, vat, da_lambda[l], gains, batch=B, seq=S, lam_init=lam_init)
        o_b = _nsa_attention(qbt, ks, kw, vst, vwt, kc, vcmp_t, small_t, batch=B, seq=S)
        o_c = _fox_attention(qct, kf, vct, batch=B, seq=S)

        xc = _merge(xc, o_a, o_b, o_c, p_gm, gm_off, w_a, w_b, w_c, w_o, l)

        xc, ffn_w = _ffn(xc, ffn_g[1], *ffn_w, l, ffn_next[2 * l + 1])
    return xc.reshape(B, S, D)
```
